```python
import math
import jax
import jax.numpy as jnp
from jax import lax
import numpy as np

D_MODEL = 1024
BATCH = 8
SEQ = 4096
DEPTH = 4

GRID_W = 64
CTX_LEN = 256
N_MIXERS = 4
GROUP_W = D_MODEL // N_MIXERS
HEAD_DIM = 64
N_HEADS = GROUP_W // HEAD_DIM
CONV_W = 4
CONV_PAD = (CONV_W // 2, CONV_W - 1 - CONV_W // 2)
LRU_C = 8.0
RWKV_LORA_W = 32
RWKV_LORA_A = 32
RWKV_LORA_G = 64
RWKV_GN_EPS = 64e-5
CHUNK = 64
D_FF = 2816
N_EXPERTS = 8
TOP_K = 2
D_FF_EXPERT = 1408
MOE_BLOCK = 256
N_DENSE = (DEPTH + 1) // 2
N_MOE = DEPTH // 2
EPS = 1e-6

A_COLS = 2 * GROUP_W
B_COLS = 3 * GROUP_W + 2 * RWKV_LORA_W + 2 * RWKV_LORA_A + RWKV_LORA_G
C_COLS = 4 * GROUP_W + 4 * N_HEADS
D_COLS = 4 * GROUP_W + 4 * N_HEADS
OFF_B = A_COLS
OFF_C = OFF_B + B_COLS
OFF_D = OFF_C + C_COLS
IN_COLS = OFF_D + D_COLS

kernel_name = 'hybrid_group_diffusion_trunk'


def rms_norm(x, g):
    xf = x.astype(jnp.float32)
    y = xf * lax.rsqrt(jnp.mean(xf * xf, axis=-1, keepdims=True) + EPS)
    return (y * g.astype(jnp.float32)).astype(x.dtype)


def heads(t):
    return t.reshape(*t.shape[:-1], N_HEADS, HEAD_DIM)


def merge_heads(t):
    return t.reshape(*t.shape[:-2], GROUP_W)


def head_rms_norm(t, g):
    return merge_heads(rms_norm(heads(t), heads(g)))


def l2_normalize(t):
    t = t.astype(jnp.float32)
    return t * lax.rsqrt(jnp.sum(t * t, axis=-1, keepdims=True) + EPS)


def centred_dwconv(t, w):
    return lax.conv_general_dilated(t, w[:, None, :].astype(t.dtype), (1,), [CONV_PAD],
                                    dimension_numbers=('NWC', 'WIO', 'NWC'),
                                    feature_group_count=t.shape[-1])


def token_shift_bidir(t, mu):
    prev = jnp.pad(t[:, :-1], ((0, 0), (1, 0), (0, 0)))
    nxt = jnp.pad(t[:, 1:], ((0, 0), (0, 1), (0, 0)))
    return t + mu[0] * (prev - t) + mu[1] * (nxt - t)


def bidir_stack(t):
    return jnp.stack([t, jnp.flip(t, axis=1)], axis=0)


def flip_dir1(t):
    return jnp.stack([t[0], jnp.flip(t[1], axis=1)], axis=0)


def bidir_merge(t):
    return t[0] + jnp.flip(t[1], axis=1)


def raster_to_colmajor(t, rows):
    b, s, ch = t.shape
    return t.reshape(b, rows, GRID_W, ch).transpose(0, 2, 1, 3).reshape(b, s, ch)


def colmajor_to_raster(t, rows):
    b, s, ch = t.shape
    return t.reshape(b, GRID_W, rows, ch).transpose(0, 2, 1, 3).reshape(b, s, ch)


def to_chunks(t):
    nd, bsz, s = t.shape[:3]
    t = t.reshape(nd, bsz, s // CHUNK, CHUNK, *t.shape[3:])
    return jnp.moveaxis(jnp.moveaxis(t, 4, 3), 2, 0)


def from_chunks(t):
    nc, nd, bsz, h, l, d = t.shape
    return jnp.swapaxes(jnp.moveaxis(t, 0, 2), 3, 4).reshape(nd, bsz, nc * l, h, d)


def linear_scan(a, b, h0):
    b = b.at[:, :, 0].add(a[:, :, 0] * h0)

    def combine(lhs, rhs):
        a_l, b_l = lhs
        a_r, b_r = rhs
        return a_l * a_r, a_r * b_l + b_r

    return lax.associative_scan(combine, (a, b), axis=2)[1]


def rglru_mixer(pc, pl, conv_w, conv_b, w_a, b_a, w_x, b_x, lam, ctx_out):
    f32 = jnp.float32

    def gates(u):
        ud = bidir_stack(u.astype(f32))
        uh = heads(ud)
        r = jax.nn.sigmoid(merge_heads(jnp.einsum('dbshi,dhij->dbshj', uh, w_a)) + b_a[:, None, None])
        i = jax.nn.sigmoid(merge_heads(jnp.einsum('dbshi,dhij->dbshj', uh, w_x)) + b_x[:, None, None])
        log_a = -LRU_C * r * jax.nn.softplus(-lam)[:, None, None]
        return jnp.exp(log_a), jnp.sqrt(-jnp.expm1(2.0 * log_a)) * (i * ud)

    a_c, b_c = gates(centred_dwconv(pc[..., :GROUP_W], conv_w) + conv_b)
    h_c = linear_scan(a_c, b_c, jnp.zeros_like(b_c[:, :, 0]))
    a_l, b_l = gates(centred_dwconv(pl[..., :GROUP_W], conv_w) + conv_b)
    h_l = linear_scan(a_l, b_l, h_c[:, :, -1])
    y_l = (bidir_merge(h_l) * jax.nn.gelu(pl[..., GROUP_W:].astype(f32))).astype(pl.dtype)
    y_c = (bidir_merge(h_c) * jax.nn.gelu(pc[..., GROUP_W:].astype(f32))).astype(pc.dtype) if ctx_out else None
    return y_c, y_l


def rwkv7_scan(r, w, kk, a, k, v, s0):
    xs = tuple(jnp.moveaxis(t, 2, 0) for t in (r, w, kk, a, k, v))

    def step(s, inp):
        r_t, w_t, kk_t, a_t, k_t, v_t = inp
        sk = jnp.einsum('...vk,...k->...v', s, kk_t)
        s = (s * w_t[..., None, :] - sk[..., :, None] * (kk_t * a_t)[..., None, :]
             + v_t[..., :, None] * k_t[..., None, :])
        return s, jnp.einsum('...vk,...k->...v', s, r_t)

    s_final, ys = lax.scan(step, s0, xs)
    return jnp.moveaxis(ys, 0, 2), s_final


def rwkv7_mixer(pc, pl, mu, w_up, w0, a_up, a0, g_up, k_k, k_a, r_k, ln_g, ln_b, ctx_out):
    f32 = jnp.float32
    G = GROUP_W

    def prepare(p):
        p = token_shift_bidir(p.astype(f32), mu)
        bsz, s = p.shape[:2]
        r, k, v = p[..., :G], p[..., G:2 * G], p[..., 2 * G:3 * G]
        off = 3 * G
        xw = p[..., off:off + 2 * RWKV_LORA_W].reshape(bsz, s, 2, RWKV_LORA_W)
        off += 2 * RWKV_LORA_W
        xa = p[..., off:off + 2 * RWKV_LORA_A].reshape(bsz, s, 2, RWKV_LORA_A)
        xg = p[..., off + 2 * RWKV_LORA_A:]
        log_w = -math.exp(-0.5) * jax.nn.sigmoid(
            jnp.einsum('bsdr,drc->dbsc', jnp.tanh(xw), w_up) + w0[:, None, None])
        a = jax.nn.sigmoid(jnp.einsum('bsdr,drc->dbsc', xa, a_up) + a0[:, None, None])
        kk = l2_normalize(heads(k * k_k))
        k_mod = k * (1.0 + (a - 1.0) * k_a)
        bonus = jnp.sum(jnp.sum(heads(r * k_mod * r_k), axis=-1, keepdims=True), axis=0) * heads(v)
        g = jax.nn.sigmoid(xg) @ g_up
        ins = (bidir_stack(heads(r)), flip_dir1(heads(jnp.exp(log_w))), bidir_stack(kk),
               flip_dir1(heads(a)), flip_dir1(heads(k_mod)), bidir_stack(heads(v)))
        return ins, bonus, g

    def finish(ys, bonus, g):
        y = bidir_merge(ys)
        mean = jnp.mean(y, axis=-1, keepdims=True)
        var = jnp.mean(jnp.square(y - mean), axis=-1, keepdims=True)
        y = merge_heads((y - mean) * lax.rsqrt(var + RWKV_GN_EPS)) * ln_g + ln_b
        return (y + merge_heads(bonus)) * g

    ins_c, bonus_c, g_c = prepare(pc)
    ins_l, bonus_l, g_l = prepare(pl)
    bsz = pc.shape[0]
    s0 = jnp.zeros((2, bsz, N_HEADS, HEAD_DIM, HEAD_DIM), f32)
    ys_c, s_c = rwkv7_scan(*ins_c, s0)
    ys_l, _ = rwkv7_scan(*ins_l, s_c)
    y_l = finish(ys_l, bonus_l, g_l).astype(pl.dtype)
    y_c = finish(ys_c, bonus_c, g_c).astype(pc.dtype) if ctx_out else None
    return y_c, y_l


def mlstm_chunked(q, k, v, log_i, log_f, state):
    qc, kc, vc = to_chunks(q), to_chunks(k), to_chunks(v)
    li, lf = to_chunks(log_i), to_chunks(log_f)
    causal = jnp.tril(jnp.ones((CHUNK, CHUNK), dtype=bool))
    b = jnp.cumsum(lf, axis=-1)
    log_d = jnp.where(causal, b[..., :, None] - b[..., None, :] + li[..., None, :], -jnp.inf)
    m_intra = jnp.max(log_d, axis=-1)
    b_last = b[..., -1]
    log_end = b_last[..., None] - b + li
    qk = jnp.einsum('...ld,...sd->...ls', qc, kc)

    def step(carry, inp):
        mem, nrm, m = carry
        q_, k_, v_, qk_, ld_, mi_, b_, bl_, le_ = inp
        m_inter = b_ + m[..., None]
        m_t = jnp.maximum(m_inter, mi_)
        w_inter = jnp.exp(m_inter - m_t)
        p = jnp.exp(ld_ - m_t[..., None]) * qk_
        num = (jnp.einsum('...ls,...sd->...ld', p, v_)
               + w_inter[..., None] * jnp.einsum('...vk,...lk->...lv', mem, q_))
        den = jnp.sum(p, axis=-1) + w_inter * jnp.einsum('...k,...lk->...l', nrm, q_)
        h = num / jnp.maximum(jnp.abs(den), jnp.exp(-m_t))[..., None]
        m_new = jnp.maximum(bl_ + m, jnp.max(le_, axis=-1))
        w_prev = jnp.exp(bl_ + m - m_new)
        w_new = jnp.exp(le_ - m_new[..., None])
        mem = w_prev[..., None, None] * mem + jnp.einsum('...l,...lv,...lk->...vk', w_new, v_, k_)
        nrm = w_prev[..., None] * nrm + jnp.einsum('...l,...lk->...k', w_new, k_)
        return (mem, nrm, m_new), h

    state, hs = lax.scan(step, state, (qc, kc, vc, qk, log_d, m_intra, b, b_last, log_end))
    return from_chunks(hs), state


def mlstm_mixer(pc, pl, i_b, f_b, norm_g, rows, ctx_out):
    f32 = jnp.float32
    G = GROUP_W

    def prepare(p):
        p = p.astype(f32)
        bsz, s = p.shape[:2]
        q, k, v, o = p[..., :G], p[..., G:2 * G], p[..., 2 * G:3 * G], p[..., 3 * G:4 * G]
        gates = jnp.moveaxis(p[..., 4 * G:].reshape(bsz, s, 4, N_HEADS), 2, 0)
        log_i = gates[:2] + i_b[:, None, None]
        log_f = jax.nn.log_sigmoid(gates[2:] + f_b[:, None, None])
        ins = (bidir_stack(heads(q)), bidir_stack(heads(k) / math.sqrt(HEAD_DIM)),
               bidir_stack(heads(v)), flip_dir1(log_i), flip_dir1(log_f))
        return ins, o

    def finish(hs, o):
        return head_rms_norm(merge_heads(bidir_merge(hs)), norm_g) * jax.nn.sigmoid(o)

    ins_c, o_c = prepare(pc)
    ins_l, o_l = prepare(raster_to_colmajor(pl, rows))
    bsz = pc.shape[0]
    state0 = (jnp.zeros((2, bsz, N_HEADS, HEAD_DIM, HEAD_DIM), f32),
              jnp.zeros((2, bsz, N_HEADS, HEAD_DIM), f32),
              jnp.zeros((2, bsz, N_HEADS), f32))
    h_c, st_c = mlstm_chunked(*ins_c, state0)
    h_l, _ = mlstm_chunked(*ins_l, st_c)
    y_l = colmajor_to_raster(finish(h_l, o_l), rows).astype(pl.dtype)
    y_c = finish(h_c, o_c).astype(pc.dtype) if ctx_out else None
    return y_c, y_l


def gdn_chunked(q, k, v, log_alpha, beta, state):
    qc, kc, vc = to_chunks(q), to_chunks(k), to_chunks(v)
    la, bt = to_chunks(log_alpha), to_chunks(beta)
    incl = jnp.tril(jnp.ones((CHUNK, CHUNK), dtype=bool))
    strict = jnp.tril(jnp.ones((CHUNK, CHUNK), dtype=bool), k=-1)
    gam = jnp.cumsum(la, axis=-1)
    decay = jnp.exp(jnp.where(incl, gam[..., :, None] - gam[..., None, :], -jnp.inf))
    k_beta = kc * bt[..., None]
    lower = jnp.where(strict, jnp.einsum('...id,...jd->...ij', k_beta, kc) * decay, 0.0)
    rhs = jnp.concatenate([vc * bt[..., None], k_beta * jnp.exp(gam)[..., None]], axis=-1)
    sol = lax.linalg.triangular_solve(lower + jnp.eye(CHUNK, dtype=lower.dtype), rhs,
                                      left_side=True, lower=True, unit_diagonal=True)
    u, w = sol[..., :HEAD_DIM], sol[..., HEAD_DIM:]
    qk = jnp.einsum('...id,...jd->...ij', qc, kc) * decay
    q_dec = qc * jnp.exp(gam)[..., None]
    k_dec = kc * jnp.exp(gam[..., -1:] - gam)[..., None]
    g_last = jnp.exp(gam[..., -1])

    def step(s, inp):
        u_, w_, qk_, qd_, kd_, gl_ = inp
        v_new = u_ - jnp.einsum('...lk,...kv->...lv', w_, s)
        o = jnp.einsum('...lk,...kv->...lv', qd_, s) + jnp.einsum('...ls,...sv->...lv', qk_, v_new)
        s = gl_[..., None, None] * s + jnp.einsum('...lk,...lv->...kv', kd_, v_new)
        return s, o

    state, os_ = lax.scan(step, state, (u, w, qk, q_dec, k_dec, g_last))
    return from_chunks(os_), state


def gdn_mixer(pc, pl, conv_w, a_log, dt_bias, norm_g, rows, ctx_out):
    f32 = jnp.float32
    G = GROUP_W

    def prepare(p):
        bsz, s = p.shape[:2]
        qkv = jax.nn.silu(centred_dwconv(p[..., :3 * G], conv_w)).astype(f32)
        q = l2_normalize(heads(qkv[..., :G])) * HEAD_DIM ** -0.5
        k = l2_normalize(heads(qkv[..., G:2 * G]))
        v = heads(qkv[..., 2 * G:])
        ab = jnp.moveaxis(p[..., 4 * G:].astype(f32).reshape(bsz, s, 4, N_HEADS), 2, 0)
        log_alpha = -jnp.exp(a_log)[:, None, None] * jax.nn.softplus(ab[:2] + dt_bias[:, None, None])
        beta = jax.nn.sigmoid(ab[2:])
        ins = (bidir_stack(q), bidir_stack(k), bidir_stack(v), flip_dir1(log_alpha), flip_dir1(beta))
        return ins, p[..., 3 * G:4 * G]

    def finish(os_, gate):
        return head_rms_norm(merge_heads(bidir_merge(os_)), norm_g) * jax.nn.silu(gate.astype(f32))

    ins_c, gate_c = prepare(pc)
    ins_l, gate_l = prepare(raster_to_colmajor(pl, rows))
    bsz = pc.shape[0]
    s0 = jnp.zeros((2, bsz, N_HEADS, HEAD_DIM, HEAD_DIM), f32)
    o_c, s_c = gdn_chunked(*ins_c, s0)
    o_l, _ = gdn_chunked(*ins_l, s_c)
    y_l = colmajor_to_raster(finish(o_l, gate_l), rows).astype(pl.dtype)
    y_c = finish(o_c, gate_c).astype(pc.dtype) if ctx_out else None
    return y_c, y_l


def swiglu(h, w_gate, w_up, w_down):
    return (jax.nn.silu(h @ w_gate) * (h @ w_up)) @ w_down


def moe_swiglu(h, router, w_gate, w_up, w_down):
    t, d = h.shape
    logits = (h @ router).astype(jnp.float32)
    top_logits, top_idx = lax.top_k(logits, TOP_K)
    gate = jax.nn.softmax(top_logits, axis=-1)
    tk = t * TOP_K
    flat_e = top_idx.reshape(tk)
    flat_tok = jnp.arange(tk, dtype=jnp.int32) // TOP_K
    order = jnp.argsort(flat_e)
    se, stok, sw = flat_e[order], flat_tok[order], gate.reshape(tk)[order]
    counts = jnp.zeros((N_EXPERTS,), jnp.int32).at[flat_e].add(1)
    start = jnp.cumsum(counts) - counts
    padded = (counts + MOE_BLOCK - 1) // MOE_BLOCK * MOE_BLOCK
    pad_start = jnp.cumsum(padded) - padded
    pad_end = pad_start + padded
    pos = pad_start[se] + jnp.arange(tk, dtype=jnp.int32) - start[se]
    n_blocks = -(-tk // MOE_BLOCK) + N_EXPERTS
    buf_tok = jnp.full((n_blocks * MOE_BLOCK,), t, jnp.int32).at[pos].set(stok)
    buf_w = jnp.zeros((n_blocks * MOE_BLOCK,), h.dtype).at[pos].set(sw.astype(h.dtype))
    block_e = jnp.clip(jnp.searchsorted(pad_end, jnp.arange(n_blocks, dtype=jnp.int32) * MOE_BLOCK,
                                        side='right'), 0, N_EXPERTS - 1)
    h_pad = jnp.concatenate([h, jnp.zeros((1, d), h.dtype)], axis=0)

    def expert_block(args):
        tok, e = args
        xb = h_pad[tok]
        return (jax.nn.silu(xb @ w_gate[e]) * (xb @ w_up[e])) @ w_down[e]

    yb = lax.map(expert_block, (buf_tok.reshape(n_blocks, MOE_BLOCK), block_e))
    y = jnp.zeros((t + 1, d), h.dtype).at[buf_tok].add(yb.reshape(-1, d) * buf_w[:, None])
    return y[:t]


def setup_inputs(seed: int = 0) -> dict:
    key = jax.random.key(seed)
    keys = iter(jax.random.split(key, 64))

    def nrm(shape, scale):
        return scale * jax.random.normal(next(keys), shape, jnp.float32)

    def unif(shape, lo, hi):
        return jax.random.uniform(next(keys), shape, jnp.float32, lo, hi)

    G, H = GROUP_W, N_HEADS
    lru_p = unif((DEPTH, 2, G), 0.9, 0.999) ** (1.0 / LRU_C)
    dt = jnp.exp(unif((DEPTH, 2, H), math.log(1e-3), math.log(1e-1)))
    return {
        'x': nrm((BATCH, SEQ, D_MODEL), 1.0),
        'c': nrm((BATCH, D_MODEL), 1.0),
        'ctx': nrm((BATCH, CTX_LEN, D_MODEL), 1.0),
        'c_ctx': nrm((D_MODEL,), 1.0),
        'mod_w': nrm((DEPTH, D_MODEL, 6 * D_MODEL), 0.5 * D_MODEL ** -0.5),
        'mod_b': nrm((DEPTH, 6 * D_MODEL), 0.02),
        'norm_mix_g': 1.0 + nrm((DEPTH, D_MODEL), 0.02),
        'norm_ffn_g': 1.0 + nrm((DEPTH, D_MODEL), 0.02),
        'w_in': nrm((DEPTH, D_MODEL, IN_COLS), D_MODEL ** -0.5),
        'w_out': nrm((DEPTH, D_MODEL, D_MODEL), D_MODEL ** -0.5),
        'lru_conv_w': nrm((DEPTH, CONV_W, G), CONV_W ** -0.5),
        'lru_conv_b': nrm((DEPTH, G), 0.02),
        'lru_w_a': nrm((DEPTH, 2, H, HEAD_DIM, HEAD_DIM), HEAD_DIM ** -0.5),
        'lru_b_a': nrm((DEPTH, 2, G), 0.02),
        'lru_w_x': nrm((DEPTH, 2, H, HEAD_DIM, HEAD_DIM), HEAD_DIM ** -0.5),
        'lru_b_x': nrm((DEPTH, 2, G), 0.02),
        'lru_lambda': jnp.log(lru_p) - jnp.log1p(-lru_p),
        'rwkv_mu': unif((DEPTH, 2, B_COLS), 0.0, 0.5),
        'rwkv_w_up': nrm((DEPTH, 2, RWKV_LORA_W, G), 0.1),
        'rwkv_w0': unif((DEPTH, 2, G), -3.0, 2.0),
        'rwkv_a_up': nrm((DEPTH, 2, RWKV_LORA_A, G), 0.1),
        'rwkv_a0': nrm((DEPTH, 2, G), 0.1),
        'rwkv_g_up': nrm((DEPTH, RWKV_LORA_G, G), RWKV_LORA_G ** -0.5),
        'rwkv_k_k': 0.85 + nrm((DEPTH, G), 0.02),
        'rwkv_k_a': 1.0 + nrm((DEPTH, G), 0.02),
        'rwkv_r_k': nrm((DEPTH, G), 0.1),
        'rwkv_ln_g': 1.0 + nrm((DEPTH, G), 0.02),
        'rwkv_ln_b': nrm((DEPTH, G), 0.02),
        'mlstm_i_b': nrm((DEPTH, 2, H), 0.1),
        'mlstm_f_b': jnp.linspace(3.0, 6.0, H, dtype=jnp.float32) + nrm((DEPTH, 2, H), 0.05),
        'mlstm_norm_g': 1.0 + nrm((DEPTH, G), 0.02),
        'gdn_conv_w': nrm((DEPTH, CONV_W, 3 * G), CONV_W ** -0.5),
        'gdn_a_log': jnp.log(unif((DEPTH, 2, H), 1.0, 16.0)),
        'gdn_dt_bias': dt + jnp.log(-jnp.expm1(-dt)),
        'gdn_norm_g': 1.0 + nrm((DEPTH, G), 0.02),
        'ffn_w_gate': nrm((N_DENSE, D_MODEL, D_FF), D_MODEL ** -0.5),
        'ffn_w_up': nrm((N_DENSE, D_MODEL, D_FF), D_MODEL ** -0.5),
        'ffn_w_down': nrm((N_DENSE, D_FF, D_MODEL), D_FF ** -0.5),
        'moe_router': nrm((N_MOE, D_MODEL, N_EXPERTS), D_MODEL ** -0.5),
        'moe_w_gate': nrm((N_MOE, N_EXPERTS, D_MODEL, D_FF_EXPERT), D_MODEL ** -0.5),
        'moe_w_up': nrm((N_MOE, N_EXPERTS, D_MODEL, D_FF_EXPERT), D_MODEL ** -0.5),
        'moe_w_down': nrm((N_MOE, N_EXPERTS, D_FF_EXPERT, D_MODEL), D_FF_EXPERT ** -0.5),
        'final_norm_g': 1.0 + nrm((D_MODEL,), 0.02),
    }


def reference(x, c, ctx, c_ctx, mod_w, mod_b, norm_mix_g, norm_ffn_g, w_in, w_out,
              lru_conv_w, lru_conv_b, lru_w_a, lru_b_a, lru_w_x, lru_b_x, lru_lambda,
              rwkv_mu, rwkv_w_up, rwkv_w0, rwkv_a_up, rwkv_a0, rwkv_g_up, rwkv_k_k, rwkv_k_a,
              rwkv_r_k, rwkv_ln_g, rwkv_ln_b, mlstm_i_b, mlstm_f_b, mlstm_norm_g,
              gdn_conv_w, gdn_a_log, gdn_dt_bias, gdn_norm_g,
              ffn_w_gate, ffn_w_up, ffn_w_down, moe_router, moe_w_gate, moe_w_up, moe_w_down,
              final_norm_g):
    rows = x.shape[1] // GRID_W
    silu_c = jax.nn.silu(c)
    silu_cc = jax.nn.silu(c_ctx)
    xc = ctx
    for l in range(DEPTH):
        last = l == DEPTH - 1
        mod = silu_c @ mod_w[l] + mod_b[l]
        mod_c = silu_cc @ mod_w[l] + mod_b[l]
        sh1, sc1, gt1, sh2, sc2, gt2 = [m[:, None] for m in jnp.split(mod, 6, axis=-1)]
        csh1, csc1, cgt1, csh2, csc2, cgt2 = jnp.split(mod_c, 6, axis=-1)
        h = rms_norm(x, norm_mix_g[l]) * (1.0 + sc1) + sh1
        hc = rms_norm(xc, norm_mix_g[l]) * (1.0 + csc1) + csh1
        p = h @ w_in[l]
        pc = hc @ w_in[l]
        ya_c, ya = rglru_mixer(pc[..., :OFF_B], p[..., :OFF_B], lru_conv_w[l], lru_conv_b[l],
                               lru_w_a[l], lru_b_a[l], lru_w_x[l], lru_b_x[l], lru_lambda[l], not last)
        yb_c, yb = rwkv7_mixer(pc[..., OFF_B:OFF_C], p[..., OFF_B:OFF_C], rwkv_mu[l], rwkv_w_up[l],
                               rwkv_w0[l], rwkv_a_up[l], rwkv_a0[l], rwkv_g_up[l], rwkv_k_k[l],
                               rwkv_k_a[l], rwkv_r_k[l], rwkv_ln_g[l], rwkv_ln_b[l], not last)
        yc_c, yc = mlstm_mixer(pc[..., OFF_C:OFF_D], p[..., OFF_C:OFF_D], mlstm_i_b[l], mlstm_f_b[l],
                               mlstm_norm_g[l], rows, not last)
        yd_c, yd = gdn_mixer(pc[..., OFF_D:], p[..., OFF_D:], gdn_conv_w[l], gdn_a_log[l],
                             gdn_dt_bias[l], gdn_norm_g[l], rows, not last)
        x = x + gt1 * (jnp.concatenate([ya, yb, yc, yd], axis=-1) @ w_out[l])
        if not last:
            xc = xc + cgt1 * (jnp.concatenate([ya_c, yb_c, yc_c, yd_c], axis=-1) @ w_out[l])
        h2 = rms_norm(x, norm_ffn_g[l]) * (1.0 + sc2) + sh2
        tokens = h2.reshape(-1, D_MODEL)
        n_lat = tokens.shape[0]
        if not last:
            hc2 = rms_norm(xc, norm_ffn_g[l]) * (1.0 + csc2) + csh2
            tokens = jnp.concatenate([tokens, hc2.reshape(-1, D_MODEL)], axis=0)
        if l % 2 == 0:
            y = swiglu(tokens, ffn_w_gate[l // 2], ffn_w_up[l // 2], ffn_w_down[l // 2])
        else:
            y = moe_swiglu(tokens, moe_router[l // 2], moe_w_gate[l // 2], moe_w_up[l // 2],
                           moe_w_down[l // 2])
        x = x + gt2 * y[:n_lat].reshape(x.shape)
        if not last:
            xc = xc + cgt2 * y[n_lat:].reshape(xc.shape)
    return rms_norm(x, final_norm_g)
```

```python
import functools
import math

import jax
import jax.numpy as jnp
from jax import lax
from jax.experimental import pallas as pl
from jax.experimental.pallas import tpu as pltpu

D_MODEL = 1024
DEPTH = 4
GRID_W = 64
N_MIXERS = 4
GROUP_W = D_MODEL // N_MIXERS
HEAD_DIM = 64
N_HEADS = GROUP_W // HEAD_DIM
CONV_W = 4
LRU_C = 8.0
RWKV_LORA_W = 32
RWKV_LORA_A = 32
RWKV_LORA_G = 64
RWKV_GN_EPS = 64e-5
CHUNK = 64
D_FF = 2816
N_EXPERTS = 8
TOP_K = 2
D_FF_EXPERT = 1408
MOE_BLOCK = 256
EPS = 1e-6

A_COLS = 2 * GROUP_W
B_COLS = 3 * GROUP_W + 2 * RWKV_LORA_W + 2 * RWKV_LORA_A + RWKV_LORA_G
C_COLS = 4 * GROUP_W + 4 * N_HEADS
D_COLS = 4 * GROUP_W + 4 * N_HEADS
OFF_B = A_COLS
OFF_C = OFF_B + B_COLS
OFF_D = OFF_C + C_COLS
IN_COLS = OFF_D + D_COLS

LANE = 128
SUBLANE = 8
VMEM_LIMIT = 56 * 1024 * 1024
ROW_TILE = 256
F32 = jnp.float32
BF16 = jnp.bfloat16
HI = lax.Precision.HIGHEST


def _round_up(n, m):
    return (n + m - 1) // m * m


def _pad_cols(w, n):
    return jnp.pad(w, ((0, 0), (0, n - w.shape[1])))


def _const_spec(shape):
    nd = len(shape)
    return pl.BlockSpec(shape, lambda *_: (0,) * nd, pipeline_mode=pl.Buffered(1))


def _params(*sem):
    return pltpu.CompilerParams(dimension_semantics=sem, vmem_limit_bytes=VMEM_LIMIT)


def _mod_spec(d, tiles_per_seq):
    def index(i):
        return (2 * (i // tiles_per_seq) + (i % tiles_per_seq) // (tiles_per_seq - 1), 0, 0)
    return pl.BlockSpec((1, 1, d), index)


def _norm_mod(x, g, sc, sh):
    ms = jnp.mean(x * x, axis=-1, keepdims=True)
    return (x * lax.rsqrt(ms + EPS) * g) * (1.0 + sc) + sh


def _dot_dims(a, b, dims, prec):
    if prec is None:
        a, b, prec = a.astype(BF16), b.astype(BF16), None
    return lax.dot_general(a, b, (dims, ((), ())), preferred_element_type=F32, precision=prec)


def _dot(a, b, prec=None):
    return _dot_dims(a, b, ((1,), (0,)), prec)


def _dot_nt(a, b, prec=None):
    return _dot_dims(a, b, ((1,), (1,)), prec)


def _dot_tn(a, b, prec=None):
    return _dot_dims(a, b, ((0,), (0,)), prec)


def _norm_proj_kernel(x_ref, g_ref, sc_ref, sh_ref, *refs, n_out):
    h = _norm_mod(x_ref[...], g_ref[...], sc_ref[0], sh_ref[0]).astype(BF16)
    for w_ref, o_ref in zip(refs[:n_out], refs[n_out:]):
        o_ref[...] = _dot(h, w_ref[...])


def norm_proj(x2d, g, sc, sh, ws, tiles_per_seq):
    n, d = x2d.shape
    tm = ROW_TILE
    mod_spec = _mod_spec(d, tiles_per_seq)
    return pl.pallas_call(
        functools.partial(_norm_proj_kernel, n_out=len(ws)),
        grid=(n // tm,),
        in_specs=[pl.BlockSpec((tm, d), lambda i: (i, 0)), _const_spec((1, d)), mod_spec, mod_spec]
        + [_const_spec(w.shape) for w in ws],
        out_specs=[pl.BlockSpec((tm, w.shape[1]), lambda i: (i, 0)) for w in ws],
        out_shape=[jax.ShapeDtypeStruct((n, w.shape[1]), F32) for w in ws],
        compiler_params=_params("arbitrary"),
        name="norm_proj",
    )(x2d, g.reshape(1, d), sc, sh, *ws)


def _out_proj_kernel(ya_ref, yb_ref, yc_ref, yd_ref, w_ref, x_ref, gt_ref, o_ref):
    y = jnp.concatenate([ya_ref[...], yb_ref[...], yc_ref[...], yd_ref[...]], axis=-1).astype(BF16)
    o_ref[...] = x_ref[...] + gt_ref[0] * _dot(y, w_ref[...])


def out_proj(ys, w, x2d, gt, tiles_per_seq):
    n, d = x2d.shape
    tm = ROW_TILE
    y_spec = pl.BlockSpec((tm, GROUP_W), lambda i: (i, 0))
    return pl.pallas_call(
        _out_proj_kernel,
        grid=(n // tm,),
        in_specs=[y_spec] * 4 + [_const_spec(w.shape), pl.BlockSpec((tm, d), lambda i: (i, 0)),
                                 _mod_spec(d, tiles_per_seq)],
        out_specs=pl.BlockSpec((tm, d), lambda i: (i, 0)),
        out_shape=jax.ShapeDtypeStruct((n, d), F32),
        compiler_params=_params("arbitrary"),
        name="out_proj",
    )(*ys, w, x2d, gt)


def _swiglu_chunks(hb, wg_ref, wu_ref, wd_ref, n_chunks, fc):
    acc = None
    for c in range(n_chunks):
        a = _dot(hb, wg_ref[:, c * fc:(c + 1) * fc])
        u = _dot(hb, wu_ref[:, c * fc:(c + 1) * fc])
        z = (a * jax.nn.sigmoid(a) * u).astype(BF16)
        part = _dot(z, wd_ref[c * fc:(c + 1) * fc, :])
        acc = part if acc is None else acc + part
    return acc


def _dense_ffn_kernel(x_ref, g_ref, sc_ref, sh_ref, gt_ref, wg_ref, wu_ref, wd_ref, o_ref, *, n_chunks, fc):
    x = x_ref[...]
    hb = _norm_mod(x, g_ref[...], sc_ref[0], sh_ref[0]).astype(BF16)
    o_ref[...] = x + gt_ref[0] * _swiglu_chunks(hb, wg_ref, wu_ref, wd_ref, n_chunks, fc)


def dense_ffn(x2d, g, sc, sh, gt, wg, wu, wd, tiles_per_seq):
    n, d = x2d.shape
    tm = ROW_TILE
    fc = D_FF // 2
    mod_spec = _mod_spec(d, tiles_per_seq)
    return pl.pallas_call(
        functools.partial(_dense_ffn_kernel, n_chunks=D_FF // fc, fc=fc),
        grid=(n // tm,),
        in_specs=[pl.BlockSpec((tm, d), lambda i: (i, 0)), _const_spec((1, d)), mod_spec, mod_spec, mod_spec,
                  _const_spec(wg.shape), _const_spec(wu.shape), _const_spec(wd.shape)],
        out_specs=pl.BlockSpec((tm, d), lambda i: (i, 0)),
        out_shape=jax.ShapeDtypeStruct((n, d), F32),
        compiler_params=_params("arbitrary"),
        name="dense_ffn",
    )(x2d, g.reshape(1, d), sc, sh, gt, wg, wu, wd)


def _router_kernel(x_ref, g_ref, sc_ref, sh_ref, r_ref, h_ref, l_ref):
    h = _norm_mod(x_ref[...], g_ref[...], sc_ref[0], sh_ref[0])
    h_ref[...] = h.astype(BF16)
    l_ref[...] = _dot(h, r_ref[...], HI)


def norm_router(x2d, g, sc, sh, router_pad, tiles_per_seq):
    n, d = x2d.shape
    tm = ROW_TILE
    mod_spec = _mod_spec(d, tiles_per_seq)
    return pl.pallas_call(
        _router_kernel,
        grid=(n // tm,),
        in_specs=[pl.BlockSpec((tm, d), lambda i: (i, 0)), _const_spec((1, d)), mod_spec, mod_spec,
                  _const_spec(router_pad.shape)],
        out_specs=[pl.BlockSpec((tm, d), lambda i: (i, 0)), pl.BlockSpec((tm, LANE), lambda i: (i, 0))],
        out_shape=[jax.ShapeDtypeStruct((n, d), BF16), jax.ShapeDtypeStruct((n, LANE), F32)],
        compiler_params=_params("arbitrary"),
        name="norm_router",
    )(x2d, g.reshape(1, d), sc, sh, router_pad)


def _expert_kernel(be_ref, nb_ref, xb_ref, wg_ref, wu_ref, wd_ref, o_ref):
    i = pl.program_id(0)

    @pl.when(i < nb_ref[0])
    def _():
        o_ref[...] = _swiglu_chunks(xb_ref[...], wg_ref.at[0], wu_ref.at[0], wd_ref.at[0], 1, D_FF_EXPERT)

    @pl.when(i >= nb_ref[0])
    def _():
        o_ref[...] = jnp.zeros_like(o_ref)


def expert_ffn(xb, block_e, n_used, wg, wu, wd, tm):
    npad, d = xb.shape
    f = wg.shape[2]
    grid_spec = pltpu.PrefetchScalarGridSpec(
        num_scalar_prefetch=2,
        grid=(npad // tm,),
        in_specs=[pl.BlockSpec((tm, d), lambda i, be, nb: (i, 0)),
                  pl.BlockSpec((1, d, f), lambda i, be, nb: (be[i], 0, 0)),
                  pl.BlockSpec((1, d, f), lambda i, be, nb: (be[i], 0, 0)),
                  pl.BlockSpec((1, f, d), lambda i, be, nb: (be[i], 0, 0))],
        out_specs=pl.BlockSpec((tm, d), lambda i, be, nb: (i, 0)),
    )
    return pl.pallas_call(
        _expert_kernel,
        grid_spec=grid_spec,
        out_shape=jax.ShapeDtypeStruct((npad, d), F32),
        compiler_params=_params("arbitrary"),
        name="expert_ffn",
    )(block_e, n_used, xb, wg, wu, wd)


def moe_ffn(h_bf, logits, wg, wu, wd):
    t, d = h_bf.shape
    top_logits, top_idx = lax.top_k(logits, TOP_K)
    gate = jax.nn.softmax(top_logits, axis=-1)
    tk = t * TOP_K
    flat_e = top_idx.reshape(tk)
    order = jnp.argsort(flat_e)
    se = flat_e[order]
    stok = (order // TOP_K).astype(jnp.int32)
    counts = jnp.zeros((N_EXPERTS,), jnp.int32).at[flat_e].add(1)
    start = jnp.cumsum(counts) - counts
    padded = (counts + MOE_BLOCK - 1) // MOE_BLOCK * MOE_BLOCK
    pad_start = jnp.cumsum(padded) - padded
    pad_end = pad_start + padded
    pos = pad_start[se] + jnp.arange(tk, dtype=jnp.int32) - start[se]
    n_blocks = -(-tk // MOE_BLOCK) + N_EXPERTS
    buf_tok = jnp.zeros((n_blocks * MOE_BLOCK,), jnp.int32).at[pos].set(stok)
    slot_of = jnp.zeros((tk,), jnp.int32).at[order].set(pos).reshape(t, TOP_K)
    block_e = jnp.clip(jnp.searchsorted(pad_end, jnp.arange(n_blocks, dtype=jnp.int32) * MOE_BLOCK,
                                        side='right'), 0, N_EXPERTS - 1).astype(jnp.int32)
    n_used = (pad_end[-1] // MOE_BLOCK).astype(jnp.int32).reshape(1)
    yb = expert_ffn(h_bf[buf_tok], block_e, n_used, wg, wu, wd, MOE_BLOCK)
    return gate[:, 0:1] * yb[slot_of[:, 0]] + gate[:, 1:2] * yb[slot_of[:, 1]]


def _final_norm_kernel(x_ref, g_ref, o_ref):
    x = x_ref[...]
    ms = jnp.mean(x * x, axis=-1, keepdims=True)
    o_ref[...] = x * lax.rsqrt(ms + EPS) * g_ref[...]


def final_norm(x3d, g, n_lat):
    bsz, t, d = x3d.shape
    tm = ROW_TILE
    return pl.pallas_call(
        _final_norm_kernel,
        grid=(bsz, n_lat // tm),
        in_specs=[pl.BlockSpec((None, tm, d), lambda b, i: (b, i, 0)), _const_spec((1, d))],
        out_specs=pl.BlockSpec((None, tm, d), lambda b, i: (b, i, 0)),
        out_shape=jax.ShapeDtypeStruct((bsz, n_lat, d), F32),
        compiler_params=_params("arbitrary", "arbitrary"),
        name="final_norm",
    )(x3d, g.reshape(1, d))


def _same_head(shape):
    r = lax.broadcasted_iota(jnp.int32, shape, 0) // HEAD_DIM
    c = lax.broadcasted_iota(jnp.int32, shape, 1) // HEAD_DIM
    return r == c


def _to_block_diag(x, mask):
    return jnp.where(mask, jnp.concatenate([x] * N_HEADS, axis=0), 0.0)


def _token_masks(reverse):
    t = lax.broadcasted_iota(jnp.int32, (CHUNK, GROUP_W), 0)
    s = lax.broadcasted_iota(jnp.int32, (CHUNK, GROUP_W), 1) % HEAD_DIM
    if reverse:
        return s > t, s >= t, s == t
    return s < t, s <= t, s == t


def _cum_matrix(reverse):
    t = lax.broadcasted_iota(jnp.int32, (CHUNK, CHUNK), 0)
    s = lax.broadcasted_iota(jnp.int32, (CHUNK, CHUNK), 1)
    return jnp.where((s >= t) if reverse else (s <= t), 1.0, 0.0).astype(F32)


def _unit_lower_inverse(n_wide, eye_wide, bd_mask, prec):
    x = eye_wide + n_wide
    p = n_wide
    p_bd = _to_block_diag(p, bd_mask)
    for _ in range(5):
        p = _dot(p, p_bd, prec)
        p_bd = _to_block_diag(p, bd_mask)
        x = x + _dot(x, p_bd, prec)
    return x


def _head_lane_expander(col0):
    r = lax.broadcasted_iota(jnp.int32, (LANE, GROUP_W), 0)
    c = lax.broadcasted_iota(jnp.int32, (LANE, GROUP_W), 1) // HEAD_DIM
    return jnp.where(r == col0 + c, 1.0, 0.0).astype(F32)


def _fwd_chunk(j, n_lat_chunks, n_chunks):
    n_ctx_chunks = n_chunks - n_lat_chunks
    return jnp.where(j < n_ctx_chunks, n_lat_chunks + j, j - n_ctx_chunks)


def _bwd_chunk(j, n_lat_chunks, n_chunks):
    return n_chunks - 1 - j


def _seq_edges(c, n_lat_chunks, n_chunks):
    first = jnp.logical_or(c == 0, c == n_lat_chunks)
    last = jnp.logical_or(c == n_lat_chunks - 1, c == n_chunks - 1)
    return first, last


def _shift_rows(x, prev8, next8, back):
    row = lax.broadcasted_iota(jnp.int32, (CHUNK, 1), 0)
    if back == -1:
        return jnp.where(row == CHUNK - 1, next8[0:1, :], pltpu.roll(x, CHUNK - 1, 0))
    out = pltpu.roll(x, back, 0)
    for i in range(back):
        out = jnp.where(row == i, prev8[SUBLANE - back + i:SUBLANE - back + i + 1, :], out)
    return out


def _natural_scan_specs(width, out_width, n_lat_chunks, n_chunks, last8):
    r8 = CHUNK // SUBLANE
    fwd = functools.partial(_fwd_chunk, n_lat_chunks=n_lat_chunks, n_chunks=n_chunks)
    bwd = functools.partial(_bwd_chunk, n_lat_chunks=n_lat_chunks, n_chunks=n_chunks)
    chunk = lambda f, w: pl.BlockSpec((None, CHUNK, w), lambda b, j: (b, f(j), 0))
    prev = lambda f: pl.BlockSpec((None, SUBLANE, width), lambda b, j: (b, jnp.maximum(f(j) * r8 - 1, 0), 0))
    nxt = lambda f: pl.BlockSpec((None, SUBLANE, width), lambda b, j: (b, jnp.minimum((f(j) + 1) * r8, last8), 0))
    ins = [chunk(fwd, width), prev(fwd), nxt(fwd), chunk(bwd, width), prev(bwd), nxt(bwd)]
    return ins, chunk(fwd, out_width), chunk(bwd, out_width)


def _chunk_specs(width, n_steps, first_row_block, column_view):
    if column_view:
        return (pl.BlockSpec((None, CHUNK, width), lambda b, j: (b, 0, j)),
                pl.BlockSpec((None, CHUNK, width), lambda b, j: (b, 0, n_steps - 1 - j)))
    return (pl.BlockSpec((None, CHUNK, width), lambda b, j: (b, first_row_block + j, 0)),
            pl.BlockSpec((None, CHUNK, width), lambda b, j: (b, first_row_block + n_steps - 1 - j, 0)))


def _halo_specs(width, n_steps, first_row_block, column_view, last8):
    r8 = CHUNK // SUBLANE
    fwd = lambda j: j
    bwd = lambda j: n_steps - 1 - j
    if column_view:
        prev = lambda f: pl.BlockSpec((None, SUBLANE, width), lambda b, j: (b, r8 - 1, jnp.maximum(f(j) - 1, 0)))
        nxt = lambda f: pl.BlockSpec((None, SUBLANE, width), lambda b, j: (b, 0, jnp.minimum(f(j) + 1, n_steps - 1)))
    else:
        prev = lambda f: pl.BlockSpec((None, SUBLANE, width),
                                      lambda b, j: (b, jnp.maximum((first_row_block + f(j)) * r8 - 1, 0), 0))
        nxt = lambda f: pl.BlockSpec((None, SUBLANE, width),
                                     lambda b, j: (b, jnp.minimum((first_row_block + f(j) + 1) * r8, last8), 0))
    return prev(fwd), nxt(fwd), prev(bwd), nxt(bwd)


def _gated_head_norm_kernel(hf_ref, hb_ref, o_ref, g_ref, y_ref, *, silu_gate):
    h = hf_ref[...] + hb_ref[...]
    seg_mean = jnp.where(_same_head((GROUP_W, GROUP_W)), 1.0 / HEAD_DIM, 0.0).astype(F32)
    hn = h * lax.rsqrt(_dot(h * h, seg_mean, HI) + EPS) * g_ref[...]
    gate = o_ref[...]
    y_ref[...] = hn * (gate * jax.nn.sigmoid(gate) if silu_gate else jax.nn.sigmoid(gate))


def gated_head_norm(hf, hb, p2d, norm_g, silu_gate):
    n, G = hf.shape
    tm = ROW_TILE
    row_spec = pl.BlockSpec((tm, G), lambda i: (i, 0))
    return pl.pallas_call(
        functools.partial(_gated_head_norm_kernel, silu_gate=silu_gate),
        grid=(n // tm,),
        in_specs=[row_spec, row_spec, pl.BlockSpec((tm, G), lambda i: (i, 3)), _const_spec((1, G))],
        out_specs=row_spec,
        out_shape=jax.ShapeDtypeStruct((n, G), F32),
        compiler_params=_params("arbitrary"),
        name="gated_head_norm",
    )(hf, hb, p2d, norm_g.reshape(1, G))


def _rwkv_chunk(p, prev_row, next_row, prm, s_ref, d, reverse):
    (mu_ref, wup_ref, w0_ref, aup_ref, a0_ref, gup_ref, kk_ref, ka_ref, rk_ref) = prm
    G = GROUP_W
    row = lax.broadcasted_iota(jnp.int32, (CHUNK, 1), 0)
    prev = jnp.where(row == 0, prev_row, pltpu.roll(p, 1, 0))
    nxt = jnp.where(row == CHUNK - 1, next_row, pltpu.roll(p, CHUNK - 1, 0))
    ps = p + mu_ref[0:1, :] * (prev - p) + mu_ref[1:2, :] * (nxt - p)
    r, k, v = ps[:, :G], ps[:, G:2 * G], ps[:, 2 * G:3 * G]
    lora_in = ps[:, 3 * G:3 * G + LANE]
    gate_in = ps[:, 3 * G + LANE:3 * G + 2 * LANE]
    bd_mask = _same_head((G, G))
    head_ones = jnp.where(bd_mask, 1.0, 0.0).astype(F32)
    log_w = -math.exp(-0.5) * jax.nn.sigmoid(_dot(jnp.tanh(lora_in), wup_ref[d], HI) + w0_ref[d:d + 1, :])
    a = jax.nn.sigmoid(_dot(lora_in, aup_ref[d], HI) + a0_ref[d:d + 1, :])
    kr = k * kk_ref[...]
    kk = kr * lax.rsqrt(_dot(kr * kr, head_ones, HI) + EPS)
    k_mod = k * (1.0 + (a - 1.0) * ka_ref[...])
    bonus = _dot(r * k_mod * rk_ref[...], head_ones, HI) * v
    lg = _dot(_cum_matrix(reverse), log_w, HI)
    e_neg = jnp.exp(-lg)
    at = kk * jnp.exp(lg - log_w)
    bt = -(kk * a) * e_neg
    kt = k_mod * e_neg
    rt = r * jnp.exp(lg)
    g_last = jnp.exp(lg[0:1, :] if reverse else lg[CHUNK - 1:CHUNK, :])
    strict, incl, diag = _token_masks(reverse)
    lhs = jnp.concatenate([at, rt], axis=0)
    m_b = _dot_nt(lhs, _to_block_diag(bt, bd_mask))
    m_k = _dot_nt(lhs, _to_block_diag(kt, bd_mask))
    a_ab = jnp.where(strict, m_b[:CHUNK], 0.0)
    a_ak = jnp.where(strict, m_k[:CHUNK], 0.0)
    a_rb = jnp.where(incl, m_b[CHUNK:], 0.0)
    a_rk = jnp.where(incl, m_k[CHUNK:], 0.0)
    tinv = _unit_lower_inverse(a_ab, jnp.where(diag, 1.0, 0.0).astype(F32), bd_mask, None)
    s = s_ref[...]
    v_bd = _to_block_diag(v, bd_mask)
    u = _dot(tinv, _to_block_diag(_dot_nt(at, s) + _dot(a_ak, v_bd), bd_mask))
    y = _dot_nt(rt, s) + _dot(a_rb, _to_block_diag(u, bd_mask)) + _dot(a_rk, v_bd)
    upd = _dot_tn(jnp.concatenate([u, v], axis=0), jnp.concatenate([bt, kt], axis=0))
    s_ref[...] = g_last * (s + jnp.where(bd_mask, upd, 0.0))
    gate = _dot(jax.nn.sigmoid(gate_in), gup_ref[...], HI)
    return y, bonus, gate


def _rwkv_kernel(pf_ref, pfp_ref, pfn_ref, pb_ref, pbp_ref, pbn_ref,
                 mu_ref, wup_ref, w0_ref, aup_ref, a0_ref, gup_ref, kk_ref, ka_ref, rk_ref,
                 yf_ref, bf_ref, g_ref, yb_ref, bb_ref, sf_ref, sb_ref, *, n_lat_chunks, n_chunks):
    j = pl.program_id(1)

    @pl.when(j == 0)
    def _():
        sf_ref[...] = jnp.zeros_like(sf_ref)
        sb_ref[...] = jnp.zeros_like(sb_ref)

    prm = (mu_ref, wup_ref, w0_ref, aup_ref, a0_ref, gup_ref, kk_ref, ka_ref, rk_ref)
    first, last = _seq_edges(_fwd_chunk(j, n_lat_chunks, n_chunks), n_lat_chunks, n_chunks)
    prev_row = jnp.where(first, 0.0, pfp_ref[SUBLANE - 1:SUBLANE, :])
    next_row = jnp.where(last, 0.0, pfn_ref[0:1, :])
    y, bonus, gate = _rwkv_chunk(pf_ref[...], prev_row, next_row, prm, sf_ref, 0, False)
    yf_ref[...] = y
    bf_ref[...] = bonus
    g_ref[...] = gate
    first, last = _seq_edges(_bwd_chunk(j, n_lat_chunks, n_chunks), n_lat_chunks, n_chunks)
    prev_row = jnp.where(first, 0.0, pbp_ref[SUBLANE - 1:SUBLANE, :])
    next_row = jnp.where(last, 0.0, pbn_ref[0:1, :])
    y, bonus, _ = _rwkv_chunk(pb_ref[...], prev_row, next_row, prm, sb_ref, 1, True)
    yb_ref[...] = y
    bb_ref[...] = bonus


def _rwkv_finish_kernel(yf_ref, yb_ref, bf_ref, bb_ref, g_ref, lng_ref, lnb_ref, o_ref):
    y = yf_ref[...] + yb_ref[...]
    seg_mean = jnp.where(_same_head((GROUP_W, GROUP_W)), 1.0 / HEAD_DIM, 0.0).astype(F32)
    mean = _dot(y, seg_mean, HI)
    var = _dot(jnp.square(y - mean), seg_mean, HI)
    yn = (y - mean) * lax.rsqrt(var + RWKV_GN_EPS) * lng_ref[...] + lnb_ref[...]
    o_ref[...] = (yn + bf_ref[...] + bb_ref[...]) * g_ref[...]


def rwkv7_mixer(p3d, n_lat, mu, w_up, w0, a_up, a0, g_up, k_k, k_a, r_k, ln_g, ln_b):
    bsz, t, width = p3d.shape
    G = GROUP_W
    n_chunks = t // CHUNK
    n_lat_chunks = n_lat // CHUNK
    mu_pad = _pad_cols(mu, width)
    wup = jnp.zeros((2, LANE, G), F32)
    aup = jnp.zeros((2, LANE, G), F32)
    for d in range(2):
        wup = wup.at[d, d * RWKV_LORA_W:(d + 1) * RWKV_LORA_W].set(w_up[d])
        lo = 2 * RWKV_LORA_W + d * RWKV_LORA_A
        aup = aup.at[d, lo:lo + RWKV_LORA_A].set(a_up[d])
    gup = jnp.zeros((LANE, G), F32).at[:RWKV_LORA_G].set(g_up)
    consts = [mu_pad, wup, w0, aup, a0, gup, k_k.reshape(1, G), k_a.reshape(1, G), r_k.reshape(1, G)]
    ins, f_out, b_out = _natural_scan_specs(width, G, n_lat_chunks, n_chunks, t // SUBLANE - 1)
    out_sds = jax.ShapeDtypeStruct((bsz, t, G), F32)
    yf, bf, gate, yb, bb = pl.pallas_call(
        functools.partial(_rwkv_kernel, n_lat_chunks=n_lat_chunks, n_chunks=n_chunks),
        grid=(bsz, n_chunks),
        in_specs=ins + [_const_spec(c.shape) for c in consts],
        out_specs=[f_out] * 3 + [b_out] * 2,
        out_shape=[out_sds] * 5,
        scratch_shapes=[pltpu.VMEM((G, G), F32), pltpu.VMEM((G, G), F32)],
        compiler_params=_params("arbitrary", "arbitrary"),
        name="rwkv_scan",
    )(p3d, p3d, p3d, p3d, p3d, p3d, *consts)
    n = bsz * t
    tm = ROW_TILE
    row_spec = pl.BlockSpec((tm, G), lambda i: (i, 0))
    return pl.pallas_call(
        _rwkv_finish_kernel,
        grid=(n // tm,),
        in_specs=[row_spec] * 5 + [_const_spec((1, G))] * 2,
        out_specs=row_spec,
        out_shape=jax.ShapeDtypeStruct((n, G), F32),
        compiler_params=_params("arbitrary"),
        name="rwkv_finish",
    )(*[a.reshape(n, G) for a in (yf, yb, bf, bb, gate)], ln_g.reshape(1, G), ln_b.reshape(1, G))


def _neg_expm1(x):
    series = -x * (1.0 + x * (1 / 2 + x * (1 / 6 + x * (1 / 24 + x * (1 / 120 + x * (1 / 720 + x / 5040))))))
    return jnp.where(x > -0.05, series, 1.0 - jnp.exp(x))


def _lru_chunk(p, prev8, next8, cw_ref, cb_ref, w_ref, bias_ref, lam_ref, h_ref, d, reverse):
    G = GROUP_W
    x = p[:, :G]
    u = (cw_ref[0:1, :] * _shift_rows(x, prev8, next8, 2) + cw_ref[1:2, :] * _shift_rows(x, prev8, next8, 1)
         + cw_ref[2:3, :] * x + cw_ref[3:4, :] * _shift_rows(x, prev8, next8, -1)) + cb_ref[...]
    pre = _dot(u, w_ref[d]) + bias_ref[d:d + 1, :]
    r = jax.nn.sigmoid(pre[:, :G])
    i = jax.nn.sigmoid(pre[:, G:])
    log_a = -LRU_C * r * jax.nn.softplus(-lam_ref[d:d + 1, :])
    a = jnp.exp(log_a)
    b = jnp.sqrt(_neg_expm1(2.0 * log_a)) * (i * u)
    row = lax.broadcasted_iota(jnp.int32, (CHUNK, 1), 0)
    k = 1
    while k < CHUNK:
        shift = CHUNK - k if reverse else k
        valid = (row < CHUNK - k) if reverse else (row >= k)
        b = jnp.where(valid, a * pltpu.roll(b, shift, 0) + b, b)
        a = jnp.where(valid, a * pltpu.roll(a, shift, 0), a)
        k *= 2
    h = b + a * h_ref[0:1, :]
    h_ref[0:1, :] = h[0:1, :] if reverse else h[CHUNK - 1:CHUNK, :]
    return h


def _lru_kernel(pf_ref, pfp_ref, pfn_ref, pb_ref, pbp_ref, pbn_ref, cw_ref, cb_ref, w_ref, bias_ref, lam_ref,
                hf_ref, hb_ref, st_ref, *, n_lat_chunks, n_chunks):
    j = pl.program_id(1)

    @pl.when(j == 0)
    def _():
        st_ref[...] = jnp.zeros_like(st_ref)

    G = GROUP_W
    first, last = _seq_edges(_fwd_chunk(j, n_lat_chunks, n_chunks), n_lat_chunks, n_chunks)
    prev8 = jnp.where(first, 0.0, pfp_ref[:, :G])
    next8 = jnp.where(last, 0.0, pfn_ref[:, :G])
    hf_ref[...] = _lru_chunk(pf_ref[...], prev8, next8, cw_ref, cb_ref, w_ref, bias_ref, lam_ref, st_ref.at[0], 0, False)
    first, last = _seq_edges(_bwd_chunk(j, n_lat_chunks, n_chunks), n_lat_chunks, n_chunks)
    prev8 = jnp.where(first, 0.0, pbp_ref[:, :G])
    next8 = jnp.where(last, 0.0, pbn_ref[:, :G])
    hb_ref[...] = _lru_chunk(pb_ref[...], prev8, next8, cw_ref, cb_ref, w_ref, bias_ref, lam_ref, st_ref.at[1], 1, True)


def _lru_finish_kernel(hf_ref, hb_ref, gate_ref, y_ref):
    y_ref[...] = (hf_ref[...] + hb_ref[...]) * jax.nn.gelu(gate_ref[...])


def rglru_mixer(p3d, n_lat, conv_w, conv_b, w_a, b_a, w_x, b_x, lam):
    bsz, t, width = p3d.shape
    G = GROUP_W
    n_chunks = t // CHUNK
    n_lat_chunks = n_lat // CHUNK
    w = jnp.zeros((2, G, 2 * G), F32)
    for h in range(N_HEADS):
        sl = slice(h * HEAD_DIM, (h + 1) * HEAD_DIM)
        w = w.at[:, sl, sl].set(w_a[:, h])
        w = w.at[:, sl, G + h * HEAD_DIM:G + (h + 1) * HEAD_DIM].set(w_x[:, h])
    consts = [jnp.pad(conv_w, ((0, SUBLANE - CONV_W), (0, 0))), conv_b.reshape(1, G), w,
              jnp.concatenate([b_a, b_x], axis=1), lam]
    ins, f_out, b_out = _natural_scan_specs(width, G, n_lat_chunks, n_chunks, t // SUBLANE - 1)
    h_sds = jax.ShapeDtypeStruct((bsz, t, G), F32)
    hf, hb = pl.pallas_call(
        functools.partial(_lru_kernel, n_lat_chunks=n_lat_chunks, n_chunks=n_chunks),
        grid=(bsz, n_chunks),
        in_specs=ins + [_const_spec(c.shape) for c in consts],
        out_specs=[f_out, b_out],
        out_shape=[h_sds, h_sds],
        scratch_shapes=[pltpu.VMEM((2, SUBLANE, G), F32)],
        compiler_params=_params("arbitrary", "arbitrary"),
        name="lru_scan",
    )(p3d, p3d, p3d, p3d, p3d, p3d, *consts)
    n = bsz * t
    tm = ROW_TILE
    row_spec = pl.BlockSpec((tm, G), lambda i: (i, 0))
    return pl.pallas_call(
        _lru_finish_kernel,
        grid=(n // tm,),
        in_specs=[row_spec, row_spec, pl.BlockSpec((tm, G), lambda i: (i, 1))],
        out_specs=row_spec,
        out_shape=jax.ShapeDtypeStruct((n, G), F32),
        compiler_params=_params("arbitrary"),
        name="lru_finish",
    )(hf.reshape(n, G), hb.reshape(n, G), p3d.reshape(n, width))


def _gdn_chunk(p, prev8, next8, cw_ref, al_ref, dt_ref, s_ref, d, reverse):
    G = GROUP_W
    x = p[:, :3 * G]
    conv = (cw_ref[0:1, :] * _shift_rows(x, prev8, next8, 2) + cw_ref[1:2, :] * _shift_rows(x, prev8, next8, 1)
            + cw_ref[2:3, :] * x + cw_ref[3:4, :] * _shift_rows(x, prev8, next8, -1))
    qkv = conv * jax.nn.sigmoid(conv)
    gates = p[:, 4 * G:4 * G + LANE]
    bd_mask = _same_head((G, G))
    head_ones = jnp.where(bd_mask, 1.0, 0.0).astype(F32)
    q, k, v = qkv[:, :G], qkv[:, G:2 * G], qkv[:, 2 * G:]
    q = q * lax.rsqrt(_dot(q * q, head_ones, HI) + EPS) * (HEAD_DIM ** -0.5)
    k = k * lax.rsqrt(_dot(k * k, head_ones, HI) + EPS)
    la = -al_ref[d:d + 1, :] * jax.nn.softplus(_dot(gates, _head_lane_expander(d * N_HEADS), HI) + dt_ref[d:d + 1, :])
    beta = jax.nn.sigmoid(_dot(gates, _head_lane_expander((2 + d) * N_HEADS), HI))
    gam = _dot(_cum_matrix(reverse), la, HI)
    strict, incl, diag = _token_masks(reverse)
    gam_row = jnp.sum(jnp.where(diag, gam, 0.0), axis=0, keepdims=True)
    decay = jnp.exp(jnp.where(incl, gam - gam_row, -jnp.inf))
    k_beta = k * beta
    k_bd = _to_block_diag(k, bd_mask)
    lower = jnp.where(strict, _dot_nt(k_beta, k_bd) * decay, 0.0)
    tinv = _unit_lower_inverse(-lower, jnp.where(diag, 1.0, 0.0).astype(F32), bd_mask, HI)
    e_gam = jnp.exp(gam)
    u = _dot(tinv, _to_block_diag(v * beta, bd_mask), HI)
    w = _dot(tinv, _to_block_diag(k_beta * e_gam, bd_mask), HI)
    qk = _dot_nt(q, k_bd) * decay
    gam_last = gam[0:1, :] if reverse else gam[CHUNK - 1:CHUNK, :]
    s = s_ref[...]
    v_new = u - _dot(w, s)
    o = _dot(q * e_gam, s) + _dot(qk, _to_block_diag(v_new, bd_mask))
    s_ref[...] = jnp.exp(gam_last) * s + jnp.where(bd_mask, _dot_tn(k * jnp.exp(gam_last - gam), v_new), 0.0)
    return o


def _gdn_kernel(pf_ref, pfp_ref, pfn_ref, pb_ref, pbp_ref, pbn_ref, cw_ref, al_ref, dt_ref, s0_ref,
                of_ref, ob_ref, s_ref, *, n_steps):
    j = pl.program_id(1)

    @pl.when(j == 0)
    def _():
        s_ref[...] = s0_ref[...]

    width = 3 * GROUP_W
    prev8 = jnp.where(j == 0, 0.0, pfp_ref[:, :width])
    next8 = jnp.where(j == n_steps - 1, 0.0, pfn_ref[:, :width])
    of_ref[...] = _gdn_chunk(pf_ref[...], prev8, next8, cw_ref, al_ref, dt_ref, s_ref.at[0], 0, False)
    prev8 = jnp.where(j == n_steps - 1, 0.0, pbp_ref[:, :width])
    next8 = jnp.where(j == 0, 0.0, pbn_ref[:, :width])
    ob_ref[...] = _gdn_chunk(pb_ref[...], prev8, next8, cw_ref, al_ref, dt_ref, s_ref.at[1], 1, True)


def gdn_scan(p, n_steps, first_row_block, column_view, last8, cw, al, dt, s0):
    bsz = p.shape[0]
    G = GROUP_W
    width = _round_up(D_COLS, LANE)
    f_in, b_in = _chunk_specs(width, n_steps, first_row_block, column_view)
    fp, fn, bp, bn = _halo_specs(width, n_steps, first_row_block, column_view, last8)
    f_out, b_out = _chunk_specs(G, n_steps, 0, column_view)
    h_sds = jax.ShapeDtypeStruct((bsz, CHUNK, GRID_W * G) if column_view else (bsz, n_steps * CHUNK, G), F32)
    state_spec = pl.BlockSpec((None, 2, G, G), lambda b, j: (b, 0, 0, 0))
    return pl.pallas_call(
        functools.partial(_gdn_kernel, n_steps=n_steps),
        grid=(bsz, n_steps),
        in_specs=[f_in, fp, fn, b_in, bp, bn, _const_spec(cw.shape), _const_spec(al.shape), _const_spec(dt.shape),
                  state_spec],
        out_specs=[f_out, b_out, state_spec],
        out_shape=[h_sds, h_sds, jax.ShapeDtypeStruct(s0.shape, F32)],
        compiler_params=_params("arbitrary", "arbitrary"),
        name="gdn_scan",
    )(p, p, p, p, p, p, cw, al, dt, s0)


def gdn_mixer(p3d, n_lat, conv_w, a_log, dt_bias, norm_g):
    bsz, t, width = p3d.shape
    G = GROUP_W
    n_ctx = t - n_lat
    assert n_lat == CHUNK * GRID_W
    al = jnp.repeat(jnp.exp(a_log), HEAD_DIM, axis=1)
    dt = jnp.repeat(dt_bias, HEAD_DIM, axis=1)
    cw = jnp.pad(conv_w, ((0, SUBLANE - CONV_W), (0, 0)))
    s0 = jnp.zeros((bsz, 2, G, G), F32)
    last8 = t // SUBLANE - 1
    of_c, ob_c, s1 = gdn_scan(p3d, n_ctx // CHUNK, n_lat // CHUNK, False, last8, cw, al, dt, s0)
    of_l, ob_l, _ = gdn_scan(p3d.reshape(bsz, t // GRID_W, GRID_W * width), GRID_W, 0, True, last8, cw, al, dt, s1)
    hf = jnp.concatenate([of_l.reshape(bsz, n_lat, G), of_c], axis=1).reshape(bsz * t, G)
    hb = jnp.concatenate([ob_l.reshape(bsz, n_lat, G), ob_c], axis=1).reshape(bsz * t, G)
    return gated_head_norm(hf, hb, p3d.reshape(bsz * t, width), norm_g, True)


def _head_max(x):
    lane_head = lax.broadcasted_iota(jnp.int32, x.shape, 1) // HEAD_DIM
    out = jnp.zeros_like(x)
    for h in range(N_HEADS):
        mh = jnp.max(jnp.where(lane_head == h, x, -jnp.inf), axis=-1, keepdims=True)
        out = jnp.where(lane_head == h, mh, out)
    return out


def _mlstm_chunk(p, ib_ref, fb_ref, c_ref, nm_ref, d, reverse):
    G = GROUP_W
    q, k, v = p[:, :G], p[:, G:2 * G] * (HEAD_DIM ** -0.5), p[:, 2 * G:3 * G]
    gates = p[:, 4 * G:4 * G + LANE]
    bd_mask = _same_head((G, G))
    head_ones = jnp.where(bd_mask, 1.0, 0.0).astype(F32)
    li = _dot(gates, _head_lane_expander(d * N_HEADS), HI) + ib_ref[d:d + 1, :]
    lf = jax.nn.log_sigmoid(_dot(gates, _head_lane_expander((2 + d) * N_HEADS), HI) + fb_ref[d:d + 1, :])
    b = _dot(_cum_matrix(reverse), lf, HI)
    _, incl, diag = _token_masks(reverse)
    g = li - b
    g_row = jnp.sum(jnp.where(diag, g, 0.0), axis=0, keepdims=True)
    log_d = jnp.where(incl, b + g_row, -jnp.inf)
    m_intra = _head_max(log_d)
    mem, nrm, m_prev = c_ref[...], nm_ref[0:1, :], nm_ref[1:2, :]
    m_inter = b + m_prev
    m_t = jnp.maximum(m_inter, m_intra)
    w_inter = jnp.exp(m_inter - m_t)
    pm = jnp.exp(log_d - m_t) * _dot_nt(q, _to_block_diag(k, bd_mask))
    num = _dot(pm, _to_block_diag(v, bd_mask)) + w_inter * _dot_nt(q, mem)
    den = _dot(pm, head_ones, HI) + w_inter * _dot(q * nrm, head_ones, HI)
    h = num / jnp.maximum(jnp.abs(den), jnp.exp(-m_t))
    b_last = b[0:1, :] if reverse else b[CHUNK - 1:CHUNK, :]
    log_end = b_last + g
    m_new = jnp.maximum(b_last + m_prev, jnp.max(log_end, axis=0, keepdims=True))
    w_prev = jnp.exp(b_last + m_prev - m_new)
    w_new = jnp.exp(log_end - m_new)
    c_ref[...] = w_prev * mem + jnp.where(bd_mask, _dot_tn(w_new * v, k), 0.0)
    nm_ref[0:1, :] = w_prev * nrm + jnp.sum(w_new * k, axis=0, keepdims=True)
    nm_ref[1:2, :] = m_new
    return h


def _mlstm_kernel(pf_ref, pb_ref, ib_ref, fb_ref, c0_ref, nm0_ref, hf_ref, hb_ref, c_ref, nm_ref):
    @pl.when(pl.program_id(1) == 0)
    def _():
        c_ref[...] = c0_ref[...]
        nm_ref[...] = nm0_ref[...]

    hf_ref[...] = _mlstm_chunk(pf_ref[...], ib_ref, fb_ref, c_ref.at[0], nm_ref.at[0], 0, False)
    hb_ref[...] = _mlstm_chunk(pb_ref[...], ib_ref, fb_ref, c_ref.at[1], nm_ref.at[1], 1, True)


def mlstm_scan(p, n_steps, first_row_block, column_view, ib, fb, c0, nm0):
    bsz = p.shape[0]
    G = GROUP_W
    width = _round_up(C_COLS, LANE)
    f_in, b_in = _chunk_specs(width, n_steps, first_row_block, column_view)
    f_out, b_out = _chunk_specs(G, n_steps, 0, column_view)
    h_sds = jax.ShapeDtypeStruct((bsz, CHUNK, GRID_W * G) if column_view else (bsz, n_steps * CHUNK, G), F32)
    state_specs = [pl.BlockSpec((None, 2, G, G), lambda b, j: (b, 0, 0, 0)),
                   pl.BlockSpec((None, 2, SUBLANE, G), lambda b, j: (b, 0, 0, 0))]
    return pl.pallas_call(
        _mlstm_kernel,
        grid=(bsz, n_steps),
        in_specs=[f_in, b_in, _const_spec(ib.shape), _const_spec(fb.shape)] + state_specs,
        out_specs=[f_out, b_out] + state_specs,
        out_shape=[h_sds, h_sds, jax.ShapeDtypeStruct(c0.shape, F32), jax.ShapeDtypeStruct(nm0.shape, F32)],
        compiler_params=_params("arbitrary", "arbitrary"),
        name="mlstm_scan",
    )(p, p, ib, fb, c0, nm0)


def mlstm_mixer(p3d, n_lat, i_b, f_b, norm_g):
    bsz, t, width = p3d.shape
    G = GROUP_W
    n_ctx = t - n_lat
    assert n_lat == CHUNK * GRID_W
    ib = jnp.repeat(i_b, HEAD_DIM, axis=1)
    fb = jnp.repeat(f_b, HEAD_DIM, axis=1)
    c0 = jnp.zeros((bsz, 2, G, G), F32)
    nm0 = jnp.zeros((bsz, 2, SUBLANE, G), F32)
    hf_c, hb_c, c1, nm1 = mlstm_scan(p3d, n_ctx // CHUNK, n_lat // CHUNK, False, ib, fb, c0, nm0)
    hf_l, hb_l, _, _ = mlstm_scan(p3d.reshape(bsz, t // GRID_W, GRID_W * width), GRID_W, 0, True, ib, fb, c1, nm1)
    hf = jnp.concatenate([hf_l.reshape(bsz, n_lat, G), hf_c], axis=1).reshape(bsz * t, G)
    hb = jnp.concatenate([hb_l.reshape(bsz, n_lat, G), hb_c], axis=1).reshape(bsz * t, G)
    return gated_head_norm(hf, hb, p3d.reshape(bsz * t, width), norm_g, False)


def kernel(x, c, ctx, c_ctx, mod_w, mod_b, norm_mix_g, norm_ffn_g, w_in, w_out, lru_conv_w, lru_conv_b, lru_w_a, lru_b_a, lru_w_x, lru_b_x, lru_lambda, rwkv_mu, rwkv_w_up, rwkv_w0, rwkv_a_up, rwkv_a0, rwkv_g_up, rwkv_k_k, rwkv_k_a, rwkv_r_k, rwkv_ln_g, rwkv_ln_b, mlstm_i_b, mlstm_f_b, mlstm_norm_g, gdn_conv_w, gdn_a_log, gdn_dt_bias, gdn_norm_g, ffn_w_gate, ffn_w_up, ffn_w_down, moe_router, moe_w_gate, moe_w_up, moe_w_down, final_norm_g):
    bsz, seq, d = x.shape
    n_ctx = ctx.shape[1]
    t = n_ctx + seq
    n = bsz * t
    tiles_per_seq = t // ROW_TILE
    assert n_ctx == ROW_TILE and seq % ROW_TILE == 0
    silu_c = jax.nn.silu(c)
    silu_cc = jax.nn.silu(c_ctx)
    x2 = jnp.concatenate([x, ctx], axis=1).reshape(n, d)
    group_pads = [(0, OFF_B, A_COLS), (OFF_B, OFF_C, _round_up(B_COLS, LANE)),
                  (OFF_C, OFF_D, _round_up(C_COLS, LANE)), (OFF_D, IN_COLS, _round_up(D_COLS, LANE))]
    for l in range(DEPTH):
        mod = (silu_c @ mod_w[l] + mod_b[l]).reshape(bsz, 6, d)
        mod_c = jnp.broadcast_to((silu_cc @ mod_w[l] + mod_b[l]).reshape(1, 6, d), (bsz, 6, d))
        mods = jnp.stack([mod, mod_c], axis=1)
        sh1, sc1, gt1, sh2, sc2, gt2 = [mods[:, :, j].reshape(2 * bsz, 1, d) for j in range(6)]
        ws = [_pad_cols(w_in[l][:, lo:hi], wd).astype(BF16) for lo, hi, wd in group_pads]
        pa, pb, pc, pd = norm_proj(x2, norm_mix_g[l], sc1, sh1, ws, tiles_per_seq)
        ya = rglru_mixer(pa.reshape(bsz, t, -1), seq, lru_conv_w[l], lru_conv_b[l], lru_w_a[l], lru_b_a[l],
                         lru_w_x[l], lru_b_x[l], lru_lambda[l])
        yb = rwkv7_mixer(pb.reshape(bsz, t, -1), seq, rwkv_mu[l], rwkv_w_up[l], rwkv_w0[l], rwkv_a_up[l],
                         rwkv_a0[l], rwkv_g_up[l], rwkv_k_k[l], rwkv_k_a[l], rwkv_r_k[l],
                         rwkv_ln_g[l], rwkv_ln_b[l])
        yc = mlstm_mixer(pc.reshape(bsz, t, -1), seq, mlstm_i_b[l], mlstm_f_b[l], mlstm_norm_g[l])
        yd = gdn_mixer(pd.reshape(bsz, t, -1), seq, gdn_conv_w[l], gdn_a_log[l], gdn_dt_bias[l], gdn_norm_g[l])
        x2 = out_proj([ya, yb, yc, yd], w_out[l].astype(BF16), x2, gt1, tiles_per_seq)
        if l % 2 == 0:
            x2 = dense_ffn(x2, norm_ffn_g[l], sc2, sh2, gt2, ffn_w_gate[l // 2].astype(BF16),
                           ffn_w_up[l // 2].astype(BF16), ffn_w_down[l // 2].astype(BF16), tiles_per_seq)
        else:
            h2, lg = norm_router(x2, norm_ffn_g[l], sc2, sh2, _pad_cols(moe_router[l // 2], LANE), tiles_per_seq)
            y = moe_ffn(h2, lg[:, :N_EXPERTS], moe_w_gate[l // 2].astype(BF16), moe_w_up[l // 2].astype(BF16),
                        moe_w_down[l // 2].astype(BF16))
            gt_rows = jnp.concatenate([jnp.broadcast_to(gt2.reshape(bsz, 2, 1, d)[:, 0], (bsz, seq, d)),
                                       jnp.broadcast_to(gt2.reshape(bsz, 2, 1, d)[:, 1], (bsz, n_ctx, d))], axis=1)
            x2 = x2 + gt_rows.reshape(n, d) * y
    return final_norm(x2.reshape(bsz, t, d), final_norm_g, seq)
```

```python
import functools
import math

import jax
import jax.numpy as jnp
from jax import lax
from jax.experimental import pallas as pl
from jax.experimental.pallas import tpu as pltpu

D_MODEL = 1024
DEPTH = 4
GRID_W = 64
N_MIXERS = 4
GROUP_W = D_MODEL // N_MIXERS
HEAD_DIM = 64
N_HEADS = GROUP_W // HEAD_DIM
CONV_W = 4
LRU_C = 8.0
RWKV_LORA_W = 32
RWKV_LORA_A = 32
RWKV_LORA_G = 64
RWKV_GN_EPS = 64e-5
CHUNK = 64
D_FF = 2816
N_EXPERTS = 8
TOP_K = 2
D_FF_EXPERT = 1408
MOE_BLOCK = 256
EPS = 1e-6

A_COLS = 2 * GROUP_W
B_COLS = 3 * GROUP_W + 2 * RWKV_LORA_W + 2 * RWKV_LORA_A + RWKV_LORA_G
C_COLS = 4 * GROUP_W + 4 * N_HEADS
D_COLS = 4 * GROUP_W + 4 * N_HEADS
OFF_B = A_COLS
OFF_C = OFF_B + B_COLS
OFF_D = OFF_C + C_COLS
IN_COLS = OFF_D + D_COLS

LANE = 128
SUBLANE = 8
VMEM_LIMIT = 56 * 1024 * 1024
ROW_TILE = 256
F32 = jnp.float32
BF16 = jnp.bfloat16
HI = lax.Precision.HIGHEST


def _round_up(n, m):
    return (n + m - 1) // m * m


def _pad_cols(w, n):
    return jnp.pad(w, ((0, 0), (0, n - w.shape[1])))


def _const_spec(shape):
    nd = len(shape)
    return pl.BlockSpec(shape, lambda *_: (0,) * nd, pipeline_mode=pl.Buffered(1))


def _params(*sem):
    return pltpu.CompilerParams(dimension_semantics=sem, vmem_limit_bytes=VMEM_LIMIT)


def _mod_spec(d, tiles_per_seq):
    def index(i):
        return (2 * (i // tiles_per_seq) + (i % tiles_per_seq) // (tiles_per_seq - 1), 0, 0)
    return pl.BlockSpec((1, 1, d), index)


def _norm_mod(x, g, sc, sh):
    ms = jnp.mean(x * x, axis=-1, keepdims=True)
    return (x * lax.rsqrt(ms + EPS) * g) * (1.0 + sc) + sh


def _dot_dims(a, b, dims, prec):
    if prec is None:
        a, b, prec = a.astype(BF16), b.astype(BF16), None
    return lax.dot_general(a, b, (dims, ((), ())), preferred_element_type=F32, precision=prec)


def _dot(a, b, prec=None):
    return _dot_dims(a, b, ((1,), (0,)), prec)


def _dot_nt(a, b, prec=None):
    return _dot_dims(a, b, ((1,), (1,)), prec)


def _dot_tn(a, b, prec=None):
    return _dot_dims(a, b, ((0,), (0,)), prec)


def _norm_proj_kernel(x_ref, g_ref, sc_ref, sh_ref, *refs, n_out):
    h = _norm_mod(x_ref[...], g_ref[...], sc_ref[0], sh_ref[0]).astype(BF16)
    for w_ref, o_ref in zip(refs[:n_out], refs[n_out:]):
        o_ref[...] = _dot(h, w_ref[...])


def norm_proj(x2d, g, sc, sh, ws, tiles_per_seq):
    n, d = x2d.shape
    tm = ROW_TILE
    mod_spec = _mod_spec(d, tiles_per_seq)
    return pl.pallas_call(
        functools.partial(_norm_proj_kernel, n_out=len(ws)),
        grid=(n // tm,),
        in_specs=[pl.BlockSpec((tm, d), lambda i: (i, 0)), _const_spec((1, d)), mod_spec, mod_spec]
        + [_const_spec(w.shape) for w in ws],
        out_specs=[pl.BlockSpec((tm, w.shape[1]), lambda i: (i, 0)) for w in ws],
        out_shape=[jax.ShapeDtypeStruct((n, w.shape[1]), F32) for w in ws],
        compiler_params=_params("arbitrary"),
        name="norm_proj",
    )(x2d, g.reshape(1, d), sc, sh, *ws)


def _out_proj_kernel(ya_ref, yb_ref, yc_ref, yd_ref, w_ref, x_ref, gt_ref, o_ref):
    y = jnp.concatenate([ya_ref[...], yb_ref[...], yc_ref[...], yd_ref[...]], axis=-1).astype(BF16)
    o_ref[...] = x_ref[...] + gt_ref[0] * _dot(y, w_ref[...])


def out_proj(ys, w, x2d, gt, tiles_per_seq):
    n, d = x2d.shape
    tm = ROW_TILE
    y_spec = pl.BlockSpec((tm, GROUP_W), lambda i: (i, 0))
    return pl.pallas_call(
        _out_proj_kernel,
        grid=(n // tm,),
        in_specs=[y_spec] * 4 + [_const_spec(w.shape), pl.BlockSpec((tm, d), lambda i: (i, 0)),
                                 _mod_spec(d, tiles_per_seq)],
        out_specs=pl.BlockSpec((tm, d), lambda i: (i, 0)),
        out_shape=jax.ShapeDtypeStruct((n, d), F32),
        compiler_params=_params("arbitrary"),
        name="out_proj",
    )(*ys, w, x2d, gt)


def _swiglu_chunks(hb, wg_ref, wu_ref, wd_ref, n_chunks, fc):
    acc = None
    for c in range(n_chunks):
        a = _dot(hb, wg_ref[:, c * fc:(c + 1) * fc])
        u = _dot(hb, wu_ref[:, c * fc:(c + 1) * fc])
        z = (a * jax.nn.sigmoid(a) * u).astype(BF16)
        part = _dot(z, wd_ref[c * fc:(c + 1) * fc, :])
        acc = part if acc is None else acc + part
    return acc


def _dense_ffn_kernel(x_ref, g_ref, sc_ref, sh_ref, gt_ref, wg_ref, wu_ref, wd_ref, o_ref, *, n_chunks, fc):
    x = x_ref[...]
    hb = _norm_mod(x, g_ref[...], sc_ref[0], sh_ref[0]).astype(BF16)
    o_ref[...] = x + gt_ref[0] * _swiglu_chunks(hb, wg_ref, wu_ref, wd_ref, n_chunks, fc)


def dense_ffn(x2d, g, sc, sh, gt, wg, wu, wd, tiles_per_seq):
    n, d = x2d.shape
    tm = ROW_TILE
    fc = D_FF // 2
    mod_spec = _mod_spec(d, tiles_per_seq)
    return pl.pallas_call(
        functools.partial(_dense_ffn_kernel, n_chunks=D_FF // fc, fc=fc),
        grid=(n // tm,),
        in_specs=[pl.BlockSpec((tm, d), lambda i: (i, 0)), _const_spec((1, d)), mod_spec, mod_spec, mod_spec,
                  _const_spec(wg.shape), _const_spec(wu.shape), _const_spec(wd.shape)],
        out_specs=pl.BlockSpec((tm, d), lambda i: (i, 0)),
        out_shape=jax.ShapeDtypeStruct((n, d), F32),
        compiler_params=_params("arbitrary"),
        name="dense_ffn",
    )(x2d, g.reshape(1, d), sc, sh, gt, wg, wu, wd)


def _router_kernel(x_ref, g_ref, sc_ref, sh_ref, r_ref, h_ref, l_ref):
    h = _norm_mod(x_ref[...], g_ref[...], sc_ref[0], sh_ref[0])
    h_ref[...] = h.astype(BF16)
    l_ref[...] = _dot(h, r_ref[...], HI)


def norm_router(x2d, g, sc, sh, router_pad, tiles_per_seq):
    n, d = x2d.shape
    tm = ROW_TILE
    mod_spec = _mod_spec(d, tiles_per_seq)
    return pl.pallas_call(
        _router_kernel,
        grid=(n // tm,),
        in_specs=[pl.BlockSpec((tm, d), lambda i: (i, 0)), _const_spec((1, d)), mod_spec, mod_spec,
                  _const_spec(router_pad.shape)],
        out_specs=[pl.BlockSpec((tm, d), lambda i: (i, 0)), pl.BlockSpec((tm, LANE), lambda i: (i, 0))],
        out_shape=[jax.ShapeDtypeStruct((n, d), BF16), jax.ShapeDtypeStruct((n, LANE), F32)],
        compiler_params=_params("arbitrary"),
        name="norm_router",
    )(x2d, g.reshape(1, d), sc, sh, router_pad)


def _expert_kernel(be_ref, nb_ref, xb_ref, wg_ref, wu_ref, wd_ref, o_ref):
    i = pl.program_id(0)

    @pl.when(i < nb_ref[0])
    def _():
        o_ref[...] = _swiglu_chunks(xb_ref[...], wg_ref.at[0], wu_ref.at[0], wd_ref.at[0], 1, D_FF_EXPERT)

    @pl.when(i >= nb_ref[0])
    def _():
        o_ref[...] = jnp.zeros_like(o_ref)


def expert_ffn(xb, block_e, n_used, wg, wu, wd, tm):
    npad, d = xb.shape
    f = wg.shape[2]
    grid_spec = pltpu.PrefetchScalarGridSpec(
        num_scalar_prefetch=2,
        grid=(npad // tm,),
        in_specs=[pl.BlockSpec((tm, d), lambda i, be, nb: (i, 0)),
                  pl.BlockSpec((1, d, f), lambda i, be, nb: (be[i], 0, 0)),
                  pl.BlockSpec((1, d, f), lambda i, be, nb: (be[i], 0, 0)),
                  pl.BlockSpec((1, f, d), lambda i, be, nb: (be[i], 0, 0))],
        out_specs=pl.BlockSpec((tm, d), lambda i, be, nb: (i, 0)),
    )
    return pl.pallas_call(
        _expert_kernel,
        grid_spec=grid_spec,
        out_shape=jax.ShapeDtypeStruct((npad, d), F32),
        compiler_params=_params("arbitrary"),
        name="expert_ffn",
    )(block_e, n_used, xb, wg, wu, wd)


def moe_ffn(h_bf, logits, wg, wu, wd):
    t, d = h_bf.shape
    top_logits, top_idx = lax.top_k(logits, TOP_K)
    gate = jax.nn.softmax(top_logits, axis=-1)
    tk = t * TOP_K
    flat_e = top_idx.reshape(tk)
    order = jnp.argsort(flat_e)
    se = flat_e[order]
    stok = (order // TOP_K).astype(jnp.int32)
    counts = jnp.zeros((N_EXPERTS,), jnp.int32).at[flat_e].add(1)
    start = jnp.cumsum(counts) - counts
    padded = (counts + MOE_BLOCK - 1) // MOE_BLOCK * MOE_BLOCK
    pad_start = jnp.cumsum(padded) - padded
    pad_end = pad_start + padded
    pos = pad_start[se] + jnp.arange(tk, dtype=jnp.int32) - start[se]
    n_blocks = -(-tk // MOE_BLOCK) + N_EXPERTS
    buf_tok = jnp.zeros((n_blocks * MOE_BLOCK,), jnp.int32).at[pos].set(stok)
    slot_of = jnp.zeros((tk,), jnp.int32).at[order].set(pos).reshape(t, TOP_K)
    block_e = jnp.clip(jnp.searchsorted(pad_end, jnp.arange(n_blocks, dtype=jnp.int32) * MOE_BLOCK,
                                        side='right'), 0, N_EXPERTS - 1).astype(jnp.int32)
    n_used = (pad_end[-1] // MOE_BLOCK).astype(jnp.int32).reshape(1)
    yb = expert_ffn(h_bf[buf_tok], block_e, n_used, wg, wu, wd, MOE_BLOCK)
    return gate[:, 0:1] * yb[slot_of[:, 0]] + gate[:, 1:2] * yb[slot_of[:, 1]]


def _final_norm_kernel(x_ref, g_ref, o_ref):
    x = x_ref[...]
    ms = jnp.mean(x * x, axis=-1, keepdims=True)
    o_ref[...] = x * lax.rsqrt(ms + EPS) * g_ref[...]


def final_norm(x3d, g, n_lat):
    bsz, t, d = x3d.shape
    tm = ROW_TILE
    return pl.pallas_call(
        _final_norm_kernel,
        grid=(bsz, n_lat // tm),
        in_specs=[pl.BlockSpec((None, tm, d), lambda b, i: (b, i, 0)), _const_spec((1, d))],
        out_specs=pl.BlockSpec((None, tm, d), lambda b, i: (b, i, 0)),
        out_shape=jax.ShapeDtypeStruct((bsz, n_lat, d), F32),
        compiler_params=_params("arbitrary", "arbitrary"),
        name="final_norm",
    )(x3d, g.reshape(1, d))


def _same_head(shape):
    r = lax.broadcasted_iota(jnp.int32, shape, 0) // HEAD_DIM
    c = lax.broadcasted_iota(jnp.int32, shape, 1) // HEAD_DIM
    return r == c


def _to_block_diag(x, mask):
    return jnp.where(mask, jnp.concatenate([x] * N_HEADS, axis=0), 0.0)


def _token_masks(reverse):
    t = lax.broadcasted_iota(jnp.int32, (CHUNK, GROUP_W), 0)
    s = lax.broadcasted_iota(jnp.int32, (CHUNK, GROUP_W), 1) % HEAD_DIM
    if reverse:
        return s > t, s >= t, s == t
    return s < t, s <= t, s == t


def _cum_matrix(reverse):
    t = lax.broadcasted_iota(jnp.int32, (CHUNK, CHUNK), 0)
    s = lax.broadcasted_iota(jnp.int32, (CHUNK, CHUNK), 1)
    return jnp.where((s >= t) if reverse else (s <= t), 1.0, 0.0).astype(F32)


def _unit_tri_inverse(l_wide, bd_mask):
    t = lax.broadcasted_iota(jnp.int32, (CHUNK, GROUP_W), 0)
    s = lax.broadcasted_iota(jnp.int32, (CHUNK, GROUP_W), 1) % HEAD_DIM
    blk16 = (t // 16) == (s // 16)
    blk32 = (t // 32) == (s // 32)
    p = jnp.where(blk16, -l_wide, 0.0)
    x = jnp.where(t == s, 1.0, 0.0) + p
    for _ in range(3):
        p = _dot(p, _to_block_diag(p, bd_mask))
        x = x + _dot(x, _to_block_diag(p, bd_mask))
    for off_diag in (jnp.logical_and(blk32, jnp.logical_not(blk16)), jnp.logical_not(blk32)):
        e = _to_block_diag(jnp.where(off_diag, l_wide, 0.0), bd_mask)
        x = x - _dot(_dot(x, e), _to_block_diag(x, bd_mask))
    return x


def _head_lane_expander(col0):
    r = lax.broadcasted_iota(jnp.int32, (LANE, GROUP_W), 0)
    c = lax.broadcasted_iota(jnp.int32, (LANE, GROUP_W), 1) // HEAD_DIM
    return jnp.where(r == col0 + c, 1.0, 0.0).astype(F32)


def _fwd_chunk(j, n_lat_chunks, n_chunks):
    n_ctx_chunks = n_chunks - n_lat_chunks
    return jnp.where(j < n_ctx_chunks, n_lat_chunks + j, j - n_ctx_chunks)


def _bwd_chunk(j, n_lat_chunks, n_chunks):
    return n_chunks - 1 - j


def _seq_edges(c, n_lat_chunks, n_chunks):
    first = jnp.logical_or(c == 0, c == n_lat_chunks)
    last = jnp.logical_or(c == n_lat_chunks - 1, c == n_chunks - 1)
    return first, last


def _shift_rows(x, prev8, next8, back):
    row = lax.broadcasted_iota(jnp.int32, (CHUNK, 1), 0)
    if back == -1:
        return jnp.where(row == CHUNK - 1, next8[0:1, :], pltpu.roll(x, CHUNK - 1, 0))
    out = pltpu.roll(x, back, 0)
    for i in range(back):
        out = jnp.where(row == i, prev8[SUBLANE - back + i:SUBLANE - back + i + 1, :], out)
    return out


def _natural_scan_specs(width, out_width, n_lat_chunks, n_chunks, last8):
    r8 = CHUNK // SUBLANE
    fwd = functools.partial(_fwd_chunk, n_lat_chunks=n_lat_chunks, n_chunks=n_chunks)
    bwd = functools.partial(_bwd_chunk, n_lat_chunks=n_lat_chunks, n_chunks=n_chunks)
    chunk = lambda f, w: pl.BlockSpec((None, CHUNK, w), lambda b, j: (b, f(j), 0))
    prev = lambda f: pl.BlockSpec((None, SUBLANE, width), lambda b, j: (b, jnp.maximum(f(j) * r8 - 1, 0), 0))
    nxt = lambda f: pl.BlockSpec((None, SUBLANE, width), lambda b, j: (b, jnp.minimum((f(j) + 1) * r8, last8), 0))
    ins = [chunk(fwd, width), prev(fwd), nxt(fwd), chunk(bwd, width), prev(bwd), nxt(bwd)]
    return ins, chunk(fwd, out_width), chunk(bwd, out_width)


def _chunk_specs(width, n_steps, first_row_block, column_view):
    if column_view:
        return (pl.BlockSpec((None, CHUNK, width), lambda b, j: (b, 0, j)),
                pl.BlockSpec((None, CHUNK, width), lambda b, j: (b, 0, n_steps - 1 - j)))
    return (pl.BlockSpec((None, CHUNK, width), lambda b, j: (b, first_row_block + j, 0)),
            pl.BlockSpec((None, CHUNK, width), lambda b, j: (b, first_row_block + n_steps - 1 - j, 0)))


def _halo_specs(width, n_steps, first_row_block, column_view, last8):
    r8 = CHUNK // SUBLANE
    fwd = lambda j: j
    bwd = lambda j: n_steps - 1 - j
    if column_view:
        prev = lambda f: pl.BlockSpec((None, SUBLANE, width), lambda b, j: (b, r8 - 1, jnp.maximum(f(j) - 1, 0)))
        nxt = lambda f: pl.BlockSpec((None, SUBLANE, width), lambda b, j: (b, 0, jnp.minimum(f(j) + 1, n_steps - 1)))
    else:
        prev = lambda f: pl.BlockSpec((None, SUBLANE, width),
                                      lambda b, j: (b, jnp.maximum((first_row_block + f(j)) * r8 - 1, 0), 0))
        nxt = lambda f: pl.BlockSpec((None, SUBLANE, width),
                                     lambda b, j: (b, jnp.minimum((first_row_block + f(j) + 1) * r8, last8), 0))
    return prev(fwd), nxt(fwd), prev(bwd), nxt(bwd)


def _gated_head_norm_kernel(hf_ref, hb_ref, o_ref, g_ref, y_ref, *, silu_gate):
    h = hf_ref[...] + hb_ref[...]
    seg_mean = jnp.where(_same_head((GROUP_W, GROUP_W)), 1.0 / HEAD_DIM, 0.0).astype(F32)
    hn = h * lax.rsqrt(_dot(h * h, seg_mean, HI) + EPS) * g_ref[...]
    gate = o_ref[...]
    y_ref[...] = hn * (gate * jax.nn.sigmoid(gate) if silu_gate else jax.nn.sigmoid(gate))


def gated_head_norm(hf, hb, p2d, norm_g, silu_gate):
    n, G = hf.shape
    tm = ROW_TILE
    row_spec = pl.BlockSpec((tm, G), lambda i: (i, 0))
    return pl.pallas_call(
        functools.partial(_gated_head_norm_kernel, silu_gate=silu_gate),
        grid=(n // tm,),
        in_specs=[row_spec, row_spec, pl.BlockSpec((tm, G), lambda i: (i, 3)), _const_spec((1, G))],
        out_specs=row_spec,
        out_shape=jax.ShapeDtypeStruct((n, G), F32),
        compiler_params=_params("arbitrary"),
        name="gated_head_norm",
    )(hf, hb, p2d, norm_g.reshape(1, G))


def _rwkv_chunk(p, prev_row, next_row, prm, s_ref, d, reverse):
    (mu_ref, wup_ref, w0_ref, aup_ref, a0_ref, gup_ref, kk_ref, ka_ref, rk_ref) = prm
    G = GROUP_W
    row = lax.broadcasted_iota(jnp.int32, (CHUNK, 1), 0)
    prev = jnp.where(row == 0, prev_row, pltpu.roll(p, 1, 0))
    nxt = jnp.where(row == CHUNK - 1, next_row, pltpu.roll(p, CHUNK - 1, 0))
    ps = p + mu_ref[0:1, :] * (prev - p) + mu_ref[1:2, :] * (nxt - p)
    r, k, v = ps[:, :G], ps[:, G:2 * G], ps[:, 2 * G:3 * G]
    lora_in = ps[:, 3 * G:3 * G + LANE]
    gate_in = ps[:, 3 * G + LANE:3 * G + 2 * LANE]
    bd_mask = _same_head((G, G))
    head_ones = jnp.where(bd_mask, 1.0, 0.0).astype(F32)
    log_w = -math.exp(-0.5) * jax.nn.sigmoid(_dot(jnp.tanh(lora_in), wup_ref[d], HI) + w0_ref[d:d + 1, :])
    a = jax.nn.sigmoid(_dot(lora_in, aup_ref[d], HI) + a0_ref[d:d + 1, :])
    kr = k * kk_ref[...]
    kk = kr * lax.rsqrt(_dot(kr * kr, head_ones, HI) + EPS)
    k_mod = k * (1.0 + (a - 1.0) * ka_ref[...])
    bonus = _dot(r * k_mod * rk_ref[...], head_ones, HI) * v
    lg = _dot(_cum_matrix(reverse), log_w, HI)
    e_neg = jnp.exp(-lg)
    at = kk * jnp.exp(lg - log_w)
    bt = -(kk * a) * e_neg
    kt = k_mod * e_neg
    rt = r * jnp.exp(lg)
    g_last = jnp.exp(lg[0:1, :] if reverse else lg[CHUNK - 1:CHUNK, :])
    strict, incl, diag = _token_masks(reverse)
    lhs = jnp.concatenate([at, rt], axis=0)
    m_b = _dot_nt(lhs, _to_block_diag(bt, bd_mask))
    m_k = _dot_nt(lhs, _to_block_diag(kt, bd_mask))
    a_ab = jnp.where(strict, m_b[:CHUNK], 0.0)
    a_ak = jnp.where(strict, m_k[:CHUNK], 0.0)
    a_rb = jnp.where(incl, m_b[CHUNK:], 0.0)
    a_rk = jnp.where(incl, m_k[CHUNK:], 0.0)
    tinv = _unit_tri_inverse(-a_ab, bd_mask)
    s = s_ref[...]
    v_bd = _to_block_diag(v, bd_mask)
    u = _dot(tinv, _to_block_diag(_dot_nt(at, s) + _dot(a_ak, v_bd), bd_mask))
    y = _dot_nt(rt, s) + _dot(a_rb, _to_block_diag(u, bd_mask)) + _dot(a_rk, v_bd)
    upd = _dot_tn(jnp.concatenate([u, v], axis=0), jnp.concatenate([bt, kt], axis=0))
    s_ref[...] = g_last * (s + jnp.where(bd_mask, upd, 0.0))
    gate = _dot(jax.nn.sigmoid(gate_in), gup_ref[...], HI)
    return y, bonus, gate


def _rwkv_kernel(pf_ref, pfp_ref, pfn_ref, pb_ref, pbp_ref, pbn_ref,
                 mu_ref, wup_ref, w0_ref, aup_ref, a0_ref, gup_ref, kk_ref, ka_ref, rk_ref,
                 yf_ref, bf_ref, g_ref, yb_ref, bb_ref, sf_ref, sb_ref, *, n_lat_chunks, n_chunks):
    j = pl.program_id(1)

    @pl.when(j == 0)
    def _():
        sf_ref[...] = jnp.zeros_like(sf_ref)
        sb_ref[...] = jnp.zeros_like(sb_ref)

    prm = (mu_ref, wup_ref, w0_ref, aup_ref, a0_ref, gup_ref, kk_ref, ka_ref, rk_ref)
    first, last = _seq_edges(_fwd_chunk(j, n_lat_chunks, n_chunks), n_lat_chunks, n_chunks)
    prev_row = jnp.where(first, 0.0, pfp_ref[SUBLANE - 1:SUBLANE, :])
    next_row = jnp.where(last, 0.0, pfn_ref[0:1, :])
    y, bonus, gate = _rwkv_chunk(pf_ref[...], prev_row, next_row, prm, sf_ref, 0, False)
    yf_ref[...] = y
    bf_ref[...] = bonus
    g_ref[...] = gate
    first, last = _seq_edges(_bwd_chunk(j, n_lat_chunks, n_chunks), n_lat_chunks, n_chunks)
    prev_row = jnp.where(first, 0.0, pbp_ref[SUBLANE - 1:SUBLANE, :])
    next_row = jnp.where(last, 0.0, pbn_ref[0:1, :])
    y, bonus, _ = _rwkv_chunk(pb_ref[...], prev_row, next_row, prm, sb_ref, 1, True)
    yb_ref[...] = y
    bb_ref[...] = bonus


def _rwkv_finish_kernel(yf_ref, yb_ref, bf_ref, bb_ref, g_ref, lng_ref, lnb_ref, o_ref):
    y = yf_ref[...] + yb_ref[...]
    seg_mean = jnp.where(_same_head((GROUP_W, GROUP_W)), 1.0 / HEAD_DIM, 0.0).astype(F32)
    mean = _dot(y, seg_mean, HI)
    var = _dot(jnp.square(y - mean), seg_mean, HI)
    yn = (y - mean) * lax.rsqrt(var + RWKV_GN_EPS) * lng_ref[...] + lnb_ref[...]
    o_ref[...] = (yn + bf_ref[...] + bb_ref[...]) * g_ref[...]


def rwkv7_mixer(p3d, n_lat, mu, w_up, w0, a_up, a0, g_up, k_k, k_a, r_k, ln_g, ln_b):
    bsz, t, width = p3d.shape
    G = GROUP_W
    n_chunks = t // CHUNK
    n_lat_chunks = n_lat // CHUNK
    mu_pad = _pad_cols(mu, width)
    wup = jnp.zeros((2, LANE, G), F32)
    aup = jnp.zeros((2, LANE, G), F32)
    for d in range(2):
        wup = wup.at[d, d * RWKV_LORA_W:(d + 1) * RWKV_LORA_W].set(w_up[d])
        lo = 2 * RWKV_LORA_W + d * RWKV_LORA_A
        aup = aup.at[d, lo:lo + RWKV_LORA_A].set(a_up[d])
    gup = jnp.zeros((LANE, G), F32).at[:RWKV_LORA_G].set(g_up)
    consts = [mu_pad, wup, w0, aup, a0, gup, k_k.reshape(1, G), k_a.reshape(1, G), r_k.reshape(1, G)]
    ins, f_out, b_out = _natural_scan_specs(width, G, n_lat_chunks, n_chunks, t // SUBLANE - 1)
    out_sds = jax.ShapeDtypeStruct((bsz, t, G), F32)
    yf, bf, gate, yb, bb = pl.pallas_call(
        functools.partial(_rwkv_kernel, n_lat_chunks=n_lat_chunks, n_chunks=n_chunks),
        grid=(bsz, n_chunks),
        in_specs=ins + [_const_spec(c.shape) for c in consts],
        out_specs=[f_out] * 3 + [b_out] * 2,
        out_shape=[out_sds] * 5,
        scratch_shapes=[pltpu.VMEM((G, G), F32), pltpu.VMEM((G, G), F32)],
        compiler_params=_params("arbitrary", "arbitrary"),
        name="rwkv_scan",
    )(p3d, p3d, p3d, p3d, p3d, p3d, *consts)
    n = bsz * t
    tm = ROW_TILE
    row_spec = pl.BlockSpec((tm, G), lambda i: (i, 0))
    return pl.pallas_call(
        _rwkv_finish_kernel,
        grid=(n // tm,),
        in_specs=[row_spec] * 5 + [_const_spec((1, G))] * 2,
        out_specs=row_spec,
        out_shape=jax.ShapeDtypeStruct((n, G), F32),
        compiler_params=_params("arbitrary"),
        name="rwkv_finish",
    )(*[a.reshape(n, G) for a in (yf, yb, bf, bb, gate)], ln_g.reshape(1, G), ln_b.reshape(1, G))


def _neg_expm1(x):
    series = -x * (1.0 + x * (1 / 2 + x * (1 / 6 + x * (1 / 24 + x * (1 / 120 + x * (1 / 720 + x / 5040))))))
    return jnp.where(x > -0.05, series, 1.0 - jnp.exp(x))


def _lru_chunk(p, prev8, next8, cw_ref, cb_ref, w_ref, bias_ref, lam_ref, h_ref, d, reverse):
    G = GROUP_W
    x = p[:, :G]
    u = (cw_ref[0:1, :] * _shift_rows(x, prev8, next8, 2) + cw_ref[1:2, :] * _shift_rows(x, prev8, next8, 1)
         + cw_ref[2:3, :] * x + cw_ref[3:4, :] * _shift_rows(x, prev8, next8, -1)) + cb_ref[...]
    pre = _dot(u, w_ref[d]) + bias_ref[d:d + 1, :]
    r = jax.nn.sigmoid(pre[:, :G])
    i = jax.nn.sigmoid(pre[:, G:])
    log_a = -LRU_C * r * jax.nn.softplus(-lam_ref[d:d + 1, :])
    a = jnp.exp(log_a)
    b = jnp.sqrt(_neg_expm1(2.0 * log_a)) * (i * u)
    row = lax.broadcasted_iota(jnp.int32, (CHUNK, 1), 0)
    k = 1
    while k < CHUNK:
        shift = CHUNK - k if reverse else k
        valid = (row < CHUNK - k) if reverse else (row >= k)
        b = jnp.where(valid, a * pltpu.roll(b, shift, 0) + b, b)
        a = jnp.where(valid, a * pltpu.roll(a, shift, 0), a)
        k *= 2
    h = b + a * h_ref[0:1, :]
    h_ref[0:1, :] = h[0:1, :] if reverse else h[CHUNK - 1:CHUNK, :]
    return h


def _lru_kernel(pf_ref, pfp_ref, pfn_ref, pb_ref, pbp_ref, pbn_ref, cw_ref, cb_ref, w_ref, bias_ref, lam_ref,
                hf_ref, hb_ref, st_ref, *, n_lat_chunks, n_chunks):
    j = pl.program_id(1)

    @pl.when(j == 0)
    def _():
        st_ref[...] = jnp.zeros_like(st_ref)

    G = GROUP_W
    first, last = _seq_edges(_fwd_chunk(j, n_lat_chunks, n_chunks), n_lat_chunks, n_chunks)
    prev8 = jnp.where(first, 0.0, pfp_ref[:, :G])
    next8 = jnp.where(last, 0.0, pfn_ref[:, :G])
    hf_ref[...] = _lru_chunk(pf_ref[...], prev8, next8, cw_ref, cb_ref, w_ref, bias_ref, lam_ref, st_ref.at[0], 0, False)
    first, last = _seq_edges(_bwd_chunk(j, n_lat_chunks, n_chunks), n_lat_chunks, n_chunks)
    prev8 = jnp.where(first, 0.0, pbp_ref[:, :G])
    next8 = jnp.where(last, 0.0, pbn_ref[:, :G])
    hb_ref[...] = _lru_chunk(pb_ref[...], prev8, next8, cw_ref, cb_ref, w_ref, bias_ref, lam_ref, st_ref.at[1], 1, True)


def _lru_finish_kernel(hf_ref, hb_ref, gate_ref, y_ref):
    y_ref[...] = (hf_ref[...] + hb_ref[...]) * jax.nn.gelu(gate_ref[...])


def rglru_mixer(p3d, n_lat, conv_w, conv_b, w_a, b_a, w_x, b_x, lam):
    bsz, t, width = p3d.shape
    G = GROUP_W
    n_chunks = t // CHUNK
    n_lat_chunks = n_lat // CHUNK
    w = jnp.zeros((2, G, 2 * G), F32)
    for h in range(N_HEADS):
        sl = slice(h * HEAD_DIM, (h + 1) * HEAD_DIM)
        w = w.at[:, sl, sl].set(w_a[:, h])
        w = w.at[:, sl, G + h * HEAD_DIM:G + (h + 1) * HEAD_DIM].set(w_x[:, h])
    consts = [jnp.pad(conv_w, ((0, SUBLANE - CONV_W), (0, 0))), conv_b.reshape(1, G), w,
              jnp.concatenate([b_a, b_x], axis=1), lam]
    ins, f_out, b_out = _natural_scan_specs(width, G, n_lat_chunks, n_chunks, t // SUBLANE - 1)
    h_sds = jax.ShapeDtypeStruct((bsz, t, G), F32)
    hf, hb = pl.pallas_call(
        functools.partial(_lru_kernel, n_lat_chunks=n_lat_chunks, n_chunks=n_chunks),
        grid=(bsz, n_chunks),
        in_specs=ins + [_const_spec(c.shape) for c in consts],
        out_specs=[f_out, b_out],
        out_shape=[h_sds, h_sds],
        scratch_shapes=[pltpu.VMEM((2, SUBLANE, G), F32)],
        compiler_params=_params("arbitrary", "arbitrary"),
        name="lru_scan",
    )(p3d, p3d, p3d, p3d, p3d, p3d, *consts)
    n = bsz * t
    tm = ROW_TILE
    row_spec = pl.BlockSpec((tm, G), lambda i: (i, 0))
    return pl.pallas_call(
        _lru_finish_kernel,
        grid=(n // tm,),
        in_specs=[row_spec, row_spec, pl.BlockSpec((tm, G), lambda i: (i, 1))],
        out_specs=row_spec,
        out_shape=jax.ShapeDtypeStruct((n, G), F32),
        compiler_params=_params("arbitrary"),
        name="lru_finish",
    )(hf.reshape(n, G), hb.reshape(n, G), p3d.reshape(n, width))


def _gdn_chunk(p, prev8, next8, cw_ref, al_ref, dt_ref, s, d, reverse):
    G = GROUP_W
    x = p[:, :3 * G]
    conv = (cw_ref[0:1, :] * _shift_rows(x, prev8, next8, 2) + cw_ref[1:2, :] * _shift_rows(x, prev8, next8, 1)
            + cw_ref[2:3, :] * x + cw_ref[3:4, :] * _shift_rows(x, prev8, next8, -1))
    qkv = conv * jax.nn.sigmoid(conv)
    gates = p[:, 4 * G:4 * G + LANE]
    bd_mask = _same_head((G, G))
    head_ones = jnp.where(bd_mask, 1.0, 0.0).astype(F32)
    q, k, v = qkv[:, :G], qkv[:, G:2 * G], qkv[:, 2 * G:]
    q = q * lax.rsqrt(_dot(q * q, head_ones, HI) + EPS) * (HEAD_DIM ** -0.5)
    k = k * lax.rsqrt(_dot(k * k, head_ones, HI) + EPS)
    la = -al_ref[d:d + 1, :] * jax.nn.softplus(_dot(gates, _head_lane_expander(d * N_HEADS), HI) + dt_ref[d:d + 1, :])
    beta = jax.nn.sigmoid(_dot(gates, _head_lane_expander((2 + d) * N_HEADS), HI))
    gam = _dot(_cum_matrix(reverse), la, HI)
    strict, incl, diag = _token_masks(reverse)
    gam_row = jnp.sum(jnp.where(diag, gam, 0.0), axis=0, keepdims=True)
    decay = jnp.exp(jnp.where(incl, gam - gam_row, -jnp.inf))
    k_beta = k * beta
    k_bd = _to_block_diag(k, bd_mask)
    lower = jnp.where(strict, _dot_nt(k_beta, k_bd) * decay, 0.0)
    tinv = _unit_tri_inverse(lower, bd_mask)
    e_gam = jnp.exp(gam)
    u = _dot(tinv, _to_block_diag(v * beta, bd_mask))
    w = _dot(tinv, _to_block_diag(k_beta * e_gam, bd_mask))
    qk = _dot_nt(q, k_bd) * decay
    gam_last = gam[0:1, :] if reverse else gam[CHUNK - 1:CHUNK, :]
    v_new = u - _dot(w, s)
    o = _dot(q * e_gam, s) + _dot(qk, _to_block_diag(v_new, bd_mask))
    s_new = jnp.exp(gam_last) * s + jnp.where(bd_mask, _dot_tn(k * jnp.exp(gam_last - gam), v_new), 0.0)
    return o, s_new


def _gdn_kernel(pf_ref, pfp_ref, pfn_ref, pb_ref, pbp_ref, pbn_ref, cw_ref, al_ref, dt_ref, s0_ref,
                of_ref, ob_ref, s_ref, *, n_steps):
    j = pl.program_id(1)

    @pl.when(j == 0)
    def _():
        s_ref[...] = s0_ref[...]

    width = 3 * GROUP_W
    prev8 = jnp.where(j == 0, 0.0, pfp_ref[:, :width])
    next8 = jnp.where(j == n_steps - 1, 0.0, pfn_ref[:, :width])
    of, sf = _gdn_chunk(pf_ref[...], prev8, next8, cw_ref, al_ref, dt_ref, s_ref[0], 0, False)
    prev8 = jnp.where(j == n_steps - 1, 0.0, pbp_ref[:, :width])
    next8 = jnp.where(j == 0, 0.0, pbn_ref[:, :width])
    ob, sb = _gdn_chunk(pb_ref[...], prev8, next8, cw_ref, al_ref, dt_ref, s_ref[1], 1, True)
    of_ref[...] = of
    ob_ref[...] = ob
    s_ref[...] = jnp.stack([sf, sb])


def gdn_scan(p, n_steps, first_row_block, column_view, last8, cw, al, dt, s0):
    bsz = p.shape[0]
    G = GROUP_W
    width = _round_up(D_COLS, LANE)
    f_in, b_in = _chunk_specs(width, n_steps, first_row_block, column_view)
    fp, fn, bp, bn = _halo_specs(width, n_steps, first_row_block, column_view, last8)
    f_out, b_out = _chunk_specs(G, n_steps, 0, column_view)
    h_sds = jax.ShapeDtypeStruct((bsz, CHUNK, GRID_W * G) if column_view else (bsz, n_steps * CHUNK, G), F32)
    state_spec = pl.BlockSpec((None, 2, G, G), lambda b, j: (b, 0, 0, 0))
    return pl.pallas_call(
        functools.partial(_gdn_kernel, n_steps=n_steps),
        grid=(bsz, n_steps),
        in_specs=[f_in, fp, fn, b_in, bp, bn, _const_spec(cw.shape), _const_spec(al.shape), _const_spec(dt.shape),
                  state_spec],
        out_specs=[f_out, b_out, state_spec],
        out_shape=[h_sds, h_sds, jax.ShapeDtypeStruct(s0.shape, F32)],
        compiler_params=_params("arbitrary", "arbitrary"),
        name="gdn_scan",
    )(p, p, p, p, p, p, cw, al, dt, s0)


def gdn_mixer(p3d, n_lat, conv_w, a_log, dt_bias, norm_g):
    bsz, t, width = p3d.shape
    G = GROUP_W
    n_ctx = t - n_lat
    assert n_lat == CHUNK * GRID_W
    al = jnp.repeat(jnp.exp(a_log), HEAD_DIM, axis=1)
    dt = jnp.repeat(dt_bias, HEAD_DIM, axis=1)
    cw = jnp.pad(conv_w, ((0, SUBLANE - CONV_W), (0, 0)))
    s0 = jnp.zeros((bsz, 2, G, G), F32)
    last8 = t // SUBLANE - 1
    of_c, ob_c, s1 = gdn_scan(p3d, n_ctx // CHUNK, n_lat // CHUNK, False, last8, cw, al, dt, s0)
    of_l, ob_l, _ = gdn_scan(p3d.reshape(bsz, t // GRID_W, GRID_W * width), GRID_W, 0, True, last8, cw, al, dt, s1)
    hf = jnp.concatenate([of_l.reshape(bsz, n_lat, G), of_c], axis=1).reshape(bsz * t, G)
    hb = jnp.concatenate([ob_l.reshape(bsz, n_lat, G), ob_c], axis=1).reshape(bsz * t, G)
    return gated_head_norm(hf, hb, p3d.reshape(bsz * t, width), norm_g, True)


def _head_max(x):
    lane_head = lax.broadcasted_iota(jnp.int32, x.shape, 1) // HEAD_DIM
    out = jnp.zeros_like(x)
    for h in range(N_HEADS):
        mh = jnp.max(jnp.where(lane_head == h, x, -jnp.inf), axis=-1, keepdims=True)
        out = jnp.where(lane_head == h, mh, out)
    return out


def _mlstm_chunk(p, ib_ref, fb_ref, c_ref, nm_ref, d, reverse):
    G = GROUP_W
    q, k, v = p[:, :G], p[:, G:2 * G] * (HEAD_DIM ** -0.5), p[:, 2 * G:3 * G]
    gates = p[:, 4 * G:4 * G + LANE]
    bd_mask = _same_head((G, G))
    head_ones = jnp.where(bd_mask, 1.0, 0.0).astype(F32)
    li = _dot(gates, _head_lane_expander(d * N_HEADS), HI) + ib_ref[d:d + 1, :]
    lf = jax.nn.log_sigmoid(_dot(gates, _head_lane_expander((2 + d) * N_HEADS), HI) + fb_ref[d:d + 1, :])
    b = _dot(_cum_matrix(reverse), lf, HI)
    _, incl, diag = _token_masks(reverse)
    g = li - b
    g_row = jnp.sum(jnp.where(diag, g, 0.0), axis=0, keepdims=True)
    log_d = jnp.where(incl, b + g_row, -jnp.inf)
    m_intra = _head_max(log_d)
    mem, nrm, m_prev = c_ref[...], nm_ref[0:1, :], nm_ref[1:2, :]
    m_inter = b + m_prev
    m_t = jnp.maximum(m_inter, m_intra)
    w_inter = jnp.exp(m_inter - m_t)
    pm = jnp.exp(log_d - m_t) * _dot_nt(q, _to_block_diag(k, bd_mask))
    num = _dot(pm, _to_block_diag(v, bd_mask)) + w_inter * _dot_nt(q, mem)
    den = _dot(pm, head_ones, HI) + w_inter * _dot(q * nrm, head_ones, HI)
    h = num / jnp.maximum(jnp.abs(den), jnp.exp(-m_t))
    b_last = b[0:1, :] if reverse else b[CHUNK - 1:CHUNK, :]
    log_end = b_last + g
    m_new = jnp.maximum(b_last + m_prev, jnp.max(log_end, axis=0, keepdims=True))
    w_prev = jnp.exp(b_last + m_prev - m_new)
    w_new = jnp.exp(log_end - m_new)
    c_ref[...] = w_prev * mem + jnp.where(bd_mask, _dot_tn(w_new * v, k), 0.0)
    nm_ref[0:1, :] = w_prev * nrm + jnp.sum(w_new * k, axis=0, keepdims=True)
    nm_ref[1:2, :] = m_new
    return h


def _mlstm_kernel(pf_ref, pb_ref, ib_ref, fb_ref, c0_ref, nm0_ref, hf_ref, hb_ref, c_ref, nm_ref):
    @pl.when(pl.program_id(1) == 0)
    def _():
        c_ref[...] = c0_ref[...]
        nm_ref[...] = nm0_ref[...]

    hf_ref[...] = _mlstm_chunk(pf_ref[...], ib_ref, fb_ref, c_ref.at[0], nm_ref.at[0], 0, False)
    hb_ref[...] = _mlstm_chunk(pb_ref[...], ib_ref, fb_ref, c_ref.at[1], nm_ref.at[1], 1, True)


def mlstm_scan(p, n_steps, first_row_block, column_view, ib, fb, c0, nm0):
    bsz = p.shape[0]
    G = GROUP_W
    width = _round_up(C_COLS, LANE)
    f_in, b_in = _chunk_specs(width, n_steps, first_row_block, column_view)
    f_out, b_out = _chunk_specs(G, n_steps, 0, column_view)
    h_sds = jax.ShapeDtypeStruct((bsz, CHUNK, GRID_W * G) if column_view else (bsz, n_steps * CHUNK, G), F32)
    state_specs = [pl.BlockSpec((None, 2, G, G), lambda b, j: (b, 0, 0, 0)),
                   pl.BlockSpec((None, 2, SUBLANE, G), lambda b, j: (b, 0, 0, 0))]
    return pl.pallas_call(
        _mlstm_kernel,
        grid=(bsz, n_steps),
        in_specs=[f_in, b_in, _const_spec(ib.shape), _const_spec(fb.shape)] + state_specs,
        out_specs=[f_out, b_out] + state_specs,
        out_shape=[h_sds, h_sds, jax.ShapeDtypeStruct(c0.shape, F32), jax.ShapeDtypeStruct(nm0.shape, F32)],
        compiler_params=_params("arbitrary", "arbitrary"),
        name="mlstm_scan",
    )(p, p, ib, fb, c0, nm0)


def mlstm_mixer(p3d, n_lat, i_b, f_b, norm_g):
    bsz, t, width = p3d.shape
    G = GROUP_W
    n_ctx = t - n_lat
    assert n_lat == CHUNK * GRID_W
    ib = jnp.repeat(i_b, HEAD_DIM, axis=1)
    fb = jnp.repeat(f_b, HEAD_DIM, axis=1)
    c0 = jnp.zeros((bsz, 2, G, G), F32)
    nm0 = jnp.zeros((bsz, 2, SUBLANE, G), F32)
    hf_c, hb_c, c1, nm1 = mlstm_scan(p3d, n_ctx // CHUNK, n_lat // CHUNK, False, ib, fb, c0, nm0)
    hf_l, hb_l, _, _ = mlstm_scan(p3d.reshape(bsz, t // GRID_W, GRID_W * width), GRID_W, 0, True, ib, fb, c1, nm1)
    hf = jnp.concatenate([hf_l.reshape(bsz, n_lat, G), hf_c], axis=1).reshape(bsz * t, G)
    hb = jnp.concatenate([hb_l.reshape(bsz, n_lat, G), hb_c], axis=1).reshape(bsz * t, G)
    return gated_head_norm(hf, hb, p3d.reshape(bsz * t, width), norm_g, False)


def kernel(x, c, ctx, c_ctx, mod_w, mod_b, norm_mix_g, norm_ffn_g, w_in, w_out, lru_conv_w, lru_conv_b, lru_w_a, lru_b_a, lru_w_x, lru_b_x, lru_lambda, rwkv_mu, rwkv_w_up, rwkv_w0, rwkv_a_up, rwkv_a0, rwkv_g_up, rwkv_k_k, rwkv_k_a, rwkv_r_k, rwkv_ln_g, rwkv_ln_b, mlstm_i_b, mlstm_f_b, mlstm_norm_g, gdn_conv_w, gdn_a_log, gdn_dt_bias, gdn_norm_g, ffn_w_gate, ffn_w_up, ffn_w_down, moe_router, moe_w_gate, moe_w_up, moe_w_down, final_norm_g):
    bsz, seq, d = x.shape
    n_ctx = ctx.shape[1]
    t = n_ctx + seq
    n = bsz * t
    tiles_per_seq = t // ROW_TILE
    assert n_ctx == ROW_TILE and seq % ROW_TILE == 0
    silu_c = jax.nn.silu(c)
    silu_cc = jax.nn.silu(c_ctx)
    x2 = jnp.concatenate([x, ctx], axis=1).reshape(n, d)
    group_pads = [(0, OFF_B, A_COLS), (OFF_B, OFF_C, _round_up(B_COLS, LANE)),
                  (OFF_C, OFF_D, _round_up(C_COLS, LANE)), (OFF_D, IN_COLS, _round_up(D_COLS, LANE))]
    for l in range(DEPTH):
        mod = (silu_c @ mod_w[l] + mod_b[l]).reshape(bsz, 6, d)
        mod_c = jnp.broadcast_to((silu_cc @ mod_w[l] + mod_b[l]).reshape(1, 6, d), (bsz, 6, d))
        mods = jnp.stack([mod, mod_c], axis=1)
        sh1, sc1, gt1, sh2, sc2, gt2 = [mods[:, :, j].reshape(2 * bsz, 1, d) for j in range(6)]
        ws = [_pad_cols(w_in[l][:, lo:hi], wd).astype(BF16) for lo, hi, wd in group_pads]
        pa, pb, pc, pd = norm_proj(x2, norm_mix_g[l], sc1, sh1, ws, tiles_per_seq)
        ya = rglru_mixer(pa.reshape(bsz, t, -1), seq, lru_conv_w[l], lru_conv_b[l], lru_w_a[l], lru_b_a[l],
                         lru_w_x[l], lru_b_x[l], lru_lambda[l])
        yb = rwkv7_mixer(pb.reshape(bsz, t, -1), seq, rwkv_mu[l], rwkv_w_up[l], rwkv_w0[l], rwkv_a_up[l],
                         rwkv_a0[l], rwkv_g_up[l], rwkv_k_k[l], rwkv_k_a[l], rwkv_r_k[l],
                         rwkv_ln_g[l], rwkv_ln_b[l])
        yc = mlstm_mixer(pc.reshape(bsz, t, -1), seq, mlstm_i_b[l], mlstm_f_b[l], mlstm_norm_g[l])
        yd = gdn_mixer(pd.reshape(bsz, t, -1), seq, gdn_conv_w[l], gdn_a_log[l], gdn_dt_bias[l], gdn_norm_g[l])
        x2 = out_proj([ya, yb, yc, yd], w_out[l].astype(BF16), x2, gt1, tiles_per_seq)
        if l % 2 == 0:
            x2 = dense_ffn(x2, norm_ffn_g[l], sc2, sh2, gt2, ffn_w_gate[l // 2].astype(BF16),
                           ffn_w_up[l // 2].astype(BF16), ffn_w_down[l // 2].astype(BF16), tiles_per_seq)
        else:
            h2, lg = norm_router(x2, norm_ffn_g[l], sc2, sh2, _pad_cols(moe_router[l // 2], LANE), tiles_per_seq)
            y = moe_ffn(h2, lg[:, :N_EXPERTS], moe_w_gate[l // 2].astype(BF16), moe_w_up[l // 2].astype(BF16),
                        moe_w_down[l // 2].astype(BF16))
            gt_rows = jnp.concatenate([jnp.broadcast_to(gt2.reshape(bsz, 2, 1, d)[:, 0], (bsz, seq, d)),
                                       jnp.broadcast_to(gt2.reshape(bsz, 2, 1, d)[:, 1], (bsz, n_ctx, d))], axis=1)
            x2 = x2 + gt_rows.reshape(n, d) * y
    return final_norm(x2.reshape(bsz, t, d), final_norm_g, seq)
```

```python
import functools
import math

import jax
import jax.numpy as jnp
from jax import lax
from jax.experimental import pallas as pl
from jax.experimental.pallas import tpu as pltpu

D_MODEL = 1024
DEPTH = 4
GRID_W = 64
N_MIXERS = 4
GROUP_W = D_MODEL // N_MIXERS
HEAD_DIM = 64
N_HEADS = GROUP_W // HEAD_DIM
CONV_W = 4
LRU_C = 8.0
RWKV_LORA_W = 32
RWKV_LORA_A = 32
RWKV_LORA_G = 64
RWKV_GN_EPS = 64e-5
CHUNK = 64
D_FF = 2816
N_EXPERTS = 8
TOP_K = 2
D_FF_EXPERT = 1408
MOE_BLOCK = 256
EPS = 1e-6

A_COLS = 2 * GROUP_W
B_COLS = 3 * GROUP_W + 2 * RWKV_LORA_W + 2 * RWKV_LORA_A + RWKV_LORA_G
C_COLS = 4 * GROUP_W + 4 * N_HEADS
D_COLS = 4 * GROUP_W + 4 * N_HEADS
OFF_B = A_COLS
OFF_C = OFF_B + B_COLS
OFF_D = OFF_C + C_COLS
IN_COLS = OFF_D + D_COLS

LANE = 128
SUBLANE = 8
VMEM_LIMIT = 56 * 1024 * 1024
ROW_TILE = 256
SCAN_SAMPLES = 2
F32 = jnp.float32
BF16 = jnp.bfloat16
HI = lax.Precision.HIGHEST


def _round_up(n, m):
    return (n + m - 1) // m * m


def _pad_cols(w, n):
    return jnp.pad(w, ((0, 0), (0, n - w.shape[1])))


def _const_spec(shape):
    nd = len(shape)
    return pl.BlockSpec(shape, lambda *_: (0,) * nd, pipeline_mode=pl.Buffered(1))


def _params(*sem):
    return pltpu.CompilerParams(dimension_semantics=sem, vmem_limit_bytes=VMEM_LIMIT)


def _mod_spec(d, tiles_per_seq):
    def index(i):
        return (2 * (i // tiles_per_seq) + (i % tiles_per_seq) // (tiles_per_seq - 1), 0, 0)
    return pl.BlockSpec((1, 1, d), index)


def _norm_mod(x, g, sc, sh):
    ms = jnp.mean(x * x, axis=-1, keepdims=True)
    return (x * lax.rsqrt(ms + EPS) * g) * (1.0 + sc) + sh


def _dot_dims(a, b, dims, prec):
    if prec is None:
        a, b, prec = a.astype(BF16), b.astype(BF16), None
    return lax.dot_general(a, b, (dims, ((), ())), preferred_element_type=F32, precision=prec)


def _dot(a, b, prec=None):
    return _dot_dims(a, b, ((1,), (0,)), prec)


def _dot_nt(a, b, prec=None):
    return _dot_dims(a, b, ((1,), (1,)), prec)


def _dot_tn(a, b, prec=None):
    return _dot_dims(a, b, ((0,), (0,)), prec)


def _norm_proj_kernel(x_ref, g_ref, sc_ref, sh_ref, *refs, n_out):
    h = _norm_mod(x_ref[...], g_ref[...], sc_ref[0], sh_ref[0]).astype(BF16)
    for w_ref, o_ref in zip(refs[:n_out], refs[n_out:]):
        o_ref[...] = _dot(h, w_ref[...])


def norm_proj(x2d, g, sc, sh, ws, tiles_per_seq):
    n, d = x2d.shape
    tm = ROW_TILE
    mod_spec = _mod_spec(d, tiles_per_seq)
    return pl.pallas_call(
        functools.partial(_norm_proj_kernel, n_out=len(ws)),
        grid=(n // tm,),
        in_specs=[pl.BlockSpec((tm, d), lambda i: (i, 0)), _const_spec((1, d)), mod_spec, mod_spec]
        + [_const_spec(w.shape) for w in ws],
        out_specs=[pl.BlockSpec((tm, w.shape[1]), lambda i: (i, 0)) for w in ws],
        out_shape=[jax.ShapeDtypeStruct((n, w.shape[1]), F32) for w in ws],
        compiler_params=_params("arbitrary"),
        name="norm_proj",
    )(x2d, g.reshape(1, d), sc, sh, *ws)


def _out_proj_kernel(ya_ref, yb_ref, yc_ref, yd_ref, w_ref, x_ref, gt_ref, o_ref):
    y = jnp.concatenate([ya_ref[...], yb_ref[...], yc_ref[...], yd_ref[...]], axis=-1).astype(BF16)
    o_ref[...] = x_ref[...] + gt_ref[0] * _dot(y, w_ref[...])


def out_proj(ys, w, x2d, gt, tiles_per_seq):
    n, d = x2d.shape
    tm = ROW_TILE
    y_spec = pl.BlockSpec((tm, GROUP_W), lambda i: (i, 0))
    return pl.pallas_call(
        _out_proj_kernel,
        grid=(n // tm,),
        in_specs=[y_spec] * 4 + [_const_spec(w.shape), pl.BlockSpec((tm, d), lambda i: (i, 0)),
                                 _mod_spec(d, tiles_per_seq)],
        out_specs=pl.BlockSpec((tm, d), lambda i: (i, 0)),
        out_shape=jax.ShapeDtypeStruct((n, d), F32),
        compiler_params=_params("arbitrary"),
        name="out_proj",
    )(*ys, w, x2d, gt)


def _swiglu_chunks(hb, wg_ref, wu_ref, wd_ref, n_chunks, fc):
    acc = None
    for c in range(n_chunks):
        a = _dot(hb, wg_ref[:, c * fc:(c + 1) * fc])
        u = _dot(hb, wu_ref[:, c * fc:(c + 1) * fc])
        z = (a * jax.nn.sigmoid(a) * u).astype(BF16)
        part = _dot(z, wd_ref[c * fc:(c + 1) * fc, :])
        acc = part if acc is None else acc + part
    return acc


def _dense_ffn_kernel(x_ref, g_ref, sc_ref, sh_ref, gt_ref, wg_ref, wu_ref, wd_ref, o_ref, *, n_chunks, fc):
    x = x_ref[...]
    hb = _norm_mod(x, g_ref[...], sc_ref[0], sh_ref[0]).astype(BF16)
    o_ref[...] = x + gt_ref[0] * _swiglu_chunks(hb, wg_ref, wu_ref, wd_ref, n_chunks, fc)


def dense_ffn(x2d, g, sc, sh, gt, wg, wu, wd, tiles_per_seq):
    n, d = x2d.shape
    tm = ROW_TILE
    fc = D_FF // 2
    mod_spec = _mod_spec(d, tiles_per_seq)
    return pl.pallas_call(
        functools.partial(_dense_ffn_kernel, n_chunks=D_FF // fc, fc=fc),
        grid=(n // tm,),
        in_specs=[pl.BlockSpec((tm, d), lambda i: (i, 0)), _const_spec((1, d)), mod_spec, mod_spec, mod_spec,
                  _const_spec(wg.shape), _const_spec(wu.shape), _const_spec(wd.shape)],
        out_specs=pl.BlockSpec((tm, d), lambda i: (i, 0)),
        out_shape=jax.ShapeDtypeStruct((n, d), F32),
        compiler_params=_params("arbitrary"),
        name="dense_ffn",
    )(x2d, g.reshape(1, d), sc, sh, gt, wg, wu, wd)


def _router_kernel(x_ref, g_ref, sc_ref, sh_ref, r_ref, h_ref, l_ref):
    h = _norm_mod(x_ref[...], g_ref[...], sc_ref[0], sh_ref[0])
    h_ref[...] = h.astype(BF16)
    l_ref[...] = _dot(h, r_ref[...], HI)


def norm_router(x2d, g, sc, sh, router_pad, tiles_per_seq):
    n, d = x2d.shape
    tm = ROW_TILE
    mod_spec = _mod_spec(d, tiles_per_seq)
    return pl.pallas_call(
        _router_kernel,
        grid=(n // tm,),
        in_specs=[pl.BlockSpec((tm, d), lambda i: (i, 0)), _const_spec((1, d)), mod_spec, mod_spec,
                  _const_spec(router_pad.shape)],
        out_specs=[pl.BlockSpec((tm, d), lambda i: (i, 0)), pl.BlockSpec((tm, LANE), lambda i: (i, 0))],
        out_shape=[jax.ShapeDtypeStruct((n, d), BF16), jax.ShapeDtypeStruct((n, LANE), F32)],
        compiler_params=_params("arbitrary"),
        name="norm_router",
    )(x2d, g.reshape(1, d), sc, sh, router_pad)


def _expert_kernel(be_ref, nb_ref, xb_ref, wg_ref, wu_ref, wd_ref, o_ref):
    i = pl.program_id(0)

    @pl.when(i < nb_ref[0])
    def _():
        o_ref[...] = _swiglu_chunks(xb_ref[...], wg_ref.at[0], wu_ref.at[0], wd_ref.at[0], 1, D_FF_EXPERT)

    @pl.when(i >= nb_ref[0])
    def _():
        o_ref[...] = jnp.zeros_like(o_ref)


def expert_ffn(xb, block_e, n_used, wg, wu, wd, tm):
    npad, d = xb.shape
    f = wg.shape[2]
    grid_spec = pltpu.PrefetchScalarGridSpec(
        num_scalar_prefetch=2,
        grid=(npad // tm,),
        in_specs=[pl.BlockSpec((tm, d), lambda i, be, nb: (i, 0)),
                  pl.BlockSpec((1, d, f), lambda i, be, nb: (be[i], 0, 0)),
                  pl.BlockSpec((1, d, f), lambda i, be, nb: (be[i], 0, 0)),
                  pl.BlockSpec((1, f, d), lambda i, be, nb: (be[i], 0, 0))],
        out_specs=pl.BlockSpec((tm, d), lambda i, be, nb: (i, 0)),
    )
    return pl.pallas_call(
        _expert_kernel,
        grid_spec=grid_spec,
        out_shape=jax.ShapeDtypeStruct((npad, d), F32),
        compiler_params=_params("arbitrary"),
        name="expert_ffn",
    )(block_e, n_used, xb, wg, wu, wd)


def moe_ffn(h_bf, logits, wg, wu, wd):
    t, d = h_bf.shape
    top_logits, top_idx = lax.top_k(logits, TOP_K)
    gate = jax.nn.softmax(top_logits, axis=-1)
    tk = t * TOP_K
    flat_e = top_idx.reshape(tk)
    order = jnp.argsort(flat_e)
    se = flat_e[order]
    stok = (order // TOP_K).astype(jnp.int32)
    counts = jnp.zeros((N_EXPERTS,), jnp.int32).at[flat_e].add(1)
    start = jnp.cumsum(counts) - counts
    padded = (counts + MOE_BLOCK - 1) // MOE_BLOCK * MOE_BLOCK
    pad_start = jnp.cumsum(padded) - padded
    pad_end = pad_start + padded
    pos = pad_start[se] + jnp.arange(tk, dtype=jnp.int32) - start[se]
    n_blocks = -(-tk // MOE_BLOCK) + N_EXPERTS
    buf_tok = jnp.zeros((n_blocks * MOE_BLOCK,), jnp.int32).at[pos].set(stok)
    slot_of = jnp.zeros((tk,), jnp.int32).at[order].set(pos).reshape(t, TOP_K)
    block_e = jnp.clip(jnp.searchsorted(pad_end, jnp.arange(n_blocks, dtype=jnp.int32) * MOE_BLOCK,
                                        side='right'), 0, N_EXPERTS - 1).astype(jnp.int32)
    n_used = (pad_end[-1] // MOE_BLOCK).astype(jnp.int32).reshape(1)
    yb = expert_ffn(h_bf[buf_tok], block_e, n_used, wg, wu, wd, MOE_BLOCK)
    return gate[:, 0:1] * yb[slot_of[:, 0]] + gate[:, 1:2] * yb[slot_of[:, 1]]


def _final_norm_kernel(x_ref, g_ref, o_ref):
    x = x_ref[...]
    ms = jnp.mean(x * x, axis=-1, keepdims=True)
    o_ref[...] = x * lax.rsqrt(ms + EPS) * g_ref[...]


def final_norm(x3d, g, n_lat):
    bsz, t, d = x3d.shape
    tm = ROW_TILE
    return pl.pallas_call(
        _final_norm_kernel,
        grid=(bsz, n_lat // tm),
        in_specs=[pl.BlockSpec((None, tm, d), lambda b, i: (b, i, 0)), _const_spec((1, d))],
        out_specs=pl.BlockSpec((None, tm, d), lambda b, i: (b, i, 0)),
        out_shape=jax.ShapeDtypeStruct((bsz, n_lat, d), F32),
        compiler_params=_params("arbitrary", "arbitrary"),
        name="final_norm",
    )(x3d, g.reshape(1, d))


def _same_head(shape):
    r = lax.broadcasted_iota(jnp.int32, shape, 0) // HEAD_DIM
    c = lax.broadcasted_iota(jnp.int32, shape, 1) // HEAD_DIM
    return r == c


def _to_block_diag(x, mask):
    return jnp.where(mask, jnp.concatenate([x] * N_HEADS, axis=0), 0.0)


def _token_masks(reverse):
    t = lax.broadcasted_iota(jnp.int32, (CHUNK, GROUP_W), 0)
    s = lax.broadcasted_iota(jnp.int32, (CHUNK, GROUP_W), 1) % HEAD_DIM
    if reverse:
        return s > t, s >= t, s == t
    return s < t, s <= t, s == t


def _cum_matrix(reverse):
    t = lax.broadcasted_iota(jnp.int32, (CHUNK, CHUNK), 0)
    s = lax.broadcasted_iota(jnp.int32, (CHUNK, CHUNK), 1)
    return jnp.where((s >= t) if reverse else (s <= t), 1.0, 0.0).astype(F32)


def _unit_tri_inverse(ls, bd_mask):
    t = lax.broadcasted_iota(jnp.int32, (CHUNK, GROUP_W), 0)
    s = lax.broadcasted_iota(jnp.int32, (CHUNK, GROUP_W), 1) % HEAD_DIM
    blk16 = (t // 16) == (s // 16)
    blk32 = (t // 32) == (s // 32)
    eye = jnp.where(t == s, 1.0, 0.0)
    ps = [jnp.where(blk16, -l, 0.0) for l in ls]
    xs = [eye + p for p in ps]
    for _ in range(3):
        ps = [_dot(p, _to_block_diag(p, bd_mask)) for p in ps]
        xs = [x + _dot(x, _to_block_diag(p, bd_mask)) for x, p in zip(xs, ps)]
    for off_diag in (jnp.logical_and(blk32, jnp.logical_not(blk16)), jnp.logical_not(blk32)):
        xe = [_dot(x, _to_block_diag(jnp.where(off_diag, l, 0.0), bd_mask)) for x, l in zip(xs, ls)]
        xs = [x - _dot(y, _to_block_diag(x, bd_mask)) for x, y in zip(xs, xe)]
    return xs


def _head_lane_expander(col0):
    r = lax.broadcasted_iota(jnp.int32, (LANE, GROUP_W), 0)
    c = lax.broadcasted_iota(jnp.int32, (LANE, GROUP_W), 1) // HEAD_DIM
    return jnp.where(r == col0 + c, 1.0, 0.0).astype(F32)


def _fwd_chunk(j, n_lat_chunks, n_chunks):
    n_ctx_chunks = n_chunks - n_lat_chunks
    return jnp.where(j < n_ctx_chunks, n_lat_chunks + j, j - n_ctx_chunks)


def _bwd_chunk(j, n_lat_chunks, n_chunks):
    return n_chunks - 1 - j


def _seq_edges(c, n_lat_chunks, n_chunks):
    first = jnp.logical_or(c == 0, c == n_lat_chunks)
    last = jnp.logical_or(c == n_lat_chunks - 1, c == n_chunks - 1)
    return first, last


def _shift_rows(x, prev8, next8, back):
    row = lax.broadcasted_iota(jnp.int32, (CHUNK, 1), 0)
    if back == -1:
        return jnp.where(row == CHUNK - 1, next8[0:1, :], pltpu.roll(x, CHUNK - 1, 0))
    out = pltpu.roll(x, back, 0)
    for i in range(back):
        out = jnp.where(row == i, prev8[SUBLANE - back + i:SUBLANE - back + i + 1, :], out)
    return out


def _natural_scan_specs(nb, width, out_width, n_lat_chunks, n_chunks, last8):
    r8 = CHUNK // SUBLANE
    fwd = functools.partial(_fwd_chunk, n_lat_chunks=n_lat_chunks, n_chunks=n_chunks)
    bwd = functools.partial(_bwd_chunk, n_lat_chunks=n_lat_chunks, n_chunks=n_chunks)
    chunk = lambda f, w: pl.BlockSpec((nb, CHUNK, w), lambda b, j: (b, f(j), 0))
    prev = lambda f: pl.BlockSpec((nb, SUBLANE, width), lambda b, j: (b, jnp.maximum(f(j) * r8 - 1, 0), 0))
    nxt = lambda f: pl.BlockSpec((nb, SUBLANE, width), lambda b, j: (b, jnp.minimum((f(j) + 1) * r8, last8), 0))
    ins = [chunk(fwd, width), prev(fwd), nxt(fwd), chunk(bwd, width), prev(bwd), nxt(bwd)]
    return ins, chunk(fwd, out_width), chunk(bwd, out_width)


def _chunk_specs(nb, width, n_steps, first_row_block, column_view):
    if column_view:
        return (pl.BlockSpec((nb, CHUNK, width), lambda b, j: (b, 0, j)),
                pl.BlockSpec((nb, CHUNK, width), lambda b, j: (b, 0, n_steps - 1 - j)))
    return (pl.BlockSpec((nb, CHUNK, width), lambda b, j: (b, first_row_block + j, 0)),
            pl.BlockSpec((nb, CHUNK, width), lambda b, j: (b, first_row_block + n_steps - 1 - j, 0)))


def _halo_specs(nb, width, n_steps, first_row_block, column_view, last8):
    r8 = CHUNK // SUBLANE
    fwd = lambda j: j
    bwd = lambda j: n_steps - 1 - j
    if column_view:
        prev = lambda f: pl.BlockSpec((nb, SUBLANE, width), lambda b, j: (b, r8 - 1, jnp.maximum(f(j) - 1, 0)))
        nxt = lambda f: pl.BlockSpec((nb, SUBLANE, width), lambda b, j: (b, 0, jnp.minimum(f(j) + 1, n_steps - 1)))
    else:
        prev = lambda f: pl.BlockSpec((nb, SUBLANE, width),
                                      lambda b, j: (b, jnp.maximum((first_row_block + f(j)) * r8 - 1, 0), 0))
        nxt = lambda f: pl.BlockSpec((nb, SUBLANE, width),
                                     lambda b, j: (b, jnp.minimum((first_row_block + f(j) + 1) * r8, last8), 0))
    return prev(fwd), nxt(fwd), prev(bwd), nxt(bwd)


def _gated_head_norm_kernel(hf_ref, hb_ref, o_ref, g_ref, y_ref, *, silu_gate):
    h = hf_ref[...] + hb_ref[...]
    seg_mean = jnp.where(_same_head((GROUP_W, GROUP_W)), 1.0 / HEAD_DIM, 0.0).astype(F32)
    hn = h * lax.rsqrt(_dot(h * h, seg_mean, HI) + EPS) * g_ref[...]
    gate = o_ref[...]
    y_ref[...] = hn * (gate * jax.nn.sigmoid(gate) if silu_gate else jax.nn.sigmoid(gate))


def gated_head_norm(hf, hb, p2d, norm_g, silu_gate):
    n, G = hf.shape
    tm = ROW_TILE
    row_spec = pl.BlockSpec((tm, G), lambda i: (i, 0))
    return pl.pallas_call(
        functools.partial(_gated_head_norm_kernel, silu_gate=silu_gate),
        grid=(n // tm,),
        in_specs=[row_spec, row_spec, pl.BlockSpec((tm, G), lambda i: (i, 3)), _const_spec((1, G))],
        out_specs=row_spec,
        out_shape=jax.ShapeDtypeStruct((n, G), F32),
        compiler_params=_params("arbitrary"),
        name="gated_head_norm",
    )(hf, hb, p2d, norm_g.reshape(1, G))


def _rwkv_prepare(p, prev_row, next_row, prm, d, reverse):
    (mu_ref, wup_ref, w0_ref, aup_ref, a0_ref, gup_ref, kk_ref, ka_ref, rk_ref) = prm
    G = GROUP_W
    row = lax.broadcasted_iota(jnp.int32, (CHUNK, 1), 0)
    prev = jnp.where(row == 0, prev_row, pltpu.roll(p, 1, 0))
    nxt = jnp.where(row == CHUNK - 1, next_row, pltpu.roll(p, CHUNK - 1, 0))
    ps = p + mu_ref[0:1, :] * (prev - p) + mu_ref[1:2, :] * (nxt - p)
    r, k, v = ps[:, :G], ps[:, G:2 * G], ps[:, 2 * G:3 * G]
    lora_in = ps[:, 3 * G:3 * G + LANE]
    gate_in = ps[:, 3 * G + LANE:3 * G + 2 * LANE]
    bd_mask = _same_head((G, G))
    head_ones = jnp.where(bd_mask, 1.0, 0.0).astype(F32)
    log_w = -math.exp(-0.5) * jax.nn.sigmoid(_dot(jnp.tanh(lora_in), wup_ref[d], HI) + w0_ref[d:d + 1, :])
    a = jax.nn.sigmoid(_dot(lora_in, aup_ref[d], HI) + a0_ref[d:d + 1, :])
    kr = k * kk_ref[...]
    kk = kr * lax.rsqrt(_dot(kr * kr, head_ones, HI) + EPS)
    k_mod = k * (1.0 + (a - 1.0) * ka_ref[...])
    bonus = _dot(r * k_mod * rk_ref[...], head_ones, HI) * v
    lg = _dot(_cum_matrix(reverse), log_w, HI)
    e_neg = jnp.exp(-lg)
    at = kk * jnp.exp(lg - log_w)
    bt = -(kk * a) * e_neg
    kt = k_mod * e_neg
    rt = r * jnp.exp(lg)
    g_last = jnp.exp(lg[0:1, :] if reverse else lg[CHUNK - 1:CHUNK, :])
    strict, incl, _ = _token_masks(reverse)
    gate = None if reverse else _dot(jax.nn.sigmoid(gate_in), gup_ref[...], HI)
    return dict(at=at, rt=rt, bt=bt, kt=kt, v=v, g_last=g_last, bonus=bonus, gate=gate, strict=strict, incl=incl)


def _rwkv_kernel(pf_ref, pfp_ref, pfn_ref, pb_ref, pbp_ref, pbn_ref,
                 mu_ref, wup_ref, w0_ref, aup_ref, a0_ref, gup_ref, kk_ref, ka_ref, rk_ref,
                 yf_ref, bf_ref, g_ref, yb_ref, bb_ref, s_ref, *, n_lat_chunks, n_chunks):
    j = pl.program_id(1)

    @pl.when(j == 0)
    def _():
        s_ref[...] = jnp.zeros_like(s_ref)

    prm = (mu_ref, wup_ref, w0_ref, aup_ref, a0_ref, gup_ref, kk_ref, ka_ref, rk_ref)
    nb = pf_ref.shape[0]
    bd_mask = _same_head((GROUP_W, GROUP_W))
    idx = [(i, d) for i in range(nb) for d in range(2)]
    cs = []
    for i, d in idx:
        chunk = (_bwd_chunk if d else _fwd_chunk)(j, n_lat_chunks, n_chunks)
        first, last = _seq_edges(chunk, n_lat_chunks, n_chunks)
        p_ref, pp_ref, pn_ref = (pb_ref, pbp_ref, pbn_ref) if d else (pf_ref, pfp_ref, pfn_ref)
        prev_row = jnp.where(first, 0.0, pp_ref[i, SUBLANE - 1:SUBLANE, :])
        next_row = jnp.where(last, 0.0, pn_ref[i, 0:1, :])
        cs.append(_rwkv_prepare(p_ref[i], prev_row, next_row, prm, d, bool(d)))
    ss = [s_ref[i, d] for i, d in idx]
    lhs = [jnp.concatenate([c['at'], c['rt']], axis=0) for c in cs]
    m_b = [_dot_nt(l, _to_block_diag(c['bt'], bd_mask)) for l, c in zip(lhs, cs)]
    m_k = [_dot_nt(l, _to_block_diag(c['kt'], bd_mask)) for l, c in zip(lhs, cs)]
    a_ak = [jnp.where(c['strict'], m[:CHUNK], 0.0) for c, m in zip(cs, m_k)]
    a_rb = [jnp.where(c['incl'], m[CHUNK:], 0.0) for c, m in zip(cs, m_b)]
    a_rk = [jnp.where(c['incl'], m[CHUNK:], 0.0) for c, m in zip(cs, m_k)]
    tinv = _unit_tri_inverse([jnp.where(c['strict'], -m[:CHUNK], 0.0) for c, m in zip(cs, m_b)], bd_mask)
    v_bd = [_to_block_diag(c['v'], bd_mask) for c in cs]
    rhs = [_dot_nt(c['at'], s) + _dot(a, vb) for c, s, a, vb in zip(cs, ss, a_ak, v_bd)]
    us = [_dot(t, _to_block_diag(r, bd_mask)) for t, r in zip(tinv, rhs)]
    ys = [_dot_nt(c['rt'], s) + _dot(a, vb) for c, s, a, vb in zip(cs, ss, a_rk, v_bd)]
    ys = [y + _dot(a, _to_block_diag(u, bd_mask)) for y, a, u in zip(ys, a_rb, us)]
    upds = [_dot_tn(jnp.concatenate([u, c['v']], axis=0), jnp.concatenate([c['bt'], c['kt']], axis=0))
            for u, c in zip(us, cs)]
    for n_, (i, d) in enumerate(idx):
        (yb_ref if d else yf_ref)[i] = ys[n_]
        (bb_ref if d else bf_ref)[i] = cs[n_]['bonus']
        if not d:
            g_ref[i] = cs[n_]['gate']
        s_ref[i, d] = cs[n_]['g_last'] * (ss[n_] + jnp.where(bd_mask, upds[n_], 0.0))


def _rwkv_finish_kernel(yf_ref, yb_ref, bf_ref, bb_ref, g_ref, lng_ref, lnb_ref, o_ref):
    y = yf_ref[...] + yb_ref[...]
    seg_mean = jnp.where(_same_head((GROUP_W, GROUP_W)), 1.0 / HEAD_DIM, 0.0).astype(F32)
    mean = _dot(y, seg_mean, HI)
    var = _dot(jnp.square(y - mean), seg_mean, HI)
    yn = (y - mean) * lax.rsqrt(var + RWKV_GN_EPS) * lng_ref[...] + lnb_ref[...]
    o_ref[...] = (yn + bf_ref[...] + bb_ref[...]) * g_ref[...]


def rwkv7_mixer(p3d, n_lat, mu, w_up, w0, a_up, a0, g_up, k_k, k_a, r_k, ln_g, ln_b):
    bsz, t, width = p3d.shape
    G = GROUP_W
    n_chunks = t // CHUNK
    n_lat_chunks = n_lat // CHUNK
    mu_pad = _pad_cols(mu, width)
    wup = jnp.zeros((2, LANE, G), F32)
    aup = jnp.zeros((2, LANE, G), F32)
    for d in range(2):
        wup = wup.at[d, d * RWKV_LORA_W:(d + 1) * RWKV_LORA_W].set(w_up[d])
        lo = 2 * RWKV_LORA_W + d * RWKV_LORA_A
        aup = aup.at[d, lo:lo + RWKV_LORA_A].set(a_up[d])
    gup = jnp.zeros((LANE, G), F32).at[:RWKV_LORA_G].set(g_up)
    consts = [mu_pad, wup, w0, aup, a0, gup, k_k.reshape(1, G), k_a.reshape(1, G), r_k.reshape(1, G)]
    nb = SCAN_SAMPLES
    ins, f_out, b_out = _natural_scan_specs(nb, width, G, n_lat_chunks, n_chunks, t // SUBLANE - 1)
    out_sds = jax.ShapeDtypeStruct((bsz, t, G), F32)
    yf, bf, gate, yb, bb = pl.pallas_call(
        functools.partial(_rwkv_kernel, n_lat_chunks=n_lat_chunks, n_chunks=n_chunks),
        grid=(bsz // nb, n_chunks),
        in_specs=ins + [_const_spec(c.shape) for c in consts],
        out_specs=[f_out] * 3 + [b_out] * 2,
        out_shape=[out_sds] * 5,
        scratch_shapes=[pltpu.VMEM((nb, 2, G, G), F32)],
        compiler_params=_params("arbitrary", "arbitrary"),
        name="rwkv_scan",
    )(p3d, p3d, p3d, p3d, p3d, p3d, *consts)
    n = bsz * t
    tm = ROW_TILE
    row_spec = pl.BlockSpec((tm, G), lambda i: (i, 0))
    return pl.pallas_call(
        _rwkv_finish_kernel,
        grid=(n // tm,),
        in_specs=[row_spec] * 5 + [_const_spec((1, G))] * 2,
        out_specs=row_spec,
        out_shape=jax.ShapeDtypeStruct((n, G), F32),
        compiler_params=_params("arbitrary"),
        name="rwkv_finish",
    )(*[a.reshape(n, G) for a in (yf, yb, bf, bb, gate)], ln_g.reshape(1, G), ln_b.reshape(1, G))


def _neg_expm1(x):
    series = -x * (1.0 + x * (1 / 2 + x * (1 / 6 + x * (1 / 24 + x * (1 / 120 + x * (1 / 720 + x / 5040))))))
    return jnp.where(x > -0.05, series, 1.0 - jnp.exp(x))


def _lru_chunk(p, prev8, next8, cw_ref, cb_ref, w_ref, bias_ref, lam_ref, h_ref, d, reverse):
    G = GROUP_W
    x = p[:, :G]
    u = (cw_ref[0:1, :] * _shift_rows(x, prev8, next8, 2) + cw_ref[1:2, :] * _shift_rows(x, prev8, next8, 1)
         + cw_ref[2:3, :] * x + cw_ref[3:4, :] * _shift_rows(x, prev8, next8, -1)) + cb_ref[...]
    pre = _dot(u, w_ref[d]) + bias_ref[d:d + 1, :]
    r = jax.nn.sigmoid(pre[:, :G])
    i = jax.nn.sigmoid(pre[:, G:])
    log_a = -LRU_C * r * jax.nn.softplus(-lam_ref[d:d + 1, :])
    a = jnp.exp(log_a)
    b = jnp.sqrt(_neg_expm1(2.0 * log_a)) * (i * u)
    row = lax.broadcasted_iota(jnp.int32, (CHUNK, 1), 0)
    k = 1
    while k < CHUNK:
        shift = CHUNK - k if reverse else k
        valid = (row < CHUNK - k) if reverse else (row >= k)
        b = jnp.where(valid, a * pltpu.roll(b, shift, 0) + b, b)
        a = jnp.where(valid, a * pltpu.roll(a, shift, 0), a)
        k *= 2
    h = b + a * h_ref[0:1, :]
    h_ref[0:1, :] = h[0:1, :] if reverse else h[CHUNK - 1:CHUNK, :]
    return h


def _lru_kernel(pf_ref, pfp_ref, pfn_ref, pb_ref, pbp_ref, pbn_ref, cw_ref, cb_ref, w_ref, bias_ref, lam_ref,
                hf_ref, hb_ref, st_ref, *, n_lat_chunks, n_chunks):
    j = pl.program_id(1)

    @pl.when(j == 0)
    def _():
        st_ref[...] = jnp.zeros_like(st_ref)

    G = GROUP_W
    for i in range(pf_ref.shape[0]):
        first, last = _seq_edges(_fwd_chunk(j, n_lat_chunks, n_chunks), n_lat_chunks, n_chunks)
        prev8 = jnp.where(first, 0.0, pfp_ref[i, :, :G])
        next8 = jnp.where(last, 0.0, pfn_ref[i, :, :G])
        hf_ref[i] = _lru_chunk(pf_ref[i], prev8, next8, cw_ref, cb_ref, w_ref, bias_ref, lam_ref,
                               st_ref.at[i, 0], 0, False)
        first, last = _seq_edges(_bwd_chunk(j, n_lat_chunks, n_chunks), n_lat_chunks, n_chunks)
        prev8 = jnp.where(first, 0.0, pbp_ref[i, :, :G])
        next8 = jnp.where(last, 0.0, pbn_ref[i, :, :G])
        hb_ref[i] = _lru_chunk(pb_ref[i], prev8, next8, cw_ref, cb_ref, w_ref, bias_ref, lam_ref,
                               st_ref.at[i, 1], 1, True)


def _lru_finish_kernel(hf_ref, hb_ref, gate_ref, y_ref):
    y_ref[...] = (hf_ref[...] + hb_ref[...]) * jax.nn.gelu(gate_ref[...])


def rglru_mixer(p3d, n_lat, conv_w, conv_b, w_a, b_a, w_x, b_x, lam):
    bsz, t, width = p3d.shape
    G = GROUP_W
    n_chunks = t // CHUNK
    n_lat_chunks = n_lat // CHUNK
    w = jnp.zeros((2, G, 2 * G), F32)
    for h in range(N_HEADS):
        sl = slice(h * HEAD_DIM, (h + 1) * HEAD_DIM)
        w = w.at[:, sl, sl].set(w_a[:, h])
        w = w.at[:, sl, G + h * HEAD_DIM:G + (h + 1) * HEAD_DIM].set(w_x[:, h])
    consts = [jnp.pad(conv_w, ((0, SUBLANE - CONV_W), (0, 0))), conv_b.reshape(1, G), w,
              jnp.concatenate([b_a, b_x], axis=1), lam]
    nb = SCAN_SAMPLES
    ins, f_out, b_out = _natural_scan_specs(nb, width, G, n_lat_chunks, n_chunks, t // SUBLANE - 1)
    h_sds = jax.ShapeDtypeStruct((bsz, t, G), F32)
    hf, hb = pl.pallas_call(
        functools.partial(_lru_kernel, n_lat_chunks=n_lat_chunks, n_chunks=n_chunks),
        grid=(bsz // nb, n_chunks),
        in_specs=ins + [_const_spec(c.shape) for c in consts],
        out_specs=[f_out, b_out],
        out_shape=[h_sds, h_sds],
        scratch_shapes=[pltpu.VMEM((nb, 2, SUBLANE, G), F32)],
        compiler_params=_params("arbitrary", "arbitrary"),
        name="lru_scan",
    )(p3d, p3d, p3d, p3d, p3d, p3d, *consts)
    n = bsz * t
    tm = ROW_TILE
    row_spec = pl.BlockSpec((tm, G), lambda i: (i, 0))
    return pl.pallas_call(
        _lru_finish_kernel,
        grid=(n // tm,),
        in_specs=[row_spec, row_spec, pl.BlockSpec((tm, G), lambda i: (i, 1))],
        out_specs=row_spec,
        out_shape=jax.ShapeDtypeStruct((n, G), F32),
        compiler_params=_params("arbitrary"),
        name="lru_finish",
    )(hf.reshape(n, G), hb.reshape(n, G), p3d.reshape(n, width))


def _gdn_prepare(p, prev8, next8, cw_ref, al_ref, dt_ref, d, reverse):
    G = GROUP_W
    x = p[:, :3 * G]
    conv = (cw_ref[0:1, :] * _shift_rows(x, prev8, next8, 2) + cw_ref[1:2, :] * _shift_rows(x, prev8, next8, 1)
            + cw_ref[2:3, :] * x + cw_ref[3:4, :] * _shift_rows(x, prev8, next8, -1))
    qkv = conv * jax.nn.sigmoid(conv)
    gates = p[:, 4 * G:4 * G + LANE]
    bd_mask = _same_head((G, G))
    head_ones = jnp.where(bd_mask, 1.0, 0.0).astype(F32)
    q, k, v = qkv[:, :G], qkv[:, G:2 * G], qkv[:, 2 * G:]
    q = q * lax.rsqrt(_dot(q * q, head_ones, HI) + EPS) * (HEAD_DIM ** -0.5)
    k = k * lax.rsqrt(_dot(k * k, head_ones, HI) + EPS)
    la = -al_ref[d:d + 1, :] * jax.nn.softplus(_dot(gates, _head_lane_expander(d * N_HEADS), HI) + dt_ref[d:d + 1, :])
    beta = jax.nn.sigmoid(_dot(gates, _head_lane_expander((2 + d) * N_HEADS), HI))
    gam = _dot(_cum_matrix(reverse), la, HI)
    strict, incl, diag = _token_masks(reverse)
    gam_row = jnp.sum(jnp.where(diag, gam, 0.0), axis=0, keepdims=True)
    decay = jnp.exp(jnp.where(incl, gam - gam_row, -jnp.inf))
    k_beta = k * beta
    k_bd = _to_block_diag(k, bd_mask)
    lower = jnp.where(strict, _dot_nt(k_beta, k_bd) * decay, 0.0)
    e_gam = jnp.exp(gam)
    gam_last = gam[0:1, :] if reverse else gam[CHUNK - 1:CHUNK, :]
    return dict(lower=lower, vb=v * beta, kbe=k_beta * e_gam, qk=_dot_nt(q, k_bd) * decay, q_dec=q * e_gam,
                k_dec=k * jnp.exp(gam_last - gam), g_last=jnp.exp(gam_last))


def _gdn_kernel(pf_ref, pfp_ref, pfn_ref, pb_ref, pbp_ref, pbn_ref, cw_ref, al_ref, dt_ref, s0_ref,
                of_ref, ob_ref, s_ref, *, n_steps):
    j = pl.program_id(1)

    @pl.when(j == 0)
    def _():
        s_ref[...] = s0_ref[...]

    width = 3 * GROUP_W
    bd_mask = _same_head((GROUP_W, GROUP_W))
    cs, ss = [], []
    for i in range(pf_ref.shape[0]):
        prev8 = jnp.where(j == 0, 0.0, pfp_ref[i, :, :width])
        next8 = jnp.where(j == n_steps - 1, 0.0, pfn_ref[i, :, :width])
        cs.append(_gdn_prepare(pf_ref[i], prev8, next8, cw_ref, al_ref, dt_ref, 0, False))
        prev8 = jnp.where(j == n_steps - 1, 0.0, pbp_ref[i, :, :width])
        next8 = jnp.where(j == 0, 0.0, pbn_ref[i, :, :width])
        cs.append(_gdn_prepare(pb_ref[i], prev8, next8, cw_ref, al_ref, dt_ref, 1, True))
        ss += [s_ref[i, 0], s_ref[i, 1]]
    tinv = _unit_tri_inverse([c['lower'] for c in cs], bd_mask)
    us = [_dot(t, _to_block_diag(c['vb'], bd_mask)) for t, c in zip(tinv, cs)]
    ws = [_dot(t, _to_block_diag(c['kbe'], bd_mask)) for t, c in zip(tinv, cs)]
    vn = [u - _dot(w, s) for u, w, s in zip(us, ws, ss)]
    outs = [_dot(c['q_dec'], s) + _dot(c['qk'], _to_block_diag(v, bd_mask)) for c, s, v in zip(cs, ss, vn)]
    sn = [c['g_last'] * s + jnp.where(bd_mask, _dot_tn(c['k_dec'], v), 0.0) for c, s, v in zip(cs, ss, vn)]
    for i in range(pf_ref.shape[0]):
        of_ref[i] = outs[2 * i]
        ob_ref[i] = outs[2 * i + 1]
        s_ref[i, 0] = sn[2 * i]
        s_ref[i, 1] = sn[2 * i + 1]


def gdn_scan(p, n_steps, first_row_block, column_view, last8, cw, al, dt, s0):
    bsz = p.shape[0]
    G = GROUP_W
    width = _round_up(D_COLS, LANE)
    nb = SCAN_SAMPLES
    f_in, b_in = _chunk_specs(nb, width, n_steps, first_row_block, column_view)
    fp, fn, bp, bn = _halo_specs(nb, width, n_steps, first_row_block, column_view, last8)
    f_out, b_out = _chunk_specs(nb, G, n_steps, 0, column_view)
    h_sds = jax.ShapeDtypeStruct((bsz, CHUNK, GRID_W * G) if column_view else (bsz, n_steps * CHUNK, G), F32)
    state_spec = pl.BlockSpec((nb, 2, G, G), lambda b, j: (b, 0, 0, 0))
    return pl.pallas_call(
        functools.partial(_gdn_kernel, n_steps=n_steps),
        grid=(bsz // nb, n_steps),
        in_specs=[f_in, fp, fn, b_in, bp, bn, _const_spec(cw.shape), _const_spec(al.shape), _const_spec(dt.shape),
                  state_spec],
        out_specs=[f_out, b_out, state_spec],
        out_shape=[h_sds, h_sds, jax.ShapeDtypeStruct(s0.shape, F32)],
        compiler_params=_params("arbitrary", "arbitrary"),
        name="gdn_scan",
    )(p, p, p, p, p, p, cw, al, dt, s0)


def gdn_mixer(p3d, n_lat, conv_w, a_log, dt_bias, norm_g):
    bsz, t, width = p3d.shape
    G = GROUP_W
    n_ctx = t - n_lat
    assert n_lat == CHUNK * GRID_W
    al = jnp.repeat(jnp.exp(a_log), HEAD_DIM, axis=1)
    dt = jnp.repeat(dt_bias, HEAD_DIM, axis=1)
    cw = jnp.pad(conv_w, ((0, SUBLANE - CONV_W), (0, 0)))
    s0 = jnp.zeros((bsz, 2, G, G), F32)
    last8 = t // SUBLANE - 1
    of_c, ob_c, s1 = gdn_scan(p3d, n_ctx // CHUNK, n_lat // CHUNK, False, last8, cw, al, dt, s0)
    of_l, ob_l, _ = gdn_scan(p3d.reshape(bsz, t // GRID_W, GRID_W * width), GRID_W, 0, True, last8, cw, al, dt, s1)
    hf = jnp.concatenate([of_l.reshape(bsz, n_lat, G), of_c], axis=1).reshape(bsz * t, G)
    hb = jnp.concatenate([ob_l.reshape(bsz, n_lat, G), ob_c], axis=1).reshape(bsz * t, G)
    return gated_head_norm(hf, hb, p3d.reshape(bsz * t, width), norm_g, True)


def _head_max(x):
    lane_head = lax.broadcasted_iota(jnp.int32, x.shape, 1) // HEAD_DIM
    out = jnp.zeros_like(x)
    for h in range(N_HEADS):
        mh = jnp.max(jnp.where(lane_head == h, x, -jnp.inf), axis=-1, keepdims=True)
        out = jnp.where(lane_head == h, mh, out)
    return out


def _mlstm_prepare(p, ib_ref, fb_ref, d, reverse):
    G = GROUP_W
    q, k, v = p[:, :G], p[:, G:2 * G] * (HEAD_DIM ** -0.5), p[:, 2 * G:3 * G]
    gates = p[:, 4 * G:4 * G + LANE]
    bd_mask = _same_head((G, G))
    li = _dot(gates, _head_lane_expander(d * N_HEADS), HI) + ib_ref[d:d + 1, :]
    lf = jax.nn.log_sigmoid(_dot(gates, _head_lane_expander((2 + d) * N_HEADS), HI) + fb_ref[d:d + 1, :])
    b = _dot(_cum_matrix(reverse), lf, HI)
    _, incl, diag = _token_masks(reverse)
    g = li - b
    g_row = jnp.sum(jnp.where(diag, g, 0.0), axis=0, keepdims=True)
    log_d = jnp.where(incl, b + g_row, -jnp.inf)
    b_last = b[0:1, :] if reverse else b[CHUNK - 1:CHUNK, :]
    log_end = b_last + g
    return dict(q=q, k=k, v=v, b=b, log_d=log_d, m_intra=_head_max(log_d), b_last=b_last, log_end=log_end,
                end_max=jnp.max(log_end, axis=0, keepdims=True), qk=_dot_nt(q, _to_block_diag(k, bd_mask)))


def _mlstm_kernel(pf_ref, pb_ref, ib_ref, fb_ref, c0_ref, nm0_ref, hf_ref, hb_ref, c_ref, nm_ref):
    @pl.when(pl.program_id(1) == 0)
    def _():
        c_ref[...] = c0_ref[...]
        nm_ref[...] = nm0_ref[...]

    nb = pf_ref.shape[0]
    bd_mask = _same_head((GROUP_W, GROUP_W))
    head_ones = jnp.where(bd_mask, 1.0, 0.0).astype(F32)
    idx = [(i, d) for i in range(nb) for d in range(2)]
    cs = [_mlstm_prepare((pb_ref if d else pf_ref)[i], ib_ref, fb_ref, d, bool(d)) for i, d in idx]
    mems = [c_ref[i, d] for i, d in idx]
    nrms = [nm_ref[i, d, 0:1, :] for i, d in idx]
    m_prevs = [nm_ref[i, d, 1:2, :] for i, d in idx]
    m_inters = [c['b'] + m for c, m in zip(cs, m_prevs)]
    m_ts = [jnp.maximum(mi, c['m_intra']) for mi, c in zip(m_inters, cs)]
    w_inters = [jnp.exp(mi - mt) for mi, mt in zip(m_inters, m_ts)]
    pms = [jnp.exp(c['log_d'] - mt) * c['qk'] for c, mt in zip(cs, m_ts)]
    inter = [_dot_nt(c['q'], mem) for c, mem in zip(cs, mems)]
    nums = [_dot(pm, _to_block_diag(c['v'], bd_mask)) + w * it for pm, c, w, it in zip(pms, cs, w_inters, inter)]
    dens = [_dot(pm, head_ones, HI) + w * _dot(c['q'] * nrm, head_ones, HI)
            for pm, w, c, nrm in zip(pms, w_inters, cs, nrms)]
    hs = [num / jnp.maximum(jnp.abs(den), jnp.exp(-mt)) for num, den, mt in zip(nums, dens, m_ts)]
    m_news = [jnp.maximum(c['b_last'] + m, c['end_max']) for c, m in zip(cs, m_prevs)]
    w_prevs = [jnp.exp(c['b_last'] + m - mn) for c, m, mn in zip(cs, m_prevs, m_news)]
    w_news = [jnp.exp(c['log_end'] - mn) for c, mn in zip(cs, m_news)]
    upds = [_dot_tn(wn * c['v'], c['k']) for wn, c in zip(w_news, cs)]
    for n_, (i, d) in enumerate(idx):
        (hb_ref if d else hf_ref)[i] = hs[n_]
        c_ref[i, d] = w_prevs[n_] * mems[n_] + jnp.where(bd_mask, upds[n_], 0.0)
        nm_ref[i, d, 0:1, :] = w_prevs[n_] * nrms[n_] + jnp.sum(w_news[n_] * cs[n_]['k'], axis=0, keepdims=True)
        nm_ref[i, d, 1:2, :] = m_news[n_]


def mlstm_scan(p, n_steps, first_row_block, column_view, ib, fb, c0, nm0):
    bsz = p.shape[0]
    G = GROUP_W
    width = _round_up(C_COLS, LANE)
    nb = SCAN_SAMPLES
    f_in, b_in = _chunk_specs(nb, width, n_steps, first_row_block, column_view)
    f_out, b_out = _chunk_specs(nb, G, n_steps, 0, column_view)
    h_sds = jax.ShapeDtypeStruct((bsz, CHUNK, GRID_W * G) if column_view else (bsz, n_steps * CHUNK, G), F32)
    state_specs = [pl.BlockSpec((nb, 2, G, G), lambda b, j: (b, 0, 0, 0)),
                   pl.BlockSpec((nb, 2, SUBLANE, G), lambda b, j: (b, 0, 0, 0))]
    return pl.pallas_call(
        _mlstm_kernel,
        grid=(bsz // nb, n_steps),
        in_specs=[f_in, b_in, _const_spec(ib.shape), _const_spec(fb.shape)] + state_specs,
        out_specs=[f_out, b_out] + state_specs,
        out_shape=[h_sds, h_sds, jax.ShapeDtypeStruct(c0.shape, F32), jax.ShapeDtypeStruct(nm0.shape, F32)],
        compiler_params=_params("arbitrary", "arbitrary"),
        name="mlstm_scan",
    )(p, p, ib, fb, c0, nm0)


def mlstm_mixer(p3d, n_lat, i_b, f_b, norm_g):
    bsz, t, width = p3d.shape
    G = GROUP_W
    n_ctx = t - n_lat
    assert n_lat == CHUNK * GRID_W
    ib = jnp.repeat(i_b, HEAD_DIM, axis=1)
    fb = jnp.repeat(f_b, HEAD_DIM, axis=1)
    c0 = jnp.zeros((bsz, 2, G, G), F32)
    nm0 = jnp.zeros((bsz, 2, SUBLANE, G), F32)
    hf_c, hb_c, c1, nm1 = mlstm_scan(p3d, n_ctx // CHUNK, n_lat // CHUNK, False, ib, fb, c0, nm0)
    hf_l, hb_l, _, _ = mlstm_scan(p3d.reshape(bsz, t // GRID_W, GRID_W * width), GRID_W, 0, True, ib, fb, c1, nm1)
    hf = jnp.concatenate([hf_l.reshape(bsz, n_lat, G), hf_c], axis=1).reshape(bsz * t, G)
    hb = jnp.concatenate([hb_l.reshape(bsz, n_lat, G), hb_c], axis=1).reshape(bsz * t, G)
    return gated_head_norm(hf, hb, p3d.reshape(bsz * t, width), norm_g, False)


def kernel(x, c, ctx, c_ctx, mod_w, mod_b, norm_mix_g, norm_ffn_g, w_in, w_out, lru_conv_w, lru_conv_b, lru_w_a, lru_b_a, lru_w_x, lru_b_x, lru_lambda, rwkv_mu, rwkv_w_up, rwkv_w0, rwkv_a_up, rwkv_a0, rwkv_g_up, rwkv_k_k, rwkv_k_a, rwkv_r_k, rwkv_ln_g, rwkv_ln_b, mlstm_i_b, mlstm_f_b, mlstm_norm_g, gdn_conv_w, gdn_a_log, gdn_dt_bias, gdn_norm_g, ffn_w_gate, ffn_w_up, ffn_w_down, moe_router, moe_w_gate, moe_w_up, moe_w_down, final_norm_g):
    bsz, seq, d = x.shape
    n_ctx = ctx.shape[1]
    t = n_ctx + seq
    n = bsz * t
    tiles_per_seq = t // ROW_TILE
    assert n_ctx == ROW_TILE and seq % ROW_TILE == 0
    silu_c = jax.nn.silu(c)
    silu_cc = jax.nn.silu(c_ctx)
    x2 = jnp.concatenate([x, ctx], axis=1).reshape(n, d)
    group_pads = [(0, OFF_B, A_COLS), (OFF_B, OFF_C, _round_up(B_COLS, LANE)),
                  (OFF_C, OFF_D, _round_up(C_COLS, LANE)), (OFF_D, IN_COLS, _round_up(D_COLS, LANE))]
    for l in range(DEPTH):
        mod = (silu_c @ mod_w[l] + mod_b[l]).reshape(bsz, 6, d)
        mod_c = jnp.broadcast_to((silu_cc @ mod_w[l] + mod_b[l]).reshape(1, 6, d), (bsz, 6, d))
        mods = jnp.stack([mod, mod_c], axis=1)
        sh1, sc1, gt1, sh2, sc2, gt2 = [mods[:, :, j].reshape(2 * bsz, 1, d) for j in range(6)]
        ws = [_pad_cols(w_in[l][:, lo:hi], wd).astype(BF16) for lo, hi, wd in group_pads]
        pa, pb, pc, pd = norm_proj(x2, norm_mix_g[l], sc1, sh1, ws, tiles_per_seq)
        ya = rglru_mixer(pa.reshape(bsz, t, -1), seq, lru_conv_w[l], lru_conv_b[l], lru_w_a[l], lru_b_a[l],
                         lru_w_x[l], lru_b_x[l], lru_lambda[l])
        yb = rwkv7_mixer(pb.reshape(bsz, t, -1), seq, rwkv_mu[l], rwkv_w_up[l], rwkv_w0[l], rwkv_a_up[l],
                         rwkv_a0[l], rwkv_g_up[l], rwkv_k_k[l], rwkv_k_a[l], rwkv_r_k[l],
                         rwkv_ln_g[l], rwkv_ln_b[l])
        yc = mlstm_mixer(pc.reshape(bsz, t, -1), seq, mlstm_i_b[l], mlstm_f_b[l], mlstm_norm_g[l])
        yd = gdn_mixer(pd.reshape(bsz, t, -1), seq, gdn_conv_w[l], gdn_a_log[l], gdn_dt_bias[l], gdn_norm_g[l])
        x2 = out_proj([ya, yb, yc, yd], w_out[l].astype(BF16), x2, gt1, tiles_per_seq)
        if l % 2 == 0:
            x2 = dense_ffn(x2, norm_ffn_g[l], sc2, sh2, gt2, ffn_w_gate[l // 2].astype(BF16),
                           ffn_w_up[l // 2].astype(BF16), ffn_w_down[l // 2].astype(BF16), tiles_per_seq)
        else:
            h2, lg = norm_router(x2, norm_ffn_g[l], sc2, sh2, _pad_cols(moe_router[l // 2], LANE), tiles_per_seq)
            y = moe_ffn(h2, lg[:, :N_EXPERTS], moe_w_gate[l // 2].astype(BF16), moe_w_up[l // 2].astype(BF16),
                        moe_w_down[l // 2].astype(BF16))
            gt_rows = jnp.concatenate([jnp.broadcast_to(gt2.reshape(bsz, 2, 1, d)[:, 0], (bsz, seq, d)),
                                       jnp.broadcast_to(gt2.reshape(bsz, 2, 1, d)[:, 1], (bsz, n_ctx, d))], axis=1)
            x2 = x2 + gt_rows.reshape(n, d) * y
    return final_norm(x2.reshape(bsz, t, d), final_norm_g, seq)
```

```python
import functools
import math

import jax
import jax.numpy as jnp
from jax import lax
from jax.experimental import pallas as pl
from jax.experimental.pallas import tpu as pltpu

D_MODEL = 1024
DEPTH = 4
GRID_W = 64
N_MIXERS = 4
GROUP_W = D_MODEL // N_MIXERS
HEAD_DIM = 64
N_HEADS = GROUP_W // HEAD_DIM
CONV_W = 4
LRU_C = 8.0
RWKV_LORA_W = 32
RWKV_LORA_A = 32
RWKV_LORA_G = 64
RWKV_GN_EPS = 64e-5
CHUNK = 64
D_FF = 2816
N_EXPERTS = 8
TOP_K = 2
D_FF_EXPERT = 1408
MOE_BLOCK = 256
EPS = 1e-6

A_COLS = 2 * GROUP_W
B_COLS = 3 * GROUP_W + 2 * RWKV_LORA_W + 2 * RWKV_LORA_A + RWKV_LORA_G
C_COLS = 4 * GROUP_W + 4 * N_HEADS
D_COLS = 4 * GROUP_W + 4 * N_HEADS
OFF_B = A_COLS
OFF_C = OFF_B + B_COLS
OFF_D = OFF_C + C_COLS
IN_COLS = OFF_D + D_COLS

LANE = 128
SUBLANE = 8
VMEM_LIMIT = 56 * 1024 * 1024
ROW_TILE = 256
SCAN_SAMPLES = 4
F32 = jnp.float32
BF16 = jnp.bfloat16
HI = lax.Precision.HIGHEST


def _round_up(n, m):
    return (n + m - 1) // m * m


def _pad_cols(w, n):
    return jnp.pad(w, ((0, 0), (0, n - w.shape[1])))


def _const_spec(shape):
    nd = len(shape)
    return pl.BlockSpec(shape, lambda *_: (0,) * nd, pipeline_mode=pl.Buffered(1))


def _params(*sem):
    return pltpu.CompilerParams(dimension_semantics=sem, vmem_limit_bytes=VMEM_LIMIT)


def _mod_spec(d, tiles_per_seq):
    def index(i):
        return (2 * (i // tiles_per_seq) + (i % tiles_per_seq) // (tiles_per_seq - 1), 0, 0)
    return pl.BlockSpec((1, 1, d), index)


def _norm_mod(x, g, sc, sh):
    ms = jnp.mean(x * x, axis=-1, keepdims=True)
    return (x * lax.rsqrt(ms + EPS) * g) * (1.0 + sc) + sh


def _dot_dims(a, b, dims, prec):
    if prec is None:
        a, b, prec = a.astype(BF16), b.astype(BF16), None
    return lax.dot_general(a, b, (dims, ((), ())), preferred_element_type=F32, precision=prec)


def _dot(a, b, prec=None):
    return _dot_dims(a, b, ((1,), (0,)), prec)


def _dot_nt(a, b, prec=None):
    return _dot_dims(a, b, ((1,), (1,)), prec)


def _dot_tn(a, b, prec=None):
    return _dot_dims(a, b, ((0,), (0,)), prec)


def _norm_proj_kernel(x_ref, g_ref, sc_ref, sh_ref, *refs, n_out):
    h = _norm_mod(x_ref[...], g_ref[...], sc_ref[0], sh_ref[0]).astype(BF16)
    for w_ref, o_ref in zip(refs[:n_out], refs[n_out:]):
        o_ref[...] = _dot(h, w_ref[...])


def _layer_spec(w, layer):
    return pl.BlockSpec((None,) + w.shape[1:], lambda *_: (layer, 0, 0), pipeline_mode=pl.Buffered(1))


def norm_proj(x2d, g, sc, sh, ws, layer, tiles_per_seq):
    n, d = x2d.shape
    tm = ROW_TILE
    mod_spec = _mod_spec(d, tiles_per_seq)
    return pl.pallas_call(
        functools.partial(_norm_proj_kernel, n_out=len(ws)),
        grid=(n // tm,),
        in_specs=[pl.BlockSpec((tm, d), lambda i: (i, 0)), _const_spec((1, d)), mod_spec, mod_spec]
        + [_layer_spec(w, layer) for w in ws],
        out_specs=[pl.BlockSpec((tm, w.shape[2]), lambda i: (i, 0)) for w in ws],
        out_shape=[jax.ShapeDtypeStruct((n, w.shape[2]), F32) for w in ws],
        compiler_params=_params("arbitrary"),
        name="norm_proj",
    )(x2d, g.reshape(1, d), sc, sh, *ws)


def _out_proj_kernel(a_hf, a_hb, a_gate, b_yf, b_yb, b_bf, b_bb, b_g, c_fl, c_bl, c_fc, c_bc, c_gate,
                     d_fl, d_bl, d_fc, d_bc, d_gate, lng_ref, lnb_ref, cg_ref, dg_ref, w_ref, x_ref, gt_ref,
                     o_ref, *, tiles_per_seq):
    is_ctx = pl.program_id(0) % tiles_per_seq == tiles_per_seq - 1
    seg_mean = jnp.where(_same_head((GROUP_W, GROUP_W)), 1.0 / HEAD_DIM, 0.0).astype(F32)
    ya = (a_hf[...] + a_hb[...]) * jax.nn.gelu(a_gate[...])
    yb = b_yf[...] + b_yb[...]
    mean = _dot(yb, seg_mean, HI)
    var = _dot(jnp.square(yb - mean), seg_mean, HI)
    yb = (yb - mean) * lax.rsqrt(var + RWKV_GN_EPS) * lng_ref[...] + lnb_ref[...]
    yb = (yb + b_bf[...] + b_bb[...]) * b_g[...]
    hc = jnp.where(is_ctx, c_fc[...] + c_bc[...], c_fl[...] + c_bl[...])
    yc = hc * lax.rsqrt(_dot(hc * hc, seg_mean, HI) + EPS) * cg_ref[...] * jax.nn.sigmoid(c_gate[...])
    hd = jnp.where(is_ctx, d_fc[...] + d_bc[...], d_fl[...] + d_bl[...])
    gate = d_gate[...]
    yd = hd * lax.rsqrt(_dot(hd * hd, seg_mean, HI) + EPS) * dg_ref[...] * (gate * jax.nn.sigmoid(gate))
    y = jnp.concatenate([ya, yb, yc, yd], axis=-1).astype(BF16)
    o_ref[...] = x_ref[...] + gt_ref[0] * _dot(y, w_ref[...])


def out_proj(lru, rwkv, mlstm, gdn, pa, pc, pd, ln_g, ln_b, c_g, d_g, w, layer, x2d, gt, tiles_per_seq):
    n, d = x2d.shape
    G = GROUP_W
    tm = ROW_TILE
    lat_tiles = tiles_per_seq - 1
    row = pl.BlockSpec((tm, G), lambda i: (i, 0))
    lat = pl.BlockSpec((tm, G), lambda i: ((i // tiles_per_seq) * lat_tiles
                                             + jnp.minimum(i % tiles_per_seq, lat_tiles - 1), 0))
    ctx = pl.BlockSpec((tm, G), lambda i: (i // tiles_per_seq, 0))
    col = lambda c: pl.BlockSpec((tm, G), lambda i: (i, c))
    vec = _const_spec((1, G))
    return pl.pallas_call(
        functools.partial(_out_proj_kernel, tiles_per_seq=tiles_per_seq),
        grid=(n // tm,),
        in_specs=[row, row, col(1)] + [row] * 5 + [lat, lat, ctx, ctx, col(3)] * 2 + [vec] * 4
        + [_layer_spec(w, layer), pl.BlockSpec((tm, d), lambda i: (i, 0)), _mod_spec(d, tiles_per_seq)],
        out_specs=pl.BlockSpec((tm, d), lambda i: (i, 0)),
        out_shape=jax.ShapeDtypeStruct((n, d), F32),
        compiler_params=_params("arbitrary"),
        name="out_proj",
    )(*lru, pa, *rwkv, *mlstm, pc, *gdn, pd, ln_g.reshape(1, G), ln_b.reshape(1, G), c_g.reshape(1, G),
      d_g.reshape(1, G), w, x2d, gt)


def _swiglu_chunks(hb, wg_ref, wu_ref, wd_ref, n_chunks, fc):
    acc = None
    for c in range(n_chunks):
        a = _dot(hb, wg_ref[:, c * fc:(c + 1) * fc])
        u = _dot(hb, wu_ref[:, c * fc:(c + 1) * fc])
        z = (a * jax.nn.sigmoid(a) * u).astype(BF16)
        part = _dot(z, wd_ref[c * fc:(c + 1) * fc, :])
        acc = part if acc is None else acc + part
    return acc


def _dense_ffn_kernel(x_ref, g_ref, sc_ref, sh_ref, gt_ref, wg_ref, wu_ref, wd_ref, o_ref, *, n_chunks, fc):
    x = x_ref[...]
    hb = _norm_mod(x, g_ref[...], sc_ref[0], sh_ref[0]).astype(BF16)
    o_ref[...] = x + gt_ref[0] * _swiglu_chunks(hb, wg_ref, wu_ref, wd_ref, n_chunks, fc)


def dense_ffn(x2d, g, sc, sh, gt, wg, wu, wd, layer, tiles_per_seq):
    n, d = x2d.shape
    tm = ROW_TILE
    fc = D_FF // 2
    mod_spec = _mod_spec(d, tiles_per_seq)
    return pl.pallas_call(
        functools.partial(_dense_ffn_kernel, n_chunks=D_FF // fc, fc=fc),
        grid=(n // tm,),
        in_specs=[pl.BlockSpec((tm, d), lambda i: (i, 0)), _const_spec((1, d)), mod_spec, mod_spec, mod_spec,
                  _layer_spec(wg, layer), _layer_spec(wu, layer), _layer_spec(wd, layer)],
        out_specs=pl.BlockSpec((tm, d), lambda i: (i, 0)),
        out_shape=jax.ShapeDtypeStruct((n, d), F32),
        compiler_params=_params("arbitrary"),
        name="dense_ffn",
    )(x2d, g.reshape(1, d), sc, sh, gt, wg, wu, wd)


def _router_kernel(x_ref, g_ref, sc_ref, sh_ref, r_ref, h_ref, l_ref):
    h = _norm_mod(x_ref[...], g_ref[...], sc_ref[0], sh_ref[0])
    h_ref[...] = h.astype(BF16)
    l_ref[...] = _dot(h, r_ref[...], HI)


def norm_router(x2d, g, sc, sh, router_pad, layer, tiles_per_seq):
    n, d = x2d.shape
    tm = ROW_TILE
    mod_spec = _mod_spec(d, tiles_per_seq)
    return pl.pallas_call(
        _router_kernel,
        grid=(n // tm,),
        in_specs=[pl.BlockSpec((tm, d), lambda i: (i, 0)), _const_spec((1, d)), mod_spec, mod_spec,
                  _layer_spec(router_pad, layer)],
        out_specs=[pl.BlockSpec((tm, d), lambda i: (i, 0)), pl.BlockSpec((tm, LANE), lambda i: (i, 0))],
        out_shape=[jax.ShapeDtypeStruct((n, d), BF16), jax.ShapeDtypeStruct((n, LANE), F32)],
        compiler_params=_params("arbitrary"),
        name="norm_router",
    )(x2d, g.reshape(1, d), sc, sh, router_pad)


def _expert_kernel(be_ref, nb_ref, xb_ref, wg_ref, wu_ref, wd_ref, o_ref):
    i = pl.program_id(0)

    @pl.when(i < nb_ref[0])
    def _():
        o_ref[...] = _swiglu_chunks(xb_ref[...], wg_ref.at[0], wu_ref.at[0], wd_ref.at[0], 1, D_FF_EXPERT)

    @pl.when(i >= nb_ref[0])
    def _():
        o_ref[...] = jnp.zeros_like(o_ref)


def expert_ffn(xb, block_e, n_used, wg, wu, wd, layer, tm):
    npad, d = xb.shape
    f = wg.shape[3]
    grid_spec = pltpu.PrefetchScalarGridSpec(
        num_scalar_prefetch=2,
        grid=(npad // tm,),
        in_specs=[pl.BlockSpec((tm, d), lambda i, be, nb: (i, 0)),
                  pl.BlockSpec((None, 1, d, f), lambda i, be, nb: (layer, be[i], 0, 0)),
                  pl.BlockSpec((None, 1, d, f), lambda i, be, nb: (layer, be[i], 0, 0)),
                  pl.BlockSpec((None, 1, f, d), lambda i, be, nb: (layer, be[i], 0, 0))],
        out_specs=pl.BlockSpec((tm, d), lambda i, be, nb: (i, 0)),
    )
    return pl.pallas_call(
        _expert_kernel,
        grid_spec=grid_spec,
        out_shape=jax.ShapeDtypeStruct((npad, d), F32),
        compiler_params=_params("arbitrary"),
        name="expert_ffn",
    )(block_e, n_used, xb, wg, wu, wd)


def moe_ffn(h_bf, logits, wg, wu, wd, layer):
    t, d = h_bf.shape
    top_logits, top_idx = lax.top_k(logits, TOP_K)
    gate = jax.nn.softmax(top_logits, axis=-1)
    tk = t * TOP_K
    onehot = (top_idx.reshape(1, tk) == jnp.arange(N_EXPERTS, dtype=jnp.int32)[:, None]).astype(jnp.int32)
    csum = jnp.cumsum(onehot, axis=1)
    counts = csum[:, -1]
    padded = (counts + MOE_BLOCK - 1) // MOE_BLOCK * MOE_BLOCK
    pad_end = jnp.cumsum(padded)
    pad_start = pad_end - padded
    pos = jnp.sum(onehot * (csum - 1 + pad_start[:, None]), axis=0)
    n_blocks = -(-tk // MOE_BLOCK) + N_EXPERTS
    buf_tok = jnp.zeros((n_blocks * MOE_BLOCK,), jnp.int32).at[pos].set(jnp.arange(tk, dtype=jnp.int32) // TOP_K)
    slot_of = pos.reshape(t, TOP_K)
    block_e = jnp.clip(jnp.searchsorted(pad_end, jnp.arange(n_blocks, dtype=jnp.int32) * MOE_BLOCK,
                                        side='right'), 0, N_EXPERTS - 1).astype(jnp.int32)
    n_used = (pad_end[-1] // MOE_BLOCK).astype(jnp.int32).reshape(1)
    yb = expert_ffn(h_bf[buf_tok], block_e, n_used, wg, wu, wd, layer, MOE_BLOCK)
    return gate[:, 0:1] * yb[slot_of[:, 0]] + gate[:, 1:2] * yb[slot_of[:, 1]]


def _final_norm_kernel(x_ref, g_ref, o_ref):
    x = x_ref[...]
    ms = jnp.mean(x * x, axis=-1, keepdims=True)
    o_ref[...] = x * lax.rsqrt(ms + EPS) * g_ref[...]


def final_norm(x3d, g, n_lat):
    bsz, t, d = x3d.shape
    tm = ROW_TILE
    return pl.pallas_call(
        _final_norm_kernel,
        grid=(bsz, n_lat // tm),
        in_specs=[pl.BlockSpec((None, tm, d), lambda b, i: (b, i, 0)), _const_spec((1, d))],
        out_specs=pl.BlockSpec((None, tm, d), lambda b, i: (b, i, 0)),
        out_shape=jax.ShapeDtypeStruct((bsz, n_lat, d), F32),
        compiler_params=_params("arbitrary", "arbitrary"),
        name="final_norm",
    )(x3d, g.reshape(1, d))


def _same_head(shape):
    r = lax.broadcasted_iota(jnp.int32, shape, 0) // HEAD_DIM
    c = lax.broadcasted_iota(jnp.int32, shape, 1) // HEAD_DIM
    return r == c


def _to_block_diag(x, mask):
    return jnp.where(mask, jnp.concatenate([x] * N_HEADS, axis=0), 0.0)


def _token_masks(reverse):
    t = lax.broadcasted_iota(jnp.int32, (CHUNK, GROUP_W), 0)
    s = lax.broadcasted_iota(jnp.int32, (CHUNK, GROUP_W), 1) % HEAD_DIM
    if reverse:
        return s > t, s >= t, s == t
    return s < t, s <= t, s == t


def _cum_matrix(reverse):
    t = lax.broadcasted_iota(jnp.int32, (CHUNK, CHUNK), 0)
    s = lax.broadcasted_iota(jnp.int32, (CHUNK, CHUNK), 1)
    return jnp.where((s >= t) if reverse else (s <= t), 1.0, 0.0).astype(F32)


def _unit_tri_inverse(ls, bd_mask):
    t = lax.broadcasted_iota(jnp.int32, (CHUNK, GROUP_W), 0)
    s = lax.broadcasted_iota(jnp.int32, (CHUNK, GROUP_W), 1) % HEAD_DIM
    blk16 = (t // 16) == (s // 16)
    blk32 = (t // 32) == (s // 32)
    eye = jnp.where(t == s, 1.0, 0.0)
    ps = [jnp.where(blk16, -l, 0.0) for l in ls]
    xs = [eye + p for p in ps]
    for _ in range(3):
        ps = [_dot(p, _to_block_diag(p, bd_mask)) for p in ps]
        xs = [x + _dot(x, _to_block_diag(p, bd_mask)) for x, p in zip(xs, ps)]
    for off_diag in (jnp.logical_and(blk32, jnp.logical_not(blk16)), jnp.logical_not(blk32)):
        xe = [_dot(x, _to_block_diag(jnp.where(off_diag, l, 0.0), bd_mask)) for x, l in zip(xs, ls)]
        xs = [x - _dot(y, _to_block_diag(x, bd_mask)) for x, y in zip(xs, xe)]
    return xs


def _head_lane_expander(col0):
    r = lax.broadcasted_iota(jnp.int32, (LANE, GROUP_W), 0)
    c = lax.broadcasted_iota(jnp.int32, (LANE, GROUP_W), 1) // HEAD_DIM
    return jnp.where(r == col0 + c, 1.0, 0.0).astype(F32)


def _fwd_chunk(j, n_lat_chunks, n_chunks):
    n_ctx_chunks = n_chunks - n_lat_chunks
    return jnp.where(j < n_ctx_chunks, n_lat_chunks + j, j - n_ctx_chunks)


def _bwd_chunk(j, n_lat_chunks, n_chunks):
    return n_chunks - 1 - j


def _seq_edges(c, n_lat_chunks, n_chunks):
    first = jnp.logical_or(c == 0, c == n_lat_chunks)
    last = jnp.logical_or(c == n_lat_chunks - 1, c == n_chunks - 1)
    return first, last


def _shift_rows(x, prev8, next8, back):
    row = lax.broadcasted_iota(jnp.int32, (CHUNK, 1), 0)
    if back == -1:
        return jnp.where(row == CHUNK - 1, next8[0:1, :], pltpu.roll(x, CHUNK - 1, 0))
    out = pltpu.roll(x, back, 0)
    for i in range(back):
        out = jnp.where(row == i, prev8[SUBLANE - back + i:SUBLANE - back + i + 1, :], out)
    return out


def _natural_scan_specs(nb, width, out_width, n_lat_chunks, n_chunks, last8):
    r8 = CHUNK // SUBLANE
    fwd = functools.partial(_fwd_chunk, n_lat_chunks=n_lat_chunks, n_chunks=n_chunks)
    bwd = functools.partial(_bwd_chunk, n_lat_chunks=n_lat_chunks, n_chunks=n_chunks)
    chunk = lambda f, w: pl.BlockSpec((nb, CHUNK, w), lambda b, j: (b, f(j), 0))
    prev = lambda f: pl.BlockSpec((nb, SUBLANE, width), lambda b, j: (b, jnp.maximum(f(j) * r8 - 1, 0), 0))
    nxt = lambda f: pl.BlockSpec((nb, SUBLANE, width), lambda b, j: (b, jnp.minimum((f(j) + 1) * r8, last8), 0))
    ins = [chunk(fwd, width), prev(fwd), nxt(fwd), chunk(bwd, width), prev(bwd), nxt(bwd)]
    return ins, chunk(fwd, out_width), chunk(bwd, out_width)


def _chunk_specs(nb, width, n_steps, first_row_block, column_view):
    if column_view:
        return (pl.BlockSpec((nb, CHUNK, width), lambda b, j: (b, 0, j)),
                pl.BlockSpec((nb, CHUNK, width), lambda b, j: (b, 0, n_steps - 1 - j)))
    return (pl.BlockSpec((nb, CHUNK, width), lambda b, j: (b, first_row_block + j, 0)),
            pl.BlockSpec((nb, CHUNK, width), lambda b, j: (b, first_row_block + n_steps - 1 - j, 0)))


def _halo_specs(nb, width, n_steps, first_row_block, column_view, last8):
    r8 = CHUNK // SUBLANE
    fwd = lambda j: j
    bwd = lambda j: n_steps - 1 - j
    if column_view:
        prev = lambda f: pl.BlockSpec((nb, SUBLANE, width), lambda b, j: (b, r8 - 1, jnp.maximum(f(j) - 1, 0)))
        nxt = lambda f: pl.BlockSpec((nb, SUBLANE, width), lambda b, j: (b, 0, jnp.minimum(f(j) + 1, n_steps - 1)))
    else:
        prev = lambda f: pl.BlockSpec((nb, SUBLANE, width),
                                      lambda b, j: (b, jnp.maximum((first_row_block + f(j)) * r8 - 1, 0), 0))
        nxt = lambda f: pl.BlockSpec((nb, SUBLANE, width),
                                     lambda b, j: (b, jnp.minimum((first_row_block + f(j) + 1) * r8, last8), 0))
    return prev(fwd), nxt(fwd), prev(bwd), nxt(bwd)


def _rwkv_prepare(p, prev_row, next_row, prm, d, reverse):
    (mu_ref, wup_ref, w0_ref, aup_ref, a0_ref, gup_ref, kk_ref, ka_ref, rk_ref) = prm
    G = GROUP_W
    row = lax.broadcasted_iota(jnp.int32, (CHUNK, 1), 0)
    prev = jnp.where(row == 0, prev_row, pltpu.roll(p, 1, 0))
    nxt = jnp.where(row == CHUNK - 1, next_row, pltpu.roll(p, CHUNK - 1, 0))
    ps = p + mu_ref[0:1, :] * (prev - p) + mu_ref[1:2, :] * (nxt - p)
    r, k, v = ps[:, :G], ps[:, G:2 * G], ps[:, 2 * G:3 * G]
    lora_in = ps[:, 3 * G:3 * G + LANE]
    gate_in = ps[:, 3 * G + LANE:3 * G + 2 * LANE]
    bd_mask = _same_head((G, G))
    head_ones = jnp.where(bd_mask, 1.0, 0.0).astype(F32)
    log_w = -math.exp(-0.5) * jax.nn.sigmoid(_dot(jnp.tanh(lora_in), wup_ref[d], HI) + w0_ref[d:d + 1, :])
    a = jax.nn.sigmoid(_dot(lora_in, aup_ref[d], HI) + a0_ref[d:d + 1, :])
    kr = k * kk_ref[...]
    kk = kr * lax.rsqrt(_dot(kr * kr, head_ones, HI) + EPS)
    k_mod = k * (1.0 + (a - 1.0) * ka_ref[...])
    bonus = _dot(r * k_mod * rk_ref[...], head_ones, HI) * v
    lg = _dot(_cum_matrix(reverse), log_w, HI)
    e_neg = jnp.exp(-lg)
    at = kk * jnp.exp(lg - log_w)
    bt = -(kk * a) * e_neg
    kt = k_mod * e_neg
    rt = r * jnp.exp(lg)
    g_last = jnp.exp(lg[0:1, :] if reverse else lg[CHUNK - 1:CHUNK, :])
    strict, incl, _ = _token_masks(reverse)
    gate = None if reverse else _dot(jax.nn.sigmoid(gate_in), gup_ref[...], HI)
    return dict(at=at, rt=rt, bt=bt, kt=kt, v=v, g_last=g_last, bonus=bonus, gate=gate, strict=strict, incl=incl)


def _rwkv_kernel(pf_ref, pfp_ref, pfn_ref, pb_ref, pbp_ref, pbn_ref,
                 mu_ref, wup_ref, w0_ref, aup_ref, a0_ref, gup_ref, kk_ref, ka_ref, rk_ref,
                 yf_ref, bf_ref, g_ref, yb_ref, bb_ref, s_ref, *, n_lat_chunks, n_chunks):
    j = pl.program_id(1)

    @pl.when(j == 0)
    def _():
        s_ref[...] = jnp.zeros_like(s_ref)

    prm = (mu_ref, wup_ref, w0_ref, aup_ref, a0_ref, gup_ref, kk_ref, ka_ref, rk_ref)
    nb = pf_ref.shape[0]
    bd_mask = _same_head((GROUP_W, GROUP_W))
    idx = [(i, d) for i in range(nb) for d in range(2)]
    cs = []
    for i, d in idx:
        chunk = (_bwd_chunk if d else _fwd_chunk)(j, n_lat_chunks, n_chunks)
        first, last = _seq_edges(chunk, n_lat_chunks, n_chunks)
        p_ref, pp_ref, pn_ref = (pb_ref, pbp_ref, pbn_ref) if d else (pf_ref, pfp_ref, pfn_ref)
        prev_row = jnp.where(first, 0.0, pp_ref[i, SUBLANE - 1:SUBLANE, :])
        next_row = jnp.where(last, 0.0, pn_ref[i, 0:1, :])
        cs.append(_rwkv_prepare(p_ref[i], prev_row, next_row, prm, d, bool(d)))
    ss = [s_ref[i, d] for i, d in idx]
    lhs = [jnp.concatenate([c['at'], c['rt']], axis=0) for c in cs]
    m_b = [_dot_nt(l, _to_block_diag(c['bt'], bd_mask)) for l, c in zip(lhs, cs)]
    m_k = [_dot_nt(l, _to_block_diag(c['kt'], bd_mask)) for l, c in zip(lhs, cs)]
    a_ak = [jnp.where(c['strict'], m[:CHUNK], 0.0) for c, m in zip(cs, m_k)]
    a_rb = [jnp.where(c['incl'], m[CHUNK:], 0.0) for c, m in zip(cs, m_b)]
    a_rk = [jnp.where(c['incl'], m[CHUNK:], 0.0) for c, m in zip(cs, m_k)]
    tinv = _unit_tri_inverse([jnp.where(c['strict'], -m[:CHUNK], 0.0) for c, m in zip(cs, m_b)], bd_mask)
    v_bd = [_to_block_diag(c['v'], bd_mask) for c in cs]
    rhs = [_dot_nt(c['at'], s) + _dot(a, vb) for c, s, a, vb in zip(cs, ss, a_ak, v_bd)]
    us = [_dot(t, _to_block_diag(r, bd_mask)) for t, r in zip(tinv, rhs)]
    ys = [_dot_nt(c['rt'], s) + _dot(a, vb) for c, s, a, vb in zip(cs, ss, a_rk, v_bd)]
    ys = [y + _dot(a, _to_block_diag(u, bd_mask)) for y, a, u in zip(ys, a_rb, us)]
    upds = [_dot_tn(jnp.concatenate([u, c['v']], axis=0), jnp.concatenate([c['bt'], c['kt']], axis=0))
            for u, c in zip(us, cs)]
    for n_, (i, d) in enumerate(idx):
        (yb_ref if d else yf_ref)[i] = ys[n_]
        (bb_ref if d else bf_ref)[i] = cs[n_]['bonus']
        if not d:
            g_ref[i] = cs[n_]['gate']
        s_ref[i, d] = cs[n_]['g_last'] * (ss[n_] + jnp.where(bd_mask, upds[n_], 0.0))


def rwkv7_mixer(p3d, n_lat, mu, w_up, w0, a_up, a0, g_up, k_k, k_a, r_k):
    bsz, t, width = p3d.shape
    G = GROUP_W
    n_chunks = t // CHUNK
    n_lat_chunks = n_lat // CHUNK
    mu_pad = _pad_cols(mu, width)
    wup = jnp.zeros((2, LANE, G), F32)
    aup = jnp.zeros((2, LANE, G), F32)
    for d in range(2):
        wup = wup.at[d, d * RWKV_LORA_W:(d + 1) * RWKV_LORA_W].set(w_up[d])
        lo = 2 * RWKV_LORA_W + d * RWKV_LORA_A
        aup = aup.at[d, lo:lo + RWKV_LORA_A].set(a_up[d])
    gup = jnp.zeros((LANE, G), F32).at[:RWKV_LORA_G].set(g_up)
    consts = [mu_pad, wup, w0, aup, a0, gup, k_k.reshape(1, G), k_a.reshape(1, G), r_k.reshape(1, G)]
    nb = math.gcd(SCAN_SAMPLES, bsz)
    ins, f_out, b_out = _natural_scan_specs(nb, width, G, n_lat_chunks, n_chunks, t // SUBLANE - 1)
    out_sds = jax.ShapeDtypeStruct((bsz, t, G), F32)
    yf, bf, gate, yb, bb = pl.pallas_call(
        functools.partial(_rwkv_kernel, n_lat_chunks=n_lat_chunks, n_chunks=n_chunks),
        grid=(bsz // nb, n_chunks),
        in_specs=ins + [_const_spec(c.shape) for c in consts],
        out_specs=[f_out] * 3 + [b_out] * 2,
        out_shape=[out_sds] * 5,
        scratch_shapes=[pltpu.VMEM((nb, 2, G, G), F32)],
        compiler_params=_params("arbitrary", "arbitrary"),
        name="rwkv_scan",
    )(p3d, p3d, p3d, p3d, p3d, p3d, *consts)
    return [a.reshape(bsz * t, G) for a in (yf, yb, bf, bb, gate)]


def _neg_expm1(x):
    series = -x * (1.0 + x * (1 / 2 + x * (1 / 6 + x * (1 / 24 + x * (1 / 120 + x * (1 / 720 + x / 5040))))))
    return jnp.where(x > -0.05, series, 1.0 - jnp.exp(x))


def _lru_chunk(p, prev8, next8, cw_ref, cb_ref, w_ref, bias_ref, lam_ref, h_ref, d, reverse):
    G = GROUP_W
    x = p[:, :G]
    u = (cw_ref[0:1, :] * _shift_rows(x, prev8, next8, 2) + cw_ref[1:2, :] * _shift_rows(x, prev8, next8, 1)
         + cw_ref[2:3, :] * x + cw_ref[3:4, :] * _shift_rows(x, prev8, next8, -1)) + cb_ref[...]
    pre = _dot(u, w_ref[d]) + bias_ref[d:d + 1, :]
    r = jax.nn.sigmoid(pre[:, :G])
    i = jax.nn.sigmoid(pre[:, G:])
    log_a = -LRU_C * r * jax.nn.softplus(-lam_ref[d:d + 1, :])
    a = jnp.exp(log_a)
    b = jnp.sqrt(_neg_expm1(2.0 * log_a)) * (i * u)
    row = lax.broadcasted_iota(jnp.int32, (CHUNK, 1), 0)
    k = 1
    while k < CHUNK:
        shift = CHUNK - k if reverse else k
        valid = (row < CHUNK - k) if reverse else (row >= k)
        b = jnp.where(valid, a * pltpu.roll(b, shift, 0) + b, b)
        a = jnp.where(valid, a * pltpu.roll(a, shift, 0), a)
        k *= 2
    h = b + a * h_ref[0:1, :]
    h_ref[0:1, :] = h[0:1, :] if reverse else h[CHUNK - 1:CHUNK, :]
    return h


def _lru_kernel(pf_ref, pfp_ref, pfn_ref, pb_ref, pbp_ref, pbn_ref, cw_ref, cb_ref, w_ref, bias_ref, lam_ref,
                hf_ref, hb_ref, st_ref, *, n_lat_chunks, n_chunks):
    j = pl.program_id(1)

    @pl.when(j == 0)
    def _():
        st_ref[...] = jnp.zeros_like(st_ref)

    G = GROUP_W
    for i in range(pf_ref.shape[0]):
        first, last = _seq_edges(_fwd_chunk(j, n_lat_chunks, n_chunks), n_lat_chunks, n_chunks)
        prev8 = jnp.where(first, 0.0, pfp_ref[i, :, :G])
        next8 = jnp.where(last, 0.0, pfn_ref[i, :, :G])
        hf_ref[i] = _lru_chunk(pf_ref[i], prev8, next8, cw_ref, cb_ref, w_ref, bias_ref, lam_ref,
                               st_ref.at[i, 0], 0, False)
        first, last = _seq_edges(_bwd_chunk(j, n_lat_chunks, n_chunks), n_lat_chunks, n_chunks)
        prev8 = jnp.where(first, 0.0, pbp_ref[i, :, :G])
        next8 = jnp.where(last, 0.0, pbn_ref[i, :, :G])
        hb_ref[i] = _lru_chunk(pb_ref[i], prev8, next8, cw_ref, cb_ref, w_ref, bias_ref, lam_ref,
                               st_ref.at[i, 1], 1, True)


def rglru_mixer(p3d, n_lat, conv_w, conv_b, w_a, b_a, w_x, b_x, lam):
    bsz, t, width = p3d.shape
    G = GROUP_W
    n_chunks = t // CHUNK
    n_lat_chunks = n_lat // CHUNK
    w = jnp.zeros((2, G, 2 * G), F32)
    for h in range(N_HEADS):
        sl = slice(h * HEAD_DIM, (h + 1) * HEAD_DIM)
        w = w.at[:, sl, sl].set(w_a[:, h])
        w = w.at[:, sl, G + h * HEAD_DIM:G + (h + 1) * HEAD_DIM].set(w_x[:, h])
    consts = [jnp.pad(conv_w, ((0, SUBLANE - CONV_W), (0, 0))), conv_b.reshape(1, G), w,
              jnp.concatenate([b_a, b_x], axis=1), lam]
    nb = math.gcd(SCAN_SAMPLES, bsz)
    ins, f_out, b_out = _natural_scan_specs(nb, width, G, n_lat_chunks, n_chunks, t // SUBLANE - 1)
    h_sds = jax.ShapeDtypeStruct((bsz, t, G), F32)
    hf, hb = pl.pallas_call(
        functools.partial(_lru_kernel, n_lat_chunks=n_lat_chunks, n_chunks=n_chunks),
        grid=(bsz // nb, n_chunks),
        in_specs=ins + [_const_spec(c.shape) for c in consts],
        out_specs=[f_out, b_out],
        out_shape=[h_sds, h_sds],
        scratch_shapes=[pltpu.VMEM((nb, 2, SUBLANE, G), F32)],
        compiler_params=_params("arbitrary", "arbitrary"),
        name="lru_scan",
    )(p3d, p3d, p3d, p3d, p3d, p3d, *consts)
    return hf.reshape(bsz * t, G), hb.reshape(bsz * t, G)


def _gdn_prepare(p, prev8, next8, cw_ref, al_ref, dt_ref, d, reverse):
    G = GROUP_W
    x = p[:, :3 * G]
    conv = (cw_ref[0:1, :] * _shift_rows(x, prev8, next8, 2) + cw_ref[1:2, :] * _shift_rows(x, prev8, next8, 1)
            + cw_ref[2:3, :] * x + cw_ref[3:4, :] * _shift_rows(x, prev8, next8, -1))
    qkv = conv * jax.nn.sigmoid(conv)
    gates = p[:, 4 * G:4 * G + LANE]
    bd_mask = _same_head((G, G))
    head_ones = jnp.where(bd_mask, 1.0, 0.0).astype(F32)
    q, k, v = qkv[:, :G], qkv[:, G:2 * G], qkv[:, 2 * G:]
    q = q * lax.rsqrt(_dot(q * q, head_ones, HI) + EPS) * (HEAD_DIM ** -0.5)
    k = k * lax.rsqrt(_dot(k * k, head_ones, HI) + EPS)
    la = -al_ref[d:d + 1, :] * jax.nn.softplus(_dot(gates, _head_lane_expander(d * N_HEADS), HI) + dt_ref[d:d + 1, :])
    beta = jax.nn.sigmoid(_dot(gates, _head_lane_expander((2 + d) * N_HEADS), HI))
    gam = _dot(_cum_matrix(reverse), la, HI)
    strict, incl, diag = _token_masks(reverse)
    gam_row = jnp.sum(jnp.where(diag, gam, 0.0), axis=0, keepdims=True)
    decay = jnp.exp(jnp.where(incl, gam - gam_row, -jnp.inf))
    k_beta = k * beta
    k_bd = _to_block_diag(k, bd_mask)
    lower = jnp.where(strict, _dot_nt(k_beta, k_bd) * decay, 0.0)
    e_gam = jnp.exp(gam)
    gam_last = gam[0:1, :] if reverse else gam[CHUNK - 1:CHUNK, :]
    return dict(lower=lower, vb=v * beta, kbe=k_beta * e_gam, qk=_dot_nt(q, k_bd) * decay, q_dec=q * e_gam,
                k_dec=k * jnp.exp(gam_last - gam), g_last=jnp.exp(gam_last))


def _gdn_kernel(pf_ref, pfp_ref, pfn_ref, pb_ref, pbp_ref, pbn_ref, cw_ref, al_ref, dt_ref, s0_ref,
                of_ref, ob_ref, s_ref, *, n_steps):
    j = pl.program_id(1)

    @pl.when(j == 0)
    def _():
        s_ref[...] = s0_ref[...]

    width = 3 * GROUP_W
    bd_mask = _same_head((GROUP_W, GROUP_W))
    cs, ss = [], []
    for i in range(pf_ref.shape[0]):
        prev8 = jnp.where(j == 0, 0.0, pfp_ref[i, :, :width])
        next8 = jnp.where(j == n_steps - 1, 0.0, pfn_ref[i, :, :width])
        cs.append(_gdn_prepare(pf_ref[i], prev8, next8, cw_ref, al_ref, dt_ref, 0, False))
        prev8 = jnp.where(j == n_steps - 1, 0.0, pbp_ref[i, :, :width])
        next8 = jnp.where(j == 0, 0.0, pbn_ref[i, :, :width])
        cs.append(_gdn_prepare(pb_ref[i], prev8, next8, cw_ref, al_ref, dt_ref, 1, True))
        ss += [s_ref[i, 0], s_ref[i, 1]]
    tinv = _unit_tri_inverse([c['lower'] for c in cs], bd_mask)
    us = [_dot(t, _to_block_diag(c['vb'], bd_mask)) for t, c in zip(tinv, cs)]
    ws = [_dot(t, _to_block_diag(c['kbe'], bd_mask)) for t, c in zip(tinv, cs)]
    vn = [u - _dot(w, s) for u, w, s in zip(us, ws, ss)]
    outs = [_dot(c['q_dec'], s) + _dot(c['qk'], _to_block_diag(v, bd_mask)) for c, s, v in zip(cs, ss, vn)]
    sn = [c['g_last'] * s + jnp.where(bd_mask, _dot_tn(c['k_dec'], v), 0.0) for c, s, v in zip(cs, ss, vn)]
    for i in range(pf_ref.shape[0]):
        of_ref[i] = outs[2 * i]
        ob_ref[i] = outs[2 * i + 1]
        s_ref[i, 0] = sn[2 * i]
        s_ref[i, 1] = sn[2 * i + 1]


def gdn_scan(p, n_steps, first_row_block, column_view, last8, cw, al, dt, s0):
    bsz = p.shape[0]
    G = GROUP_W
    width = _round_up(D_COLS, LANE)
    nb = math.gcd(SCAN_SAMPLES, bsz)
    f_in, b_in = _chunk_specs(nb, width, n_steps, first_row_block, column_view)
    fp, fn, bp, bn = _halo_specs(nb, width, n_steps, first_row_block, column_view, last8)
    f_out, b_out = _chunk_specs(nb, G, n_steps, 0, column_view)
    h_sds = jax.ShapeDtypeStruct((bsz, CHUNK, GRID_W * G) if column_view else (bsz, n_steps * CHUNK, G), F32)
    state_spec = pl.BlockSpec((nb, 2, G, G), lambda b, j: (b, 0, 0, 0))
    return pl.pallas_call(
        functools.partial(_gdn_kernel, n_steps=n_steps),
        grid=(bsz // nb, n_steps),
        in_specs=[f_in, fp, fn, b_in, bp, bn, _const_spec(cw.shape), _const_spec(al.shape), _const_spec(dt.shape),
                  state_spec],
        out_specs=[f_out, b_out, state_spec],
        out_shape=[h_sds, h_sds, jax.ShapeDtypeStruct(s0.shape, F32)],
        compiler_params=_params("arbitrary", "arbitrary"),
        name="gdn_scan",
    )(p, p, p, p, p, p, cw, al, dt, s0)


def gdn_mixer(p3d, n_lat, conv_w, a_log, dt_bias):
    bsz, t, width = p3d.shape
    G = GROUP_W
    n_ctx = t - n_lat
    assert n_lat == CHUNK * GRID_W
    al = jnp.repeat(jnp.exp(a_log), HEAD_DIM, axis=1)
    dt = jnp.repeat(dt_bias, HEAD_DIM, axis=1)
    cw = jnp.pad(conv_w, ((0, SUBLANE - CONV_W), (0, 0)))
    s0 = jnp.zeros((bsz, 2, G, G), F32)
    last8 = t // SUBLANE - 1
    of_c, ob_c, s1 = gdn_scan(p3d, n_ctx // CHUNK, n_lat // CHUNK, False, last8, cw, al, dt, s0)
    of_l, ob_l, _ = gdn_scan(p3d.reshape(bsz, t // GRID_W, GRID_W * width), GRID_W, 0, True, last8, cw, al, dt, s1)
    return (of_l.reshape(bsz * n_lat, G), ob_l.reshape(bsz * n_lat, G),
            of_c.reshape(bsz * n_ctx, G), ob_c.reshape(bsz * n_ctx, G))


def _head_max(x):
    lane_head = lax.broadcasted_iota(jnp.int32, x.shape, 1) // HEAD_DIM
    out = jnp.zeros_like(x)
    for h in range(N_HEADS):
        mh = jnp.max(jnp.where(lane_head == h, x, -jnp.inf), axis=-1, keepdims=True)
        out = jnp.where(lane_head == h, mh, out)
    return out


def _mlstm_prepare(p, ib_ref, fb_ref, d, reverse):
    G = GROUP_W
    q, k, v = p[:, :G], p[:, G:2 * G] * (HEAD_DIM ** -0.5), p[:, 2 * G:3 * G]
    gates = p[:, 4 * G:4 * G + LANE]
    bd_mask = _same_head((G, G))
    li = _dot(gates, _head_lane_expander(d * N_HEADS), HI) + ib_ref[d:d + 1, :]
    lf = jax.nn.log_sigmoid(_dot(gates, _head_lane_expander((2 + d) * N_HEADS), HI) + fb_ref[d:d + 1, :])
    b = _dot(_cum_matrix(reverse), lf, HI)
    _, incl, diag = _token_masks(reverse)
    g = li - b
    g_row = jnp.sum(jnp.where(diag, g, 0.0), axis=0, keepdims=True)
    log_d = jnp.where(incl, b + g_row, -jnp.inf)
    b_last = b[0:1, :] if reverse else b[CHUNK - 1:CHUNK, :]
    log_end = b_last + g
    return dict(q=q, k=k, v=v, b=b, log_d=log_d, m_intra=_head_max(log_d), b_last=b_last, log_end=log_end,
                end_max=jnp.max(log_end, axis=0, keepdims=True), qk=_dot_nt(q, _to_block_diag(k, bd_mask)))


def _mlstm_kernel(pf_ref, pb_ref, ib_ref, fb_ref, c0_ref, nm0_ref, hf_ref, hb_ref, c_ref, nm_ref):
    @pl.when(pl.program_id(1) == 0)
    def _():
        c_ref[...] = c0_ref[...]
        nm_ref[...] = nm0_ref[...]

    nb = pf_ref.shape[0]
    bd_mask = _same_head((GROUP_W, GROUP_W))
    head_ones = jnp.where(bd_mask, 1.0, 0.0).astype(F32)
    idx = [(i, d) for i in range(nb) for d in range(2)]
    cs = [_mlstm_prepare((pb_ref if d else pf_ref)[i], ib_ref, fb_ref, d, bool(d)) for i, d in idx]
    mems = [c_ref[i, d] for i, d in idx]
    nrms = [nm_ref[i, d, 0:1, :] for i, d in idx]
    m_prevs = [nm_ref[i, d, 1:2, :] for i, d in idx]
    m_inters = [c['b'] + m for c, m in zip(cs, m_prevs)]
    m_ts = [jnp.maximum(mi, c['m_intra']) for mi, c in zip(m_inters, cs)]
    w_inters = [jnp.exp(mi - mt) for mi, mt in zip(m_inters, m_ts)]
    pms = [jnp.exp(c['log_d'] - mt) * c['qk'] for c, mt in zip(cs, m_ts)]
    inter = [_dot_nt(c['q'], mem) for c, mem in zip(cs, mems)]
    nums = [_dot(pm, _to_block_diag(c['v'], bd_mask)) + w * it for pm, c, w, it in zip(pms, cs, w_inters, inter)]
    dens = [_dot(pm, head_ones, HI) + w * _dot(c['q'] * nrm, head_ones, HI)
            for pm, w, c, nrm in zip(pms, w_inters, cs, nrms)]
    hs = [num / jnp.maximum(jnp.abs(den), jnp.exp(-mt)) for num, den, mt in zip(nums, dens, m_ts)]
    m_news = [jnp.maximum(c['b_last'] + m, c['end_max']) for c, m in zip(cs, m_prevs)]
    w_prevs = [jnp.exp(c['b_last'] + m - mn) for c, m, mn in zip(cs, m_prevs, m_news)]
    w_news = [jnp.exp(c['log_end'] - mn) for c, mn in zip(cs, m_news)]
    upds = [_dot_tn(wn * c['v'], c['k']) for wn, c in zip(w_news, cs)]
    for n_, (i, d) in enumerate(idx):
        (hb_ref if d else hf_ref)[i] = hs[n_]
        c_ref[i, d] = w_prevs[n_] * mems[n_] + jnp.where(bd_mask, upds[n_], 0.0)
        nm_ref[i, d, 0:1, :] = w_prevs[n_] * nrms[n_] + jnp.sum(w_news[n_] * cs[n_]['k'], axis=0, keepdims=True)
        nm_ref[i, d, 1:2, :] = m_news[n_]


def mlstm_scan(p, n_steps, first_row_block, column_view, ib, fb, c0, nm0):
    bsz = p.shape[0]
    G = GROUP_W
    width = _round_up(C_COLS, LANE)
    nb = math.gcd(SCAN_SAMPLES, bsz)
    f_in, b_in = _chunk_specs(nb, width, n_steps, first_row_block, column_view)
    f_out, b_out = _chunk_specs(nb, G, n_steps, 0, column_view)
    h_sds = jax.ShapeDtypeStruct((bsz, CHUNK, GRID_W * G) if column_view else (bsz, n_steps * CHUNK, G), F32)
    state_specs = [pl.BlockSpec((nb, 2, G, G), lambda b, j: (b, 0, 0, 0)),
                   pl.BlockSpec((nb, 2, SUBLANE, G), lambda b, j: (b, 0, 0, 0))]
    return pl.pallas_call(
        _mlstm_kernel,
        grid=(bsz // nb, n_steps),
        in_specs=[f_in, b_in, _const_spec(ib.shape), _const_spec(fb.shape)] + state_specs,
        out_specs=[f_out, b_out] + state_specs,
        out_shape=[h_sds, h_sds, jax.ShapeDtypeStruct(c0.shape, F32), jax.ShapeDtypeStruct(nm0.shape, F32)],
        compiler_params=_params("arbitrary", "arbitrary"),
        name="mlstm_scan",
    )(p, p, ib, fb, c0, nm0)


def mlstm_mixer(p3d, n_lat, i_b, f_b):
    bsz, t, width = p3d.shape
    G = GROUP_W
    n_ctx = t - n_lat
    assert n_lat == CHUNK * GRID_W
    ib = jnp.repeat(i_b, HEAD_DIM, axis=1)
    fb = jnp.repeat(f_b, HEAD_DIM, axis=1)
    c0 = jnp.zeros((bsz, 2, G, G), F32)
    nm0 = jnp.zeros((bsz, 2, SUBLANE, G), F32)
    hf_c, hb_c, c1, nm1 = mlstm_scan(p3d, n_ctx // CHUNK, n_lat // CHUNK, False, ib, fb, c0, nm0)
    hf_l, hb_l, _, _ = mlstm_scan(p3d.reshape(bsz, t // GRID_W, GRID_W * width), GRID_W, 0, True, ib, fb, c1, nm1)
    return (hf_l.reshape(bsz * n_lat, G), hb_l.reshape(bsz * n_lat, G),
            hf_c.reshape(bsz * n_ctx, G), hb_c.reshape(bsz * n_ctx, G))


def kernel(x, c, ctx, c_ctx, mod_w, mod_b, norm_mix_g, norm_ffn_g, w_in, w_out, lru_conv_w, lru_conv_b, lru_w_a, lru_b_a, lru_w_x, lru_b_x, lru_lambda, rwkv_mu, rwkv_w_up, rwkv_w0, rwkv_a_up, rwkv_a0, rwkv_g_up, rwkv_k_k, rwkv_k_a, rwkv_r_k, rwkv_ln_g, rwkv_ln_b, mlstm_i_b, mlstm_f_b, mlstm_norm_g, gdn_conv_w, gdn_a_log, gdn_dt_bias, gdn_norm_g, ffn_w_gate, ffn_w_up, ffn_w_down, moe_router, moe_w_gate, moe_w_up, moe_w_down, final_norm_g):
    bsz, seq, d = x.shape
    n_ctx = ctx.shape[1]
    t = n_ctx + seq
    n = bsz * t
    tiles_per_seq = t // ROW_TILE
    assert n_ctx == ROW_TILE and seq % ROW_TILE == 0
    x2 = jnp.concatenate([x, ctx], axis=1).reshape(n, d)
    mod = jnp.einsum('bd,ldk->lbk', jax.nn.silu(c), mod_w) + mod_b[:, None]
    mod_c = jnp.einsum('d,ldk->lk', jax.nn.silu(c_ctx), mod_w) + mod_b
    mods = jnp.stack([mod, jnp.broadcast_to(mod_c[:, None], mod.shape)], axis=2).reshape(DEPTH, 2 * bsz, 6, 1, d)
    w_groups = [jnp.pad(w_in[:, :, lo:hi], ((0, 0), (0, 0), (0, _round_up(hi - lo, LANE) - (hi - lo)))).astype(BF16)
                for lo, hi in ((0, OFF_B), (OFF_B, OFF_C), (OFF_C, OFF_D), (OFF_D, IN_COLS))]
    w_out_bf = w_out.astype(BF16)
    ffn_bf = [w.astype(BF16) for w in (ffn_w_gate, ffn_w_up, ffn_w_down)]
    moe_bf = [w.astype(BF16) for w in (moe_w_gate, moe_w_up, moe_w_down)]
    router_pad = jnp.pad(moe_router, ((0, 0), (0, 0), (0, LANE - N_EXPERTS)))
    for l in range(DEPTH):
        sh1, sc1, gt1, sh2, sc2, gt2 = [mods[l, :, j] for j in range(6)]
        pa, pb, pc, pd = norm_proj(x2, norm_mix_g[l], sc1, sh1, w_groups, l, tiles_per_seq)
        lru = rglru_mixer(pa.reshape(bsz, t, -1), seq, lru_conv_w[l], lru_conv_b[l], lru_w_a[l], lru_b_a[l],
                          lru_w_x[l], lru_b_x[l], lru_lambda[l])
        rwkv = rwkv7_mixer(pb.reshape(bsz, t, -1), seq, rwkv_mu[l], rwkv_w_up[l], rwkv_w0[l], rwkv_a_up[l],
                           rwkv_a0[l], rwkv_g_up[l], rwkv_k_k[l], rwkv_k_a[l], rwkv_r_k[l])
        mlstm = mlstm_mixer(pc.reshape(bsz, t, -1), seq, mlstm_i_b[l], mlstm_f_b[l])
        gdn = gdn_mixer(pd.reshape(bsz, t, -1), seq, gdn_conv_w[l], gdn_a_log[l], gdn_dt_bias[l])
        x2 = out_proj(lru, rwkv, mlstm, gdn, pa, pc, pd, rwkv_ln_g[l], rwkv_ln_b[l], mlstm_norm_g[l],
                      gdn_norm_g[l], w_out_bf, l, x2, gt1, tiles_per_seq)
        if l % 2 == 0:
            x2 = dense_ffn(x2, norm_ffn_g[l], sc2, sh2, gt2, *ffn_bf, l // 2, tiles_per_seq)
        else:
            h2, lg = norm_router(x2, norm_ffn_g[l], sc2, sh2, router_pad, l // 2, tiles_per_seq)
            y = moe_ffn(h2, lg[:, :N_EXPERTS], *moe_bf, l // 2)
            gt_rows = jnp.concatenate([jnp.broadcast_to(gt2.reshape(bsz, 2, 1, d)[:, 0], (bsz, seq, d)),
                                       jnp.broadcast_to(gt2.reshape(bsz, 2, 1, d)[:, 1], (bsz, n_ctx, d))], axis=1)
            x2 = x2 + gt_rows.reshape(n, d) * y
    return final_norm(x2.reshape(bsz, t, d), final_norm_g, seq)
```

```python
import functools
import math

import jax
import jax.numpy as jnp
from jax import lax
from jax.experimental import pallas as pl
from jax.experimental.pallas import tpu as pltpu

D_MODEL = 1024
DEPTH = 4
GRID_W = 64
N_MIXERS = 4
GROUP_W = D_MODEL // N_MIXERS
HEAD_DIM = 64
N_HEADS = GROUP_W // HEAD_DIM
CONV_W = 4
LRU_C = 8.0
RWKV_LORA_W = 32
RWKV_LORA_A = 32
RWKV_LORA_G = 64
RWKV_GN_EPS = 64e-5
CHUNK = 64
D_FF = 2816
N_EXPERTS = 8
TOP_K = 2
D_FF_EXPERT = 1408
MOE_BLOCK = 256
EPS = 1e-6

A_COLS = 2 * GROUP_W
B_COLS = 3 * GROUP_W + 2 * RWKV_LORA_W + 2 * RWKV_LORA_A + RWKV_LORA_G
C_COLS = 4 * GROUP_W + 4 * N_HEADS
D_COLS = 4 * GROUP_W + 4 * N_HEADS
OFF_B = A_COLS
OFF_C = OFF_B + B_COLS
OFF_D = OFF_C + C_COLS
IN_COLS = OFF_D + D_COLS

LANE = 128
SUBLANE = 8
VMEM_LIMIT = 56 * 1024 * 1024
ROW_TILE = 256
SCAN_SAMPLES = 4
F32 = jnp.float32
BF16 = jnp.bfloat16
HI = lax.Precision.HIGHEST


def _round_up(n, m):
    return (n + m - 1) // m * m


def _pad_cols(w, n):
    return jnp.pad(w, ((0, 0), (0, n - w.shape[1])))


def _const_spec(shape):
    nd = len(shape)
    return pl.BlockSpec(shape, lambda *_: (0,) * nd, pipeline_mode=pl.Buffered(1))


def _params(*sem):
    return pltpu.CompilerParams(dimension_semantics=sem, vmem_limit_bytes=VMEM_LIMIT)


def _mod_spec(d, tiles_per_seq):
    def index(i):
        return (2 * (i // tiles_per_seq) + (i % tiles_per_seq) // (tiles_per_seq - 1), 0, 0)
    return pl.BlockSpec((1, 1, d), index)


def _norm_mod(x, g, sc, sh):
    ms = jnp.mean(x * x, axis=-1, keepdims=True)
    return (x * lax.rsqrt(ms + EPS) * g) * (1.0 + sc) + sh


def _dot_dims(a, b, dims, prec):
    if prec is None:
        a, b, prec = a.astype(BF16), b.astype(BF16), None
    return lax.dot_general(a, b, (dims, ((), ())), preferred_element_type=F32, precision=prec)


def _dot(a, b, prec=None):
    return _dot_dims(a, b, ((1,), (0,)), prec)


def _dot_nt(a, b, prec=None):
    return _dot_dims(a, b, ((1,), (1,)), prec)


def _dot_tn(a, b, prec=None):
    return _dot_dims(a, b, ((0,), (0,)), prec)


def _split3(x):
    x1 = x.astype(BF16)
    r = x - x1.astype(F32)
    x2 = r.astype(BF16)
    return x1, x2, (r - x2.astype(F32)).astype(BF16)


def _dot_sel(a, sel):
    s = sel.astype(BF16)
    p1, p2, p3 = [lax.dot_general(p, s, (((1,), (0,)), ((), ())), preferred_element_type=F32) for p in _split3(a)]
    return p1 + p2 + p3


def _sel_dot(sel, b):
    s = sel.astype(BF16)
    p1, p2, p3 = [lax.dot_general(s, p, (((1,), (0,)), ((), ())), preferred_element_type=F32) for p in _split3(b)]
    return p1 + p2 + p3


def _norm_proj_kernel(x_ref, g_ref, sc_ref, sh_ref, *refs, n_out):
    h = _norm_mod(x_ref[...], g_ref[...], sc_ref[0], sh_ref[0]).astype(BF16)
    for w_ref, o_ref in zip(refs[:n_out], refs[n_out:]):
        o_ref[...] = _dot(h, w_ref[...])


def _layer_spec(w, layer):
    return pl.BlockSpec((None,) + w.shape[1:], lambda *_: (layer, 0, 0), pipeline_mode=pl.Buffered(1))


def norm_proj(x2d, g, sc, sh, ws, layer, tiles_per_seq):
    n, d = x2d.shape
    tm = ROW_TILE
    mod_spec = _mod_spec(d, tiles_per_seq)
    return pl.pallas_call(
        functools.partial(_norm_proj_kernel, n_out=len(ws)),
        grid=(n // tm,),
        in_specs=[pl.BlockSpec((tm, d), lambda i: (i, 0)), _const_spec((1, d)), mod_spec, mod_spec]
        + [_layer_spec(w, layer) for w in ws],
        out_specs=[pl.BlockSpec((tm, w.shape[2]), lambda i: (i, 0)) for w in ws],
        out_shape=[jax.ShapeDtypeStruct((n, w.shape[2]), F32) for w in ws],
        compiler_params=_params("arbitrary"),
        name="norm_proj",
    )(x2d, g.reshape(1, d), sc, sh, *ws)


def _out_proj_kernel(a_hf, a_hb, a_gate, b_yf, b_yb, b_bf, b_bb, b_g, c_fl, c_bl, c_fc, c_bc, c_gate,
                     d_fl, d_bl, d_fc, d_bc, d_gate, lng_ref, lnb_ref, cg_ref, dg_ref, w_ref, x_ref, gt_ref,
                     o_ref, *, tiles_per_seq):
    is_ctx = pl.program_id(0) % tiles_per_seq == tiles_per_seq - 1
    seg_mean = jnp.where(_same_head((GROUP_W, GROUP_W)), 1.0 / HEAD_DIM, 0.0).astype(F32)
    ya = (a_hf[...] + a_hb[...]) * jax.nn.gelu(a_gate[...])
    yb = b_yf[...] + b_yb[...]
    mean = _dot_sel(yb, seg_mean)
    var = _dot_sel(jnp.square(yb - mean), seg_mean)
    yb = (yb - mean) * lax.rsqrt(var + RWKV_GN_EPS) * lng_ref[...] + lnb_ref[...]
    yb = (yb + b_bf[...] + b_bb[...]) * b_g[...]
    hc = jnp.where(is_ctx, c_fc[...] + c_bc[...], c_fl[...] + c_bl[...])
    yc = hc * lax.rsqrt(_dot_sel(hc * hc, seg_mean) + EPS) * cg_ref[...] * jax.nn.sigmoid(c_gate[...])
    hd = jnp.where(is_ctx, d_fc[...] + d_bc[...], d_fl[...] + d_bl[...])
    gate = d_gate[...]
    yd = hd * lax.rsqrt(_dot_sel(hd * hd, seg_mean) + EPS) * dg_ref[...] * (gate * jax.nn.sigmoid(gate))
    y = jnp.concatenate([ya, yb, yc, yd], axis=-1).astype(BF16)
    o_ref[...] = x_ref[...] + gt_ref[0] * _dot(y, w_ref[...])


def out_proj(lru, rwkv, mlstm, gdn, pa, pc, pd, ln_g, ln_b, c_g, d_g, w, layer, x2d, gt, tiles_per_seq):
    n, d = x2d.shape
    G = GROUP_W
    tm = ROW_TILE
    lat_tiles = tiles_per_seq - 1
    row = pl.BlockSpec((tm, G), lambda i: (i, 0))
    lat = pl.BlockSpec((tm, G), lambda i: ((i // tiles_per_seq) * lat_tiles
                                             + jnp.minimum(i % tiles_per_seq, lat_tiles - 1), 0))
    ctx = pl.BlockSpec((tm, G), lambda i: (i // tiles_per_seq, 0))
    col = lambda c: pl.BlockSpec((tm, G), lambda i: (i, c))
    vec = _const_spec((1, G))
    return pl.pallas_call(
        functools.partial(_out_proj_kernel, tiles_per_seq=tiles_per_seq),
        grid=(n // tm,),
        in_specs=[row, row, col(1)] + [row] * 5 + [lat, lat, ctx, ctx, col(3)] * 2 + [vec] * 4
        + [_layer_spec(w, layer), pl.BlockSpec((tm, d), lambda i: (i, 0)), _mod_spec(d, tiles_per_seq)],
        out_specs=pl.BlockSpec((tm, d), lambda i: (i, 0)),
        out_shape=jax.ShapeDtypeStruct((n, d), F32),
        compiler_params=_params("arbitrary"),
        name="out_proj",
    )(*lru, pa, *rwkv, *mlstm, pc, *gdn, pd, ln_g.reshape(1, G), ln_b.reshape(1, G), c_g.reshape(1, G),
      d_g.reshape(1, G), w, x2d, gt)


def _swiglu_chunks(hb, wg_ref, wu_ref, wd_ref, n_chunks, fc):
    acc = None
    for c in range(n_chunks):
        a = _dot(hb, wg_ref[:, c * fc:(c + 1) * fc])
        u = _dot(hb, wu_ref[:, c * fc:(c + 1) * fc])
        z = (a * jax.nn.sigmoid(a) * u).astype(BF16)
        part = _dot(z, wd_ref[c * fc:(c + 1) * fc, :])
        acc = part if acc is None else acc + part
    return acc


def _dense_ffn_kernel(x_ref, g_ref, sc_ref, sh_ref, gt_ref, wg_ref, wu_ref, wd_ref, o_ref, *, n_chunks, fc):
    x = x_ref[...]
    hb = _norm_mod(x, g_ref[...], sc_ref[0], sh_ref[0]).astype(BF16)
    o_ref[...] = x + gt_ref[0] * _swiglu_chunks(hb, wg_ref, wu_ref, wd_ref, n_chunks, fc)


def dense_ffn(x2d, g, sc, sh, gt, wg, wu, wd, layer, tiles_per_seq):
    n, d = x2d.shape
    tm = ROW_TILE
    fc = D_FF // 2
    mod_spec = _mod_spec(d, tiles_per_seq)
    return pl.pallas_call(
        functools.partial(_dense_ffn_kernel, n_chunks=D_FF // fc, fc=fc),
        grid=(n // tm,),
        in_specs=[pl.BlockSpec((tm, d), lambda i: (i, 0)), _const_spec((1, d)), mod_spec, mod_spec, mod_spec,
                  _layer_spec(wg, layer), _layer_spec(wu, layer), _layer_spec(wd, layer)],
        out_specs=pl.BlockSpec((tm, d), lambda i: (i, 0)),
        out_shape=jax.ShapeDtypeStruct((n, d), F32),
        compiler_params=_params("arbitrary"),
        name="dense_ffn",
    )(x2d, g.reshape(1, d), sc, sh, gt, wg, wu, wd)


def _router_kernel(x_ref, g_ref, sc_ref, sh_ref, r_ref, h_ref, l_ref):
    h = _norm_mod(x_ref[...], g_ref[...], sc_ref[0], sh_ref[0])
    h_ref[...] = h.astype(BF16)
    l_ref[...] = _dot(h, r_ref[...], HI)


def norm_router(x2d, g, sc, sh, router_pad, layer, tiles_per_seq):
    n, d = x2d.shape
    tm = ROW_TILE
    mod_spec = _mod_spec(d, tiles_per_seq)
    return pl.pallas_call(
        _router_kernel,
        grid=(n // tm,),
        in_specs=[pl.BlockSpec((tm, d), lambda i: (i, 0)), _const_spec((1, d)), mod_spec, mod_spec,
                  _layer_spec(router_pad, layer)],
        out_specs=[pl.BlockSpec((tm, d), lambda i: (i, 0)), pl.BlockSpec((tm, LANE), lambda i: (i, 0))],
        out_shape=[jax.ShapeDtypeStruct((n, d), BF16), jax.ShapeDtypeStruct((n, LANE), F32)],
        compiler_params=_params("arbitrary"),
        name="norm_router",
    )(x2d, g.reshape(1, d), sc, sh, router_pad)


def _expert_kernel(be_ref, nb_ref, xb_ref, wg_ref, wu_ref, wd_ref, o_ref):
    i = pl.program_id(0)

    @pl.when(i < nb_ref[0])
    def _():
        o_ref[...] = _swiglu_chunks(xb_ref[...], wg_ref.at[0], wu_ref.at[0], wd_ref.at[0], 1, D_FF_EXPERT)

    @pl.when(i >= nb_ref[0])
    def _():
        o_ref[...] = jnp.zeros_like(o_ref)


def expert_ffn(xb, block_e, n_used, wg, wu, wd, layer, tm):
    npad, d = xb.shape
    f = wg.shape[3]
    grid_spec = pltpu.PrefetchScalarGridSpec(
        num_scalar_prefetch=2,
        grid=(npad // tm,),
        in_specs=[pl.BlockSpec((tm, d), lambda i, be, nb: (i, 0)),
                  pl.BlockSpec((None, 1, d, f), lambda i, be, nb: (layer, be[i], 0, 0)),
                  pl.BlockSpec((None, 1, d, f), lambda i, be, nb: (layer, be[i], 0, 0)),
                  pl.BlockSpec((None, 1, f, d), lambda i, be, nb: (layer, be[i], 0, 0))],
        out_specs=pl.BlockSpec((tm, d), lambda i, be, nb: (i, 0)),
    )
    return pl.pallas_call(
        _expert_kernel,
        grid_spec=grid_spec,
        out_shape=jax.ShapeDtypeStruct((npad, d), F32),
        compiler_params=_params("arbitrary"),
        name="expert_ffn",
    )(block_e, n_used, xb, wg, wu, wd)


def moe_ffn(h_bf, logits, wg, wu, wd, layer):
    t, d = h_bf.shape
    top_logits, top_idx = lax.top_k(logits, TOP_K)
    gate = jax.nn.softmax(top_logits, axis=-1)
    tk = t * TOP_K
    onehot = (top_idx.reshape(1, tk) == jnp.arange(N_EXPERTS, dtype=jnp.int32)[:, None]).astype(jnp.int32)
    csum = jnp.cumsum(onehot, axis=1)
    counts = csum[:, -1]
    padded = (counts + MOE_BLOCK - 1) // MOE_BLOCK * MOE_BLOCK
    pad_end = jnp.cumsum(padded)
    pad_start = pad_end - padded
    pos = jnp.sum(onehot * (csum - 1 + pad_start[:, None]), axis=0)
    n_blocks = -(-tk // MOE_BLOCK) + N_EXPERTS
    buf_tok = jnp.zeros((n_blocks * MOE_BLOCK,), jnp.int32).at[pos].set(jnp.arange(tk, dtype=jnp.int32) // TOP_K)
    slot_of = pos.reshape(t, TOP_K)
    block_e = jnp.clip(jnp.searchsorted(pad_end, jnp.arange(n_blocks, dtype=jnp.int32) * MOE_BLOCK,
                                        side='right'), 0, N_EXPERTS - 1).astype(jnp.int32)
    n_used = (pad_end[-1] // MOE_BLOCK).astype(jnp.int32).reshape(1)
    yb = expert_ffn(h_bf[buf_tok], block_e, n_used, wg, wu, wd, layer, MOE_BLOCK)
    return gate[:, 0:1] * yb[slot_of[:, 0]] + gate[:, 1:2] * yb[slot_of[:, 1]]


def _final_norm_kernel(x_ref, g_ref, o_ref):
    x = x_ref[...]
    ms = jnp.mean(x * x, axis=-1, keepdims=True)
    o_ref[...] = x * lax.rsqrt(ms + EPS) * g_ref[...]


def final_norm(x3d, g, n_lat):
    bsz, t, d = x3d.shape
    tm = ROW_TILE
    return pl.pallas_call(
        _final_norm_kernel,
        grid=(bsz, n_lat // tm),
        in_specs=[pl.BlockSpec((None, tm, d), lambda b, i: (b, i, 0)), _const_spec((1, d))],
        out_specs=pl.BlockSpec((None, tm, d), lambda b, i: (b, i, 0)),
        out_shape=jax.ShapeDtypeStruct((bsz, n_lat, d), F32),
        compiler_params=_params("arbitrary", "arbitrary"),
        name="final_norm",
    )(x3d, g.reshape(1, d))


def _same_head(shape):
    r = lax.broadcasted_iota(jnp.int32, shape, 0) // HEAD_DIM
    c = lax.broadcasted_iota(jnp.int32, shape, 1) // HEAD_DIM
    return r == c


def _to_block_diag(x, mask):
    xb = x.astype(BF16)
    return jnp.where(mask, jnp.concatenate([xb] * N_HEADS, axis=0), jnp.zeros((), BF16))


def _token_masks(reverse):
    t = lax.broadcasted_iota(jnp.int32, (CHUNK, GROUP_W), 0)
    s = lax.broadcasted_iota(jnp.int32, (CHUNK, GROUP_W), 1) % HEAD_DIM
    if reverse:
        return s > t, s >= t, s == t
    return s < t, s <= t, s == t


def _cum_matrix(reverse):
    t = lax.broadcasted_iota(jnp.int32, (CHUNK, CHUNK), 0)
    s = lax.broadcasted_iota(jnp.int32, (CHUNK, CHUNK), 1)
    return jnp.where((s >= t) if reverse else (s <= t), 1.0, 0.0).astype(F32)


def _unit_tri_inverse(ls, bd_mask):
    t = lax.broadcasted_iota(jnp.int32, (CHUNK, GROUP_W), 0)
    s = lax.broadcasted_iota(jnp.int32, (CHUNK, GROUP_W), 1) % HEAD_DIM
    blk16 = (t // 16) == (s // 16)
    blk32 = (t // 32) == (s // 32)
    eye = jnp.where(t == s, 1.0, 0.0)
    ps = [jnp.where(blk16, -l, 0.0) for l in ls]
    xs = [eye + p for p in ps]
    for _ in range(3):
        ps = [_dot(p, _to_block_diag(p, bd_mask)) for p in ps]
        xs = [x + _dot(x, _to_block_diag(p, bd_mask)) for x, p in zip(xs, ps)]
    for off_diag in (jnp.logical_and(blk32, jnp.logical_not(blk16)), jnp.logical_not(blk32)):
        xe = [_dot(x, _to_block_diag(jnp.where(off_diag, l, 0.0), bd_mask)) for x, l in zip(xs, ls)]
        xs = [x - _dot(y, _to_block_diag(x, bd_mask)) for x, y in zip(xs, xe)]
    return xs


def _head_lane_expander(col0):
    r = lax.broadcasted_iota(jnp.int32, (LANE, GROUP_W), 0)
    c = lax.broadcasted_iota(jnp.int32, (LANE, GROUP_W), 1) // HEAD_DIM
    return jnp.where(r == col0 + c, 1.0, 0.0).astype(F32)


def _fwd_chunk(j, n_lat_chunks, n_chunks):
    n_ctx_chunks = n_chunks - n_lat_chunks
    return jnp.where(j < n_ctx_chunks, n_lat_chunks + j, j - n_ctx_chunks)


def _bwd_chunk(j, n_lat_chunks, n_chunks):
    return n_chunks - 1 - j


def _seq_edges(c, n_lat_chunks, n_chunks):
    first = jnp.logical_or(c == 0, c == n_lat_chunks)
    last = jnp.logical_or(c == n_lat_chunks - 1, c == n_chunks - 1)
    return first, last


def _shift_rows(x, prev8, next8, back):
    row = lax.broadcasted_iota(jnp.int32, (CHUNK, 1), 0)
    if back == -1:
        return jnp.where(row == CHUNK - 1, next8[0:1, :], pltpu.roll(x, CHUNK - 1, 0))
    out = pltpu.roll(x, back, 0)
    for i in range(back):
        out = jnp.where(row == i, prev8[SUBLANE - back + i:SUBLANE - back + i + 1, :], out)
    return out


def _natural_scan_specs(nb, width, out_width, n_lat_chunks, n_chunks, last8):
    r8 = CHUNK // SUBLANE
    fwd = functools.partial(_fwd_chunk, n_lat_chunks=n_lat_chunks, n_chunks=n_chunks)
    bwd = functools.partial(_bwd_chunk, n_lat_chunks=n_lat_chunks, n_chunks=n_chunks)
    chunk = lambda f, w: pl.BlockSpec((nb, CHUNK, w), lambda b, j: (b, f(j), 0))
    prev = lambda f: pl.BlockSpec((nb, SUBLANE, width), lambda b, j: (b, jnp.maximum(f(j) * r8 - 1, 0), 0))
    nxt = lambda f: pl.BlockSpec((nb, SUBLANE, width), lambda b, j: (b, jnp.minimum((f(j) + 1) * r8, last8), 0))
    ins = [chunk(fwd, width), prev(fwd), nxt(fwd), chunk(bwd, width), prev(bwd), nxt(bwd)]
    return ins, chunk(fwd, out_width), chunk(bwd, out_width)


def _chunk_specs(nb, width, n_steps, first_row_block, column_view):
    if column_view:
        return (pl.BlockSpec((nb, CHUNK, width), lambda b, j: (b, 0, j)),
                pl.BlockSpec((nb, CHUNK, width), lambda b, j: (b, 0, n_steps - 1 - j)))
    return (pl.BlockSpec((nb, CHUNK, width), lambda b, j: (b, first_row_block + j, 0)),
            pl.BlockSpec((nb, CHUNK, width), lambda b, j: (b, first_row_block + n_steps - 1 - j, 0)))


def _halo_specs(nb, width, n_steps, first_row_block, column_view, last8):
    r8 = CHUNK // SUBLANE
    fwd = lambda j: j
    bwd = lambda j: n_steps - 1 - j
    if column_view:
        prev = lambda f: pl.BlockSpec((nb, SUBLANE, width), lambda b, j: (b, r8 - 1, jnp.maximum(f(j) - 1, 0)))
        nxt = lambda f: pl.BlockSpec((nb, SUBLANE, width), lambda b, j: (b, 0, jnp.minimum(f(j) + 1, n_steps - 1)))
    else:
        prev = lambda f: pl.BlockSpec((nb, SUBLANE, width),
                                      lambda b, j: (b, jnp.maximum((first_row_block + f(j)) * r8 - 1, 0), 0))
        nxt = lambda f: pl.BlockSpec((nb, SUBLANE, width),
                                     lambda b, j: (b, jnp.minimum((first_row_block + f(j) + 1) * r8, last8), 0))
    return prev(fwd), nxt(fwd), prev(bwd), nxt(bwd)


def _rwkv_prepare(p, prev_row, next_row, prm, d, reverse):
    (mu_ref, wup_ref, w0_ref, aup_ref, a0_ref, gup_ref, kk_ref, ka_ref, rk_ref) = prm
    G = GROUP_W
    row = lax.broadcasted_iota(jnp.int32, (CHUNK, 1), 0)
    prev = jnp.where(row == 0, prev_row, pltpu.roll(p, 1, 0))
    nxt = jnp.where(row == CHUNK - 1, next_row, pltpu.roll(p, CHUNK - 1, 0))
    ps = p + mu_ref[0:1, :] * (prev - p) + mu_ref[1:2, :] * (nxt - p)
    r, k, v = ps[:, :G], ps[:, G:2 * G], ps[:, 2 * G:3 * G]
    lora_in = ps[:, 3 * G:3 * G + LANE]
    gate_in = ps[:, 3 * G + LANE:3 * G + 2 * LANE]
    bd_mask = _same_head((G, G))
    head_ones = jnp.where(bd_mask, 1.0, 0.0).astype(F32)
    log_w = -math.exp(-0.5) * jax.nn.sigmoid(_dot(jnp.tanh(lora_in), wup_ref[d], HI) + w0_ref[d:d + 1, :])
    a = jax.nn.sigmoid(_dot(lora_in, aup_ref[d], HI) + a0_ref[d:d + 1, :])
    kr = k * kk_ref[...]
    kk = kr * lax.rsqrt(_dot_sel(kr * kr, head_ones) + EPS)
    k_mod = k * (1.0 + (a - 1.0) * ka_ref[...])
    bonus = _dot_sel(r * k_mod * rk_ref[...], head_ones) * v
    lg = _sel_dot(_cum_matrix(reverse), log_w)
    e_neg = jnp.exp(-lg)
    at = kk * jnp.exp(lg - log_w)
    bt = -(kk * a) * e_neg
    kt = k_mod * e_neg
    rt = r * jnp.exp(lg)
    g_last = jnp.exp(lg[0:1, :] if reverse else lg[CHUNK - 1:CHUNK, :])
    strict, incl, _ = _token_masks(reverse)
    gate = None if reverse else _dot(jax.nn.sigmoid(gate_in), gup_ref[...], HI)
    return dict(at=at, rt=rt, bt=bt, kt=kt, v=v, g_last=g_last, bonus=bonus, gate=gate, strict=strict, incl=incl)


def _rwkv_kernel(pf_ref, pfp_ref, pfn_ref, pb_ref, pbp_ref, pbn_ref,
                 mu_ref, wup_ref, w0_ref, aup_ref, a0_ref, gup_ref, kk_ref, ka_ref, rk_ref,
                 yf_ref, bf_ref, g_ref, yb_ref, bb_ref, s_ref, *, n_lat_chunks, n_chunks):
    j = pl.program_id(1)

    @pl.when(j == 0)
    def _():
        s_ref[...] = jnp.zeros_like(s_ref)

    prm = (mu_ref, wup_ref, w0_ref, aup_ref, a0_ref, gup_ref, kk_ref, ka_ref, rk_ref)
    nb = pf_ref.shape[0]
    bd_mask = _same_head((GROUP_W, GROUP_W))
    idx = [(i, d) for i in range(nb) for d in range(2)]
    cs = []
    for i, d in idx:
        chunk = (_bwd_chunk if d else _fwd_chunk)(j, n_lat_chunks, n_chunks)
        first, last = _seq_edges(chunk, n_lat_chunks, n_chunks)
        p_ref, pp_ref, pn_ref = (pb_ref, pbp_ref, pbn_ref) if d else (pf_ref, pfp_ref, pfn_ref)
        prev_row = jnp.where(first, 0.0, pp_ref[i, SUBLANE - 1:SUBLANE, :])
        next_row = jnp.where(last, 0.0, pn_ref[i, 0:1, :])
        cs.append(_rwkv_prepare(p_ref[i], prev_row, next_row, prm, d, bool(d)))
    ss = [s_ref[i, d] for i, d in idx]
    lhs = [jnp.concatenate([c['at'], c['rt']], axis=0) for c in cs]
    m_b = [_dot_nt(l, _to_block_diag(c['bt'], bd_mask)) for l, c in zip(lhs, cs)]
    m_k = [_dot_nt(l, _to_block_diag(c['kt'], bd_mask)) for l, c in zip(lhs, cs)]
    a_ak = [jnp.where(c['strict'], m[:CHUNK], 0.0) for c, m in zip(cs, m_k)]
    a_rb = [jnp.where(c['incl'], m[CHUNK:], 0.0) for c, m in zip(cs, m_b)]
    a_rk = [jnp.where(c['incl'], m[CHUNK:], 0.0) for c, m in zip(cs, m_k)]
    tinv = _unit_tri_inverse([jnp.where(c['strict'], -m[:CHUNK], 0.0) for c, m in zip(cs, m_b)], bd_mask)
    v_bd = [_to_block_diag(c['v'], bd_mask) for c in cs]
    rhs = [_dot_nt(c['at'], s) + _dot(a, vb) for c, s, a, vb in zip(cs, ss, a_ak, v_bd)]
    us = [_dot(t, _to_block_diag(r, bd_mask)) for t, r in zip(tinv, rhs)]
    ys = [_dot_nt(c['rt'], s) + _dot(a, vb) for c, s, a, vb in zip(cs, ss, a_rk, v_bd)]
    ys = [y + _dot(a, _to_block_diag(u, bd_mask)) for y, a, u in zip(ys, a_rb, us)]
    upds = [_dot_tn(jnp.concatenate([u, c['v']], axis=0), jnp.concatenate([c['bt'], c['kt']], axis=0))
            for u, c in zip(us, cs)]
    for n_, (i, d) in enumerate(idx):
        (yb_ref if d else yf_ref)[i] = ys[n_]
        (bb_ref if d else bf_ref)[i] = cs[n_]['bonus']
        if not d:
            g_ref[i] = cs[n_]['gate']
        s_ref[i, d] = cs[n_]['g_last'] * (ss[n_] + jnp.where(bd_mask, upds[n_], 0.0))


def rwkv7_mixer(p3d, n_lat, mu, w_up, w0, a_up, a0, g_up, k_k, k_a, r_k):
    bsz, t, width = p3d.shape
    G = GROUP_W
    n_chunks = t // CHUNK
    n_lat_chunks = n_lat // CHUNK
    mu_pad = _pad_cols(mu, width)
    wup = jnp.zeros((2, LANE, G), F32)
    aup = jnp.zeros((2, LANE, G), F32)
    for d in range(2):
        wup = wup.at[d, d * RWKV_LORA_W:(d + 1) * RWKV_LORA_W].set(w_up[d])
        lo = 2 * RWKV_LORA_W + d * RWKV_LORA_A
        aup = aup.at[d, lo:lo + RWKV_LORA_A].set(a_up[d])
    gup = jnp.zeros((LANE, G), F32).at[:RWKV_LORA_G].set(g_up)
    consts = [mu_pad, wup, w0, aup, a0, gup, k_k.reshape(1, G), k_a.reshape(1, G), r_k.reshape(1, G)]
    nb = math.gcd(SCAN_SAMPLES, bsz)
    ins, f_out, b_out = _natural_scan_specs(nb, width, G, n_lat_chunks, n_chunks, t // SUBLANE - 1)
    out_sds = jax.ShapeDtypeStruct((bsz, t, G), F32)
    yf, bf, gate, yb, bb = pl.pallas_call(
        functools.partial(_rwkv_kernel, n_lat_chunks=n_lat_chunks, n_chunks=n_chunks),
        grid=(bsz // nb, n_chunks),
        in_specs=ins + [_const_spec(c.shape) for c in consts],
        out_specs=[f_out] * 3 + [b_out] * 2,
        out_shape=[out_sds] * 5,
        scratch_shapes=[pltpu.VMEM((nb, 2, G, G), F32)],
        compiler_params=_params("arbitrary", "arbitrary"),
        name="rwkv_scan",
    )(p3d, p3d, p3d, p3d, p3d, p3d, *consts)
    return [a.reshape(bsz * t, G) for a in (yf, yb, bf, bb, gate)]


def _neg_expm1(x):
    series = -x * (1.0 + x * (1 / 2 + x * (1 / 6 + x * (1 / 24 + x * (1 / 120 + x * (1 / 720 + x / 5040))))))
    return jnp.where(x > -0.05, series, 1.0 - jnp.exp(x))


def _lru_chunk(p, prev8, next8, cw_ref, cb_ref, w_ref, bias_ref, lam_ref, h_ref, d, reverse):
    G = GROUP_W
    x = p[:, :G]
    u = (cw_ref[0:1, :] * _shift_rows(x, prev8, next8, 2) + cw_ref[1:2, :] * _shift_rows(x, prev8, next8, 1)
         + cw_ref[2:3, :] * x + cw_ref[3:4, :] * _shift_rows(x, prev8, next8, -1)) + cb_ref[...]
    pre = _dot(u, w_ref[d]) + bias_ref[d:d + 1, :]
    r = jax.nn.sigmoid(pre[:, :G])
    i = jax.nn.sigmoid(pre[:, G:])
    log_a = -LRU_C * r * jax.nn.softplus(-lam_ref[d:d + 1, :])
    a = jnp.exp(log_a)
    b = jnp.sqrt(_neg_expm1(2.0 * log_a)) * (i * u)
    row = lax.broadcasted_iota(jnp.int32, (CHUNK, 1), 0)
    k = 1
    while k < CHUNK:
        shift = CHUNK - k if reverse else k
        valid = (row < CHUNK - k) if reverse else (row >= k)
        b = jnp.where(valid, a * pltpu.roll(b, shift, 0) + b, b)
        a = jnp.where(valid, a * pltpu.roll(a, shift, 0), a)
        k *= 2
    h = b + a * h_ref[0:1, :]
    h_ref[0:1, :] = h[0:1, :] if reverse else h[CHUNK - 1:CHUNK, :]
    return h


def _lru_kernel(pf_ref, pfp_ref, pfn_ref, pb_ref, pbp_ref, pbn_ref, cw_ref, cb_ref, w_ref, bias_ref, lam_ref,
                hf_ref, hb_ref, st_ref, *, n_lat_chunks, n_chunks):
    j = pl.program_id(1)

    @pl.when(j == 0)
    def _():
        st_ref[...] = jnp.zeros_like(st_ref)

    G = GROUP_W
    for i in range(pf_ref.shape[0]):
        first, last = _seq_edges(_fwd_chunk(j, n_lat_chunks, n_chunks), n_lat_chunks, n_chunks)
        prev8 = jnp.where(first, 0.0, pfp_ref[i, :, :G])
        next8 = jnp.where(last, 0.0, pfn_ref[i, :, :G])
        hf_ref[i] = _lru_chunk(pf_ref[i], prev8, next8, cw_ref, cb_ref, w_ref, bias_ref, lam_ref,
                               st_ref.at[i, 0], 0, False)
        first, last = _seq_edges(_bwd_chunk(j, n_lat_chunks, n_chunks), n_lat_chunks, n_chunks)
        prev8 = jnp.where(first, 0.0, pbp_ref[i, :, :G])
        next8 = jnp.where(last, 0.0, pbn_ref[i, :, :G])
        hb_ref[i] = _lru_chunk(pb_ref[i], prev8, next8, cw_ref, cb_ref, w_ref, bias_ref, lam_ref,
                               st_ref.at[i, 1], 1, True)


def rglru_mixer(p3d, n_lat, conv_w, conv_b, w_a, b_a, w_x, b_x, lam):
    bsz, t, width = p3d.shape
    G = GROUP_W
    n_chunks = t // CHUNK
    n_lat_chunks = n_lat // CHUNK
    w = jnp.zeros((2, G, 2 * G), F32)
    for h in range(N_HEADS):
        sl = slice(h * HEAD_DIM, (h + 1) * HEAD_DIM)
        w = w.at[:, sl, sl].set(w_a[:, h])
        w = w.at[:, sl, G + h * HEAD_DIM:G + (h + 1) * HEAD_DIM].set(w_x[:, h])
    consts = [jnp.pad(conv_w, ((0, SUBLANE - CONV_W), (0, 0))), conv_b.reshape(1, G), w,
              jnp.concatenate([b_a, b_x], axis=1), lam]
    nb = math.gcd(SCAN_SAMPLES, bsz)
    ins, f_out, b_out = _natural_scan_specs(nb, width, G, n_lat_chunks, n_chunks, t // SUBLANE - 1)
    h_sds = jax.ShapeDtypeStruct((bsz, t, G), F32)
    hf, hb = pl.pallas_call(
        functools.partial(_lru_kernel, n_lat_chunks=n_lat_chunks, n_chunks=n_chunks),
        grid=(bsz // nb, n_chunks),
        in_specs=ins + [_const_spec(c.shape) for c in consts],
        out_specs=[f_out, b_out],
        out_shape=[h_sds, h_sds],
        scratch_shapes=[pltpu.VMEM((nb, 2, SUBLANE, G), F32)],
        compiler_params=_params("arbitrary", "arbitrary"),
        name="lru_scan",
    )(p3d, p3d, p3d, p3d, p3d, p3d, *consts)
    return hf.reshape(bsz * t, G), hb.reshape(bsz * t, G)


def _gdn_prepare(p, prev8, next8, cw_ref, al_ref, dt_ref, d, reverse):
    G = GROUP_W
    x = p[:, :3 * G]
    conv = (cw_ref[0:1, :] * _shift_rows(x, prev8, next8, 2) + cw_ref[1:2, :] * _shift_rows(x, prev8, next8, 1)
            + cw_ref[2:3, :] * x + cw_ref[3:4, :] * _shift_rows(x, prev8, next8, -1))
    qkv = conv * jax.nn.sigmoid(conv)
    gates = p[:, 4 * G:4 * G + LANE]
    bd_mask = _same_head((G, G))
    head_ones = jnp.where(bd_mask, 1.0, 0.0).astype(F32)
    q, k, v = qkv[:, :G], qkv[:, G:2 * G], qkv[:, 2 * G:]
    q = q * lax.rsqrt(_dot_sel(q * q, head_ones) + EPS) * (HEAD_DIM ** -0.5)
    k = k * lax.rsqrt(_dot_sel(k * k, head_ones) + EPS)
    la = -al_ref[d:d + 1, :] * jax.nn.softplus(_dot_sel(gates, _head_lane_expander(d * N_HEADS)) + dt_ref[d:d + 1, :])
    beta = jax.nn.sigmoid(_dot_sel(gates, _head_lane_expander((2 + d) * N_HEADS)))
    gam = _sel_dot(_cum_matrix(reverse), la)
    strict, incl, diag = _token_masks(reverse)
    gam_row = jnp.sum(jnp.where(diag, gam, 0.0), axis=0, keepdims=True)
    decay = jnp.exp(jnp.where(incl, gam - gam_row, -jnp.inf))
    k_beta = k * beta
    k_bd = _to_block_diag(k, bd_mask)
    lower = jnp.where(strict, _dot_nt(k_beta, k_bd) * decay, 0.0)
    e_gam = jnp.exp(gam)
    gam_last = gam[0:1, :] if reverse else gam[CHUNK - 1:CHUNK, :]
    return dict(lower=lower, vb=v * beta, kbe=k_beta * e_gam, qk=_dot_nt(q, k_bd) * decay, q_dec=q * e_gam,
                k_dec=k * jnp.exp(gam_last - gam), g_last=jnp.exp(gam_last))


def _gdn_kernel(pf_ref, pfp_ref, pfn_ref, pb_ref, pbp_ref, pbn_ref, cw_ref, al_ref, dt_ref, s0_ref,
                of_ref, ob_ref, s_ref, *, n_steps):
    j = pl.program_id(1)

    @pl.when(j == 0)
    def _():
        s_ref[...] = s0_ref[...]

    width = 3 * GROUP_W
    bd_mask = _same_head((GROUP_W, GROUP_W))
    cs, ss = [], []
    for i in range(pf_ref.shape[0]):
        prev8 = jnp.where(j == 0, 0.0, pfp_ref[i, :, :width])
        next8 = jnp.where(j == n_steps - 1, 0.0, pfn_ref[i, :, :width])
        cs.append(_gdn_prepare(pf_ref[i], prev8, next8, cw_ref, al_ref, dt_ref, 0, False))
        prev8 = jnp.where(j == n_steps - 1, 0.0, pbp_ref[i, :, :width])
        next8 = jnp.where(j == 0, 0.0, pbn_ref[i, :, :width])
        cs.append(_gdn_prepare(pb_ref[i], prev8, next8, cw_ref, al_ref, dt_ref, 1, True))
        ss += [s_ref[i, 0], s_ref[i, 1]]
    tinv = _unit_tri_inverse([c['lower'] for c in cs], bd_mask)
    us = [_dot(t, _to_block_diag(c['vb'], bd_mask)) for t, c in zip(tinv, cs)]
    ws = [_dot(t, _to_block_diag(c['kbe'], bd_mask)) for t, c in zip(tinv, cs)]
    vn = [u - _dot(w, s) for u, w, s in zip(us, ws, ss)]
    outs = [_dot(c['q_dec'], s) + _dot(c['qk'], _to_block_diag(v, bd_mask)) for c, s, v in zip(cs, ss, vn)]
    sn = [c['g_last'] * s + jnp.where(bd_mask, _dot_tn(c['k_dec'], v), 0.0) for c, s, v in zip(cs, ss, vn)]
    for i in range(pf_ref.shape[0]):
        of_ref[i] = outs[2 * i]
        ob_ref[i] = outs[2 * i + 1]
        s_ref[i, 0] = sn[2 * i]
        s_ref[i, 1] = sn[2 * i + 1]


def gdn_scan(p, n_steps, first_row_block, column_view, last8, cw, al, dt, s0):
    bsz = p.shape[0]
    G = GROUP_W
    width = _round_up(D_COLS, LANE)
    nb = math.gcd(SCAN_SAMPLES, bsz)
    f_in, b_in = _chunk_specs(nb, width, n_steps, first_row_block, column_view)
    fp, fn, bp, bn = _halo_specs(nb, width, n_steps, first_row_block, column_view, last8)
    f_out, b_out = _chunk_specs(nb, G, n_steps, 0, column_view)
    h_sds = jax.ShapeDtypeStruct((bsz, CHUNK, GRID_W * G) if column_view else (bsz, n_steps * CHUNK, G), F32)
    state_spec = pl.BlockSpec((nb, 2, G, G), lambda b, j: (b, 0, 0, 0))
    return pl.pallas_call(
        functools.partial(_gdn_kernel, n_steps=n_steps),
        grid=(bsz // nb, n_steps),
        in_specs=[f_in, fp, fn, b_in, bp, bn, _const_spec(cw.shape), _const_spec(al.shape), _const_spec(dt.shape),
                  state_spec],
        out_specs=[f_out, b_out, state_spec],
        out_shape=[h_sds, h_sds, jax.ShapeDtypeStruct(s0.shape, F32)],
        compiler_params=_params("arbitrary", "arbitrary"),
        name="gdn_scan",
    )(p, p, p, p, p, p, cw, al, dt, s0)


def gdn_mixer(p3d, n_lat, conv_w, a_log, dt_bias):
    bsz, t, width = p3d.shape
    G = GROUP_W
    n_ctx = t - n_lat
    assert n_lat == CHUNK * GRID_W
    al = jnp.repeat(jnp.exp(a_log), HEAD_DIM, axis=1)
    dt = jnp.repeat(dt_bias, HEAD_DIM, axis=1)
    cw = jnp.pad(conv_w, ((0, SUBLANE - CONV_W), (0, 0)))
    s0 = jnp.zeros((bsz, 2, G, G), F32)
    last8 = t // SUBLANE - 1
    of_c, ob_c, s1 = gdn_scan(p3d, n_ctx // CHUNK, n_lat // CHUNK, False, last8, cw, al, dt, s0)
    of_l, ob_l, _ = gdn_scan(p3d.reshape(bsz, t // GRID_W, GRID_W * width), GRID_W, 0, True, last8, cw, al, dt, s1)
    return (of_l.reshape(bsz * n_lat, G), ob_l.reshape(bsz * n_lat, G),
            of_c.reshape(bsz * n_ctx, G), ob_c.reshape(bsz * n_ctx, G))


def _head_max(x):
    lane_head = lax.broadcasted_iota(jnp.int32, x.shape, 1) // HEAD_DIM
    out = jnp.zeros_like(x)
    for h in range(N_HEADS):
        mh = jnp.max(jnp.where(lane_head == h, x, -jnp.inf), axis=-1, keepdims=True)
        out = jnp.where(lane_head == h, mh, out)
    return out


def _mlstm_prepare(p, ib_ref, fb_ref, d, reverse):
    G = GROUP_W
    q, k, v = p[:, :G], p[:, G:2 * G] * (HEAD_DIM ** -0.5), p[:, 2 * G:3 * G]
    gates = p[:, 4 * G:4 * G + LANE]
    bd_mask = _same_head((G, G))
    li = _dot_sel(gates, _head_lane_expander(d * N_HEADS)) + ib_ref[d:d + 1, :]
    lf = jax.nn.log_sigmoid(_dot_sel(gates, _head_lane_expander((2 + d) * N_HEADS)) + fb_ref[d:d + 1, :])
    b = _sel_dot(_cum_matrix(reverse), lf)
    _, incl, diag = _token_masks(reverse)
    g = li - b
    g_row = jnp.sum(jnp.where(diag, g, 0.0), axis=0, keepdims=True)
    log_d = jnp.where(incl, b + g_row, -jnp.inf)
    b_last = b[0:1, :] if reverse else b[CHUNK - 1:CHUNK, :]
    log_end = b_last + g
    return dict(q=q, k=k, v=v, b=b, log_d=log_d, m_intra=_head_max(log_d), b_last=b_last, log_end=log_end,
                end_max=jnp.max(log_end, axis=0, keepdims=True), qk=_dot_nt(q, _to_block_diag(k, bd_mask)))


def _mlstm_kernel(pf_ref, pb_ref, ib_ref, fb_ref, c0_ref, nm0_ref, hf_ref, hb_ref, c_ref, nm_ref):
    @pl.when(pl.program_id(1) == 0)
    def _():
        c_ref[...] = c0_ref[...]
        nm_ref[...] = nm0_ref[...]

    nb = pf_ref.shape[0]
    bd_mask = _same_head((GROUP_W, GROUP_W))
    head_ones = jnp.where(bd_mask, 1.0, 0.0).astype(F32)
    idx = [(i, d) for i in range(nb) for d in range(2)]
    cs = [_mlstm_prepare((pb_ref if d else pf_ref)[i], ib_ref, fb_ref, d, bool(d)) for i, d in idx]
    mems = [c_ref[i, d] for i, d in idx]
    nrms = [nm_ref[i, d, 0:1, :] for i, d in idx]
    m_prevs = [nm_ref[i, d, 1:2, :] for i, d in idx]
    m_inters = [c['b'] + m for c, m in zip(cs, m_prevs)]
    m_ts = [jnp.maximum(mi, c['m_intra']) for mi, c in zip(m_inters, cs)]
    w_inters = [jnp.exp(mi - mt) for mi, mt in zip(m_inters, m_ts)]
    pms = [jnp.exp(c['log_d'] - mt) * c['qk'] for c, mt in zip(cs, m_ts)]
    inter = [_dot_nt(c['q'], mem) for c, mem in zip(cs, mems)]
    nums = [_dot(pm, _to_block_diag(c['v'], bd_mask)) + w * it for pm, c, w, it in zip(pms, cs, w_inters, inter)]
    dens = [_dot_sel(pm + w * (c['q'] * nrm), head_ones) for pm, w, c, nrm in zip(pms, w_inters, cs, nrms)]
    hs = [num / jnp.maximum(jnp.abs(den), jnp.exp(-mt)) for num, den, mt in zip(nums, dens, m_ts)]
    m_news = [jnp.maximum(c['b_last'] + m, c['end_max']) for c, m in zip(cs, m_prevs)]
    w_prevs = [jnp.exp(c['b_last'] + m - mn) for c, m, mn in zip(cs, m_prevs, m_news)]
    w_news = [jnp.exp(c['log_end'] - mn) for c, mn in zip(cs, m_news)]
    upds = [_dot_tn(wn * c['v'], c['k']) for wn, c in zip(w_news, cs)]
    for n_, (i, d) in enumerate(idx):
        (hb_ref if d else hf_ref)[i] = hs[n_]
        c_ref[i, d] = w_prevs[n_] * mems[n_] + jnp.where(bd_mask, upds[n_], 0.0)
        nm_ref[i, d, 0:1, :] = w_prevs[n_] * nrms[n_] + jnp.sum(w_news[n_] * cs[n_]['k'], axis=0, keepdims=True)
        nm_ref[i, d, 1:2, :] = m_news[n_]


def mlstm_scan(p, n_steps, first_row_block, column_view, ib, fb, c0, nm0):
    bsz = p.shape[0]
    G = GROUP_W
    width = _round_up(C_COLS, LANE)
    nb = math.gcd(SCAN_SAMPLES, bsz)
    f_in, b_in = _chunk_specs(nb, width, n_steps, first_row_block, column_view)
    f_out, b_out = _chunk_specs(nb, G, n_steps, 0, column_view)
    h_sds = jax.ShapeDtypeStruct((bsz, CHUNK, GRID_W * G) if column_view else (bsz, n_steps * CHUNK, G), F32)
    state_specs = [pl.BlockSpec((nb, 2, G, G), lambda b, j: (b, 0, 0, 0)),
                   pl.BlockSpec((nb, 2, SUBLANE, G), lambda b, j: (b, 0, 0, 0))]
    return pl.pallas_call(
        _mlstm_kernel,
        grid=(bsz // nb, n_steps),
        in_specs=[f_in, b_in, _const_spec(ib.shape), _const_spec(fb.shape)] + state_specs,
        out_specs=[f_out, b_out] + state_specs,
        out_shape=[h_sds, h_sds, jax.ShapeDtypeStruct(c0.shape, F32), jax.ShapeDtypeStruct(nm0.shape, F32)],
        compiler_params=_params("arbitrary", "arbitrary"),
        name="mlstm_scan",
    )(p, p, ib, fb, c0, nm0)


def mlstm_mixer(p3d, n_lat, i_b, f_b):
    bsz, t, width = p3d.shape
    G = GROUP_W
    n_ctx = t - n_lat
    assert n_lat == CHUNK * GRID_W
    ib = jnp.repeat(i_b, HEAD_DIM, axis=1)
    fb = jnp.repeat(f_b, HEAD_DIM, axis=1)
    c0 = jnp.zeros((bsz, 2, G, G), F32)
    nm0 = jnp.zeros((bsz, 2, SUBLANE, G), F32)
    hf_c, hb_c, c1, nm1 = mlstm_scan(p3d, n_ctx // CHUNK, n_lat // CHUNK, False, ib, fb, c0, nm0)
    hf_l, hb_l, _, _ = mlstm_scan(p3d.reshape(bsz, t // GRID_W, GRID_W * width), GRID_W, 0, True, ib, fb, c1, nm1)
    return (hf_l.reshape(bsz * n_lat, G), hb_l.reshape(bsz * n_lat, G),
            hf_c.reshape(bsz * n_ctx, G), hb_c.reshape(bsz * n_ctx, G))


def kernel(x, c, ctx, c_ctx, mod_w, mod_b, norm_mix_g, norm_ffn_g, w_in, w_out, lru_conv_w, lru_conv_b, lru_w_a, lru_b_a, lru_w_x, lru_b_x, lru_lambda, rwkv_mu, rwkv_w_up, rwkv_w0, rwkv_a_up, rwkv_a0, rwkv_g_up, rwkv_k_k, rwkv_k_a, rwkv_r_k, rwkv_ln_g, rwkv_ln_b, mlstm_i_b, mlstm_f_b, mlstm_norm_g, gdn_conv_w, gdn_a_log, gdn_dt_bias, gdn_norm_g, ffn_w_gate, ffn_w_up, ffn_w_down, moe_router, moe_w_gate, moe_w_up, moe_w_down, final_norm_g):
    bsz, seq, d = x.shape
    n_ctx = ctx.shape[1]
    t = n_ctx + seq
    n = bsz * t
    tiles_per_seq = t // ROW_TILE
    assert n_ctx == ROW_TILE and seq % ROW_TILE == 0
    x2 = jnp.concatenate([x, ctx], axis=1).reshape(n, d)
    mod = jnp.einsum('bd,ldk->lbk', jax.nn.silu(c), mod_w) + mod_b[:, None]
    mod_c = jnp.einsum('d,ldk->lk', jax.nn.silu(c_ctx), mod_w) + mod_b
    mods = jnp.stack([mod, jnp.broadcast_to(mod_c[:, None], mod.shape)], axis=2).reshape(DEPTH, 2 * bsz, 6, 1, d)
    w_groups = [jnp.pad(w_in[:, :, lo:hi], ((0, 0), (0, 0), (0, _round_up(hi - lo, LANE) - (hi - lo)))).astype(BF16)
                for lo, hi in ((0, OFF_B), (OFF_B, OFF_C), (OFF_C, OFF_D), (OFF_D, IN_COLS))]
    w_out_bf = w_out.astype(BF16)
    ffn_bf = [w.astype(BF16) for w in (ffn_w_gate, ffn_w_up, ffn_w_down)]
    moe_bf = [w.astype(BF16) for w in (moe_w_gate, moe_w_up, moe_w_down)]
    router_pad = jnp.pad(moe_router, ((0, 0), (0, 0), (0, LANE - N_EXPERTS)))
    for l in range(DEPTH):
        sh1, sc1, gt1, sh2, sc2, gt2 = [mods[l, :, j] for j in range(6)]
        pa, pb, pc, pd = norm_proj(x2, norm_mix_g[l], sc1, sh1, w_groups, l, tiles_per_seq)
        lru = rglru_mixer(pa.reshape(bsz, t, -1), seq, lru_conv_w[l], lru_conv_b[l], lru_w_a[l], lru_b_a[l],
                          lru_w_x[l], lru_b_x[l], lru_lambda[l])
        rwkv = rwkv7_mixer(pb.reshape(bsz, t, -1), seq, rwkv_mu[l], rwkv_w_up[l], rwkv_w0[l], rwkv_a_up[l],
                           rwkv_a0[l], rwkv_g_up[l], rwkv_k_k[l], rwkv_k_a[l], rwkv_r_k[l])
        mlstm = mlstm_mixer(pc.reshape(bsz, t, -1), seq, mlstm_i_b[l], mlstm_f_b[l])
        gdn = gdn_mixer(pd.reshape(bsz, t, -1), seq, gdn_conv_w[l], gdn_a_log[l], gdn_dt_bias[l])
        x2 = out_proj(lru, rwkv, mlstm, gdn, pa, pc, pd, rwkv_ln_g[l], rwkv_ln_b[l], mlstm_norm_g[l],
                      gdn_norm_g[l], w_out_bf, l, x2, gt1, tiles_per_seq)
        if l % 2 == 0:
            x2 = dense_ffn(x2, norm_ffn_g[l], sc2, sh2, gt2, *ffn_bf, l // 2, tiles_per_seq)
        else:
            h2, lg = norm_router(x2, norm_ffn_g[l], sc2, sh2, router_pad, l // 2, tiles_per_seq)
            y = moe_ffn(h2, lg[:, :N_EXPERTS], *moe_bf, l // 2)
            gt_rows = jnp.concatenate([jnp.broadcast_to(gt2.reshape(bsz, 2, 1, d)[:, 0], (bsz, seq, d)),
                                       jnp.broadcast_to(gt2.reshape(bsz, 2, 1, d)[:, 1], (bsz, n_ctx, d))], axis=1)
            x2 = x2 + gt_rows.reshape(n, d) * y
    return final_norm(x2.reshape(bsz, t, d), final_norm_g, seq)
```

```python
import functools
import math

import jax
import jax.numpy as jnp
from jax import lax
from jax.experimental import pallas as pl
from jax.experimental.pallas import tpu as pltpu

D_MODEL = 1024
DEPTH = 4
GRID_W = 64
N_MIXERS = 4
GROUP_W = D_MODEL // N_MIXERS
HEAD_DIM = 64
N_HEADS = GROUP_W // HEAD_DIM
CONV_W = 4
LRU_C = 8.0
RWKV_LORA_W = 32
RWKV_LORA_A = 32
RWKV_LORA_G = 64
RWKV_GN_EPS = 64e-5
CHUNK = 64
D_FF = 2816
N_EXPERTS = 8
TOP_K = 2
D_FF_EXPERT = 1408
MOE_BLOCK = 256
EPS = 1e-6

A_COLS = 2 * GROUP_W
B_COLS = 3 * GROUP_W + 2 * RWKV_LORA_W + 2 * RWKV_LORA_A + RWKV_LORA_G
C_COLS = 4 * GROUP_W + 4 * N_HEADS
D_COLS = 4 * GROUP_W + 4 * N_HEADS
OFF_B = A_COLS
OFF_C = OFF_B + B_COLS
OFF_D = OFF_C + C_COLS
IN_COLS = OFF_D + D_COLS

LANE = 128
SUBLANE = 8
VMEM_LIMIT = 60 * 1024 * 1024
ROW_TILE = 256
SCAN_SAMPLES = 4
F32 = jnp.float32
BF16 = jnp.bfloat16
HI = lax.Precision.HIGHEST


def _round_up(n, m):
    return (n + m - 1) // m * m


def _pad_cols(w, n):
    return jnp.pad(w, ((0, 0), (0, n - w.shape[1])))


def _const_spec(shape):
    nd = len(shape)
    return pl.BlockSpec(shape, lambda *_: (0,) * nd, pipeline_mode=pl.Buffered(1))


def _params(*sem):
    return pltpu.CompilerParams(dimension_semantics=sem, vmem_limit_bytes=VMEM_LIMIT)


def _mod_spec(d, tiles_per_seq):
    def index(i):
        return (2 * (i // tiles_per_seq) + (i % tiles_per_seq) // (tiles_per_seq - 1), 0, 0)
    return pl.BlockSpec((1, 1, d), index)


def _norm_mod(x, g, sc, sh):
    ms = jnp.mean(x * x, axis=-1, keepdims=True)
    return (x * lax.rsqrt(ms + EPS) * g) * (1.0 + sc) + sh


def _dot_dims(a, b, dims, prec):
    if prec is None:
        a, b, prec = a.astype(BF16), b.astype(BF16), None
    return lax.dot_general(a, b, (dims, ((), ())), preferred_element_type=F32, precision=prec)


def _dot(a, b, prec=None):
    return _dot_dims(a, b, ((1,), (0,)), prec)


def _dot_nt(a, b, prec=None):
    return _dot_dims(a, b, ((1,), (1,)), prec)


def _dot_tn(a, b, prec=None):
    return _dot_dims(a, b, ((0,), (0,)), prec)


def _split3(x):
    x1 = x.astype(BF16)
    r = x - x1.astype(F32)
    x2 = r.astype(BF16)
    return x1, x2, (r - x2.astype(F32)).astype(BF16)


def _dot_sel(a, sel):
    s = sel.astype(BF16)
    p1, p2, p3 = [lax.dot_general(p, s, (((1,), (0,)), ((), ())), preferred_element_type=F32) for p in _split3(a)]
    return p1 + p2 + p3


def _sel_dot(sel, b):
    s = sel.astype(BF16)
    p1, p2, p3 = [lax.dot_general(s, p, (((1,), (0,)), ((), ())), preferred_element_type=F32) for p in _split3(b)]
    return p1 + p2 + p3


def _norm_proj_kernel(x_ref, g_ref, sc_ref, sh_ref, *refs, n_out):
    h = _norm_mod(x_ref[...], g_ref[...], sc_ref[0], sh_ref[0]).astype(BF16)
    for w_ref, o_ref in zip(refs[:n_out], refs[n_out:]):
        o_ref[...] = _dot(h, w_ref[...])


def _layer_spec(w, layer):
    return pl.BlockSpec((None,) + w.shape[1:], lambda *_: (layer, 0, 0), pipeline_mode=pl.Buffered(1))


def norm_proj(x2d, g, sc, sh, ws, layer, tiles_per_seq):
    n, d = x2d.shape
    tm = ROW_TILE
    mod_spec = _mod_spec(d, tiles_per_seq)
    return pl.pallas_call(
        functools.partial(_norm_proj_kernel, n_out=len(ws)),
        grid=(n // tm,),
        in_specs=[pl.BlockSpec((tm, d), lambda i: (i, 0)), _const_spec((1, d)), mod_spec, mod_spec]
        + [_layer_spec(w, layer) for w in ws],
        out_specs=[pl.BlockSpec((tm, w.shape[2]), lambda i: (i, 0)) for w in ws],
        out_shape=[jax.ShapeDtypeStruct((n, w.shape[2]), F32) for w in ws],
        compiler_params=_params("arbitrary"),
        name="norm_proj",
    )(x2d, g.reshape(1, d), sc, sh, *ws)


def _out_proj_kernel(a_hf, a_hb, a_gate, b_yf, b_yb, b_bf, b_bb, b_g, c_fl, c_bl, c_fc, c_bc, c_gate,
                     d_fl, d_bl, d_fc, d_bc, d_gate, lng_ref, lnb_ref, cg_ref, dg_ref, w_ref, x_ref, gt_ref,
                     o_ref, *, tiles_per_seq):
    is_ctx = pl.program_id(0) % tiles_per_seq == tiles_per_seq - 1
    seg_mean = jnp.where(_same_head((GROUP_W, GROUP_W)), 1.0 / HEAD_DIM, 0.0).astype(F32)
    ya = (a_hf[...] + a_hb[...]) * jax.nn.gelu(a_gate[...])
    yb = b_yf[...] + b_yb[...]
    mean = _dot_sel(yb, seg_mean)
    var = _dot_sel(jnp.square(yb - mean), seg_mean)
    yb = (yb - mean) * lax.rsqrt(var + RWKV_GN_EPS) * lng_ref[...] + lnb_ref[...]
    yb = (yb + b_bf[...] + b_bb[...]) * b_g[...]
    hc = jnp.where(is_ctx, c_fc[...] + c_bc[...], c_fl[...] + c_bl[...])
    yc = hc * lax.rsqrt(_dot_sel(hc * hc, seg_mean) + EPS) * cg_ref[...] * jax.nn.sigmoid(c_gate[...])
    hd = jnp.where(is_ctx, d_fc[...] + d_bc[...], d_fl[...] + d_bl[...])
    gate = d_gate[...]
    yd = hd * lax.rsqrt(_dot_sel(hd * hd, seg_mean) + EPS) * dg_ref[...] * (gate * jax.nn.sigmoid(gate))
    y = jnp.concatenate([ya, yb, yc, yd], axis=-1).astype(BF16)
    o_ref[...] = x_ref[...] + gt_ref[0] * _dot(y, w_ref[...])


def out_proj(lru, rwkv, mlstm, gdn, pa, pc, pd, ln_g, ln_b, c_g, d_g, w, layer, x2d, gt, tiles_per_seq):
    n, d = x2d.shape
    G = GROUP_W
    tm = ROW_TILE
    lat_tiles = tiles_per_seq - 1
    row = pl.BlockSpec((tm, G), lambda i: (i, 0))
    lat = pl.BlockSpec((tm, G), lambda i: ((i // tiles_per_seq) * lat_tiles
                                             + jnp.minimum(i % tiles_per_seq, lat_tiles - 1), 0))
    ctx = pl.BlockSpec((tm, G), lambda i: (i // tiles_per_seq, 0))
    col = lambda c: pl.BlockSpec((tm, G), lambda i: (i, c))
    vec = _const_spec((1, G))
    return pl.pallas_call(
        functools.partial(_out_proj_kernel, tiles_per_seq=tiles_per_seq),
        grid=(n // tm,),
        in_specs=[row, row, col(1)] + [row] * 5 + [lat, lat, ctx, ctx, col(3)] * 2 + [vec] * 4
        + [_layer_spec(w, layer), pl.BlockSpec((tm, d), lambda i: (i, 0)), _mod_spec(d, tiles_per_seq)],
        out_specs=pl.BlockSpec((tm, d), lambda i: (i, 0)),
        out_shape=jax.ShapeDtypeStruct((n, d), F32),
        compiler_params=_params("arbitrary"),
        name="out_proj",
    )(*lru, pa, *rwkv, *mlstm, pc, *gdn, pd, ln_g.reshape(1, G), ln_b.reshape(1, G), c_g.reshape(1, G),
      d_g.reshape(1, G), w, x2d, gt)


def _swiglu_chunks(hb, wg_ref, wu_ref, wd_ref, n_chunks, fc):
    acc = None
    for c in range(n_chunks):
        a = _dot(hb, wg_ref[:, c * fc:(c + 1) * fc])
        u = _dot(hb, wu_ref[:, c * fc:(c + 1) * fc])
        z = (a * jax.nn.sigmoid(a) * u).astype(BF16)
        part = _dot(z, wd_ref[c * fc:(c + 1) * fc, :])
        acc = part if acc is None else acc + part
    return acc


def _dense_ffn_kernel(x_ref, g_ref, sc_ref, sh_ref, gt_ref, wg_ref, wu_ref, wd_ref, o_ref, *, n_chunks, fc):
    x = x_ref[...]
    hb = _norm_mod(x, g_ref[...], sc_ref[0], sh_ref[0]).astype(BF16)
    o_ref[...] = x + gt_ref[0] * _swiglu_chunks(hb, wg_ref, wu_ref, wd_ref, n_chunks, fc)


def dense_ffn(x2d, g, sc, sh, gt, wg, wu, wd, layer, tiles_per_seq):
    n, d = x2d.shape
    tm = ROW_TILE
    fc = D_FF // 2
    mod_spec = _mod_spec(d, tiles_per_seq)
    return pl.pallas_call(
        functools.partial(_dense_ffn_kernel, n_chunks=D_FF // fc, fc=fc),
        grid=(n // tm,),
        in_specs=[pl.BlockSpec((tm, d), lambda i: (i, 0)), _const_spec((1, d)), mod_spec, mod_spec, mod_spec,
                  _layer_spec(wg, layer), _layer_spec(wu, layer), _layer_spec(wd, layer)],
        out_specs=pl.BlockSpec((tm, d), lambda i: (i, 0)),
        out_shape=jax.ShapeDtypeStruct((n, d), F32),
        compiler_params=_params("arbitrary"),
        name="dense_ffn",
    )(x2d, g.reshape(1, d), sc, sh, gt, wg, wu, wd)


def _router_kernel(x_ref, g_ref, sc_ref, sh_ref, r_ref, h_ref, l_ref):
    h = _norm_mod(x_ref[...], g_ref[...], sc_ref[0], sh_ref[0])
    h_ref[...] = h.astype(BF16)
    l_ref[...] = _dot(h, r_ref[...], HI)


def norm_router(x2d, g, sc, sh, router_pad, layer, tiles_per_seq):
    n, d = x2d.shape
    tm = ROW_TILE
    mod_spec = _mod_spec(d, tiles_per_seq)
    return pl.pallas_call(
        _router_kernel,
        grid=(n // tm,),
        in_specs=[pl.BlockSpec((tm, d), lambda i: (i, 0)), _const_spec((1, d)), mod_spec, mod_spec,
                  _layer_spec(router_pad, layer)],
        out_specs=[pl.BlockSpec((tm, d), lambda i: (i, 0)), pl.BlockSpec((tm, LANE), lambda i: (i, 0))],
        out_shape=[jax.ShapeDtypeStruct((n, d), BF16), jax.ShapeDtypeStruct((n, LANE), F32)],
        compiler_params=_params("arbitrary"),
        name="norm_router",
    )(x2d, g.reshape(1, d), sc, sh, router_pad)


def _expert_kernel(be_ref, nb_ref, xb_ref, wg_ref, wu_ref, wd_ref, o_ref):
    i = pl.program_id(0)

    @pl.when(i < nb_ref[0])
    def _():
        o_ref[...] = _swiglu_chunks(xb_ref[...], wg_ref.at[0], wu_ref.at[0], wd_ref.at[0], 1, D_FF_EXPERT)

    @pl.when(i >= nb_ref[0])
    def _():
        o_ref[...] = jnp.zeros_like(o_ref)


def expert_ffn(xb, block_e, n_used, wg, wu, wd, layer, tm):
    npad, d = xb.shape
    f = wg.shape[3]
    grid_spec = pltpu.PrefetchScalarGridSpec(
        num_scalar_prefetch=2,
        grid=(npad // tm,),
        in_specs=[pl.BlockSpec((tm, d), lambda i, be, nb: (i, 0)),
                  pl.BlockSpec((None, 1, d, f), lambda i, be, nb: (layer, be[i], 0, 0)),
                  pl.BlockSpec((None, 1, d, f), lambda i, be, nb: (layer, be[i], 0, 0)),
                  pl.BlockSpec((None, 1, f, d), lambda i, be, nb: (layer, be[i], 0, 0))],
        out_specs=pl.BlockSpec((tm, d), lambda i, be, nb: (i, 0)),
    )
    return pl.pallas_call(
        _expert_kernel,
        grid_spec=grid_spec,
        out_shape=jax.ShapeDtypeStruct((npad, d), F32),
        compiler_params=_params("arbitrary"),
        name="expert_ffn",
    )(block_e, n_used, xb, wg, wu, wd)


def moe_ffn(h_bf, logits, wg, wu, wd, layer):
    t, d = h_bf.shape
    top_logits, top_idx = lax.top_k(logits, TOP_K)
    gate = jax.nn.softmax(top_logits, axis=-1)
    tk = t * TOP_K
    onehot = (top_idx.reshape(1, tk) == jnp.arange(N_EXPERTS, dtype=jnp.int32)[:, None]).astype(jnp.int32)
    csum = jnp.cumsum(onehot, axis=1)
    counts = csum[:, -1]
    padded = (counts + MOE_BLOCK - 1) // MOE_BLOCK * MOE_BLOCK
    pad_end = jnp.cumsum(padded)
    pad_start = pad_end - padded
    pos = jnp.sum(onehot * (csum - 1 + pad_start[:, None]), axis=0)
    n_blocks = -(-tk // MOE_BLOCK) + N_EXPERTS
    buf_tok = jnp.zeros((n_blocks * MOE_BLOCK,), jnp.int32).at[pos].set(jnp.arange(tk, dtype=jnp.int32) // TOP_K)
    slot_of = pos.reshape(t, TOP_K)
    block_e = jnp.clip(jnp.searchsorted(pad_end, jnp.arange(n_blocks, dtype=jnp.int32) * MOE_BLOCK,
                                        side='right'), 0, N_EXPERTS - 1).astype(jnp.int32)
    n_used = (pad_end[-1] // MOE_BLOCK).astype(jnp.int32).reshape(1)
    yb = expert_ffn(h_bf[buf_tok], block_e, n_used, wg, wu, wd, layer, MOE_BLOCK)
    return gate[:, 0:1] * yb[slot_of[:, 0]] + gate[:, 1:2] * yb[slot_of[:, 1]]


def _final_norm_kernel(x_ref, g_ref, o_ref):
    x = x_ref[...]
    ms = jnp.mean(x * x, axis=-1, keepdims=True)
    o_ref[...] = x * lax.rsqrt(ms + EPS) * g_ref[...]


def final_norm(x3d, g, n_lat):
    bsz, t, d = x3d.shape
    tm = ROW_TILE
    return pl.pallas_call(
        _final_norm_kernel,
        grid=(bsz, n_lat // tm),
        in_specs=[pl.BlockSpec((None, tm, d), lambda b, i: (b, i, 0)), _const_spec((1, d))],
        out_specs=pl.BlockSpec((None, tm, d), lambda b, i: (b, i, 0)),
        out_shape=jax.ShapeDtypeStruct((bsz, n_lat, d), F32),
        compiler_params=_params("arbitrary", "arbitrary"),
        name="final_norm",
    )(x3d, g.reshape(1, d))


def _same_head(shape):
    r = lax.broadcasted_iota(jnp.int32, shape, 0) // HEAD_DIM
    c = lax.broadcasted_iota(jnp.int32, shape, 1) // HEAD_DIM
    return r == c


def _to_block_diag(x, mask):
    xb = x.astype(BF16)
    return jnp.where(mask, jnp.concatenate([xb] * N_HEADS, axis=0), jnp.zeros((), BF16))


def _token_masks(reverse):
    t = lax.broadcasted_iota(jnp.int32, (CHUNK, GROUP_W), 0)
    s = lax.broadcasted_iota(jnp.int32, (CHUNK, GROUP_W), 1) % HEAD_DIM
    if reverse:
        return s > t, s >= t, s == t
    return s < t, s <= t, s == t


def _cum_matrix(reverse):
    t = lax.broadcasted_iota(jnp.int32, (CHUNK, CHUNK), 0)
    s = lax.broadcasted_iota(jnp.int32, (CHUNK, CHUNK), 1)
    return jnp.where((s >= t) if reverse else (s <= t), 1.0, 0.0).astype(F32)


def _unit_tri_inverse(ls, bd_mask):
    t = lax.broadcasted_iota(jnp.int32, (CHUNK, GROUP_W), 0)
    s = lax.broadcasted_iota(jnp.int32, (CHUNK, GROUP_W), 1) % HEAD_DIM
    blk16 = (t // 16) == (s // 16)
    blk32 = (t // 32) == (s // 32)
    eye = jnp.where(t == s, 1.0, 0.0)
    ps = [jnp.where(blk16, -l, 0.0) for l in ls]
    xs = [eye + p for p in ps]
    for _ in range(3):
        ps = [_dot(p, _to_block_diag(p, bd_mask)) for p in ps]
        xs = [x + _dot(x, _to_block_diag(p, bd_mask)) for x, p in zip(xs, ps)]
    for off_diag in (jnp.logical_and(blk32, jnp.logical_not(blk16)), jnp.logical_not(blk32)):
        xe = [_dot(x, _to_block_diag(jnp.where(off_diag, l, 0.0), bd_mask)) for x, l in zip(xs, ls)]
        xs = [x - _dot(y, _to_block_diag(x, bd_mask)) for x, y in zip(xs, xe)]
    return xs


def _head_lane_expander(col0):
    r = lax.broadcasted_iota(jnp.int32, (LANE, GROUP_W), 0)
    c = lax.broadcasted_iota(jnp.int32, (LANE, GROUP_W), 1) // HEAD_DIM
    return jnp.where(r == col0 + c, 1.0, 0.0).astype(F32)


def _fwd_chunk(j, n_lat_chunks, n_chunks):
    n_ctx_chunks = n_chunks - n_lat_chunks
    return jnp.where(j < n_ctx_chunks, n_lat_chunks + j, j - n_ctx_chunks)


def _bwd_chunk(j, n_lat_chunks, n_chunks):
    return n_chunks - 1 - j


def _seq_edges(c, n_lat_chunks, n_chunks):
    first = jnp.logical_or(c == 0, c == n_lat_chunks)
    last = jnp.logical_or(c == n_lat_chunks - 1, c == n_chunks - 1)
    return first, last


def _shift_rows(x, prev8, next8, back):
    row = lax.broadcasted_iota(jnp.int32, (CHUNK, 1), 0)
    if back == -1:
        return jnp.where(row == CHUNK - 1, next8[0:1, :], pltpu.roll(x, CHUNK - 1, 0))
    out = pltpu.roll(x, back, 0)
    for i in range(back):
        out = jnp.where(row == i, prev8[SUBLANE - back + i:SUBLANE - back + i + 1, :], out)
    return out


def _natural_scan_specs(nb, width, out_width, n_lat_chunks, n_chunks, last8):
    r8 = CHUNK // SUBLANE
    fwd = functools.partial(_fwd_chunk, n_lat_chunks=n_lat_chunks, n_chunks=n_chunks)
    bwd = functools.partial(_bwd_chunk, n_lat_chunks=n_lat_chunks, n_chunks=n_chunks)
    chunk = lambda f, w: pl.BlockSpec((nb, CHUNK, w), lambda b, j: (b, f(j), 0))
    prev = lambda f: pl.BlockSpec((nb, SUBLANE, width), lambda b, j: (b, jnp.maximum(f(j) * r8 - 1, 0), 0))
    nxt = lambda f: pl.BlockSpec((nb, SUBLANE, width), lambda b, j: (b, jnp.minimum((f(j) + 1) * r8, last8), 0))
    ins = [chunk(fwd, width), prev(fwd), nxt(fwd), chunk(bwd, width), prev(bwd), nxt(bwd)]
    return ins, chunk(fwd, out_width), chunk(bwd, out_width)


def _chunk_specs(nb, width, n_steps, first_row_block, column_view, lane_block=0):
    if column_view:
        return (pl.BlockSpec((nb, CHUNK, SUBLANE, width), lambda b, j: (b, 0, j // SUBLANE, lane_block)),
                pl.BlockSpec((nb, CHUNK, SUBLANE, width),
                             lambda b, j: (b, 0, (n_steps - 1 - j) // SUBLANE, lane_block)))
    return (pl.BlockSpec((nb, CHUNK, width), lambda b, j: (b, first_row_block + j, lane_block)),
            pl.BlockSpec((nb, CHUNK, width), lambda b, j: (b, first_row_block + n_steps - 1 - j, lane_block)))


def _take_column(ref, i, col):
    if col is None:
        return ref[i]
    return lax.switch(col % SUBLANE, [functools.partial(lambda k: ref[i, :, k, :], k) for k in range(SUBLANE)])


def _put_column(ref, i, col, val):
    if col is None:
        ref[i] = val
        return

    def store(k):
        ref[i, :, k, :] = val

    lax.switch(col % SUBLANE, [functools.partial(store, k) for k in range(SUBLANE)])


def _halo_specs(nb, width, n_steps, first_row_block, column_view, last8):
    r8 = CHUNK // SUBLANE
    fwd = lambda j: j
    bwd = lambda j: n_steps - 1 - j
    if column_view:
        prev = lambda f: pl.BlockSpec((nb, SUBLANE, SUBLANE, width),
                                      lambda b, j: (b, r8 - 1, jnp.maximum(f(j) - 1, 0) // SUBLANE, 0))
        nxt = lambda f: pl.BlockSpec((nb, SUBLANE, SUBLANE, width),
                                     lambda b, j: (b, 0, jnp.minimum(f(j) + 1, n_steps - 1) // SUBLANE, 0))
    else:
        prev = lambda f: pl.BlockSpec((nb, SUBLANE, width),
                                      lambda b, j: (b, jnp.maximum((first_row_block + f(j)) * r8 - 1, 0), 0))
        nxt = lambda f: pl.BlockSpec((nb, SUBLANE, width),
                                     lambda b, j: (b, jnp.minimum((first_row_block + f(j) + 1) * r8, last8), 0))
    return prev(fwd), nxt(fwd), prev(bwd), nxt(bwd)


def _rwkv_prepare(p, prev_row, next_row, prm, d, reverse):
    (mu_ref, wup_ref, w0_ref, aup_ref, a0_ref, gup_ref, kk_ref, ka_ref, rk_ref) = prm
    G = GROUP_W
    row = lax.broadcasted_iota(jnp.int32, (CHUNK, 1), 0)
    prev = jnp.where(row == 0, prev_row, pltpu.roll(p, 1, 0))
    nxt = jnp.where(row == CHUNK - 1, next_row, pltpu.roll(p, CHUNK - 1, 0))
    ps = p + mu_ref[0:1, :] * (prev - p) + mu_ref[1:2, :] * (nxt - p)
    r, k, v = ps[:, :G], ps[:, G:2 * G], ps[:, 2 * G:3 * G]
    lora_in = ps[:, 3 * G:3 * G + LANE]
    gate_in = ps[:, 3 * G + LANE:3 * G + 2 * LANE]
    bd_mask = _same_head((G, G))
    head_ones = jnp.where(bd_mask, 1.0, 0.0).astype(F32)
    log_w = -math.exp(-0.5) * jax.nn.sigmoid(_dot(jnp.tanh(lora_in), wup_ref[d]) + w0_ref[d:d + 1, :])
    a = jax.nn.sigmoid(_dot(lora_in, aup_ref[d]) + a0_ref[d:d + 1, :])
    kr = k * kk_ref[...]
    kk = kr * lax.rsqrt(_dot_sel(kr * kr, head_ones) + EPS)
    k_mod = k * (1.0 + (a - 1.0) * ka_ref[...])
    bonus = _dot_sel(r * k_mod * rk_ref[...], head_ones) * v
    lg = _sel_dot(_cum_matrix(reverse), log_w)
    e_neg = jnp.exp(-lg)
    at = kk * jnp.exp(lg - log_w)
    bt = -(kk * a) * e_neg
    kt = k_mod * e_neg
    rt = r * jnp.exp(lg)
    g_last = jnp.exp(lg[0:1, :] if reverse else lg[CHUNK - 1:CHUNK, :])
    strict, incl, _ = _token_masks(reverse)
    gate = None if reverse else _dot(jax.nn.sigmoid(gate_in), gup_ref[...])
    return dict(at=at, rt=rt, bt=bt, kt=kt, v=v, g_last=g_last, bonus=bonus, gate=gate, strict=strict, incl=incl)


def _rwkv_kernel(pf_ref, pfp_ref, pfn_ref, pb_ref, pbp_ref, pbn_ref,
                 mu_ref, wup_ref, w0_ref, aup_ref, a0_ref, gup_ref, kk_ref, ka_ref, rk_ref,
                 yf_ref, bf_ref, g_ref, yb_ref, bb_ref, s_ref, *, n_lat_chunks, n_chunks):
    j = pl.program_id(1)

    @pl.when(j == 0)
    def _():
        s_ref[...] = jnp.zeros_like(s_ref)

    prm = (mu_ref, wup_ref, w0_ref, aup_ref, a0_ref, gup_ref, kk_ref, ka_ref, rk_ref)
    nb = pf_ref.shape[0]
    bd_mask = _same_head((GROUP_W, GROUP_W))
    idx = [(i, d) for i in range(nb) for d in range(2)]
    cs = []
    for i, d in idx:
        chunk = (_bwd_chunk if d else _fwd_chunk)(j, n_lat_chunks, n_chunks)
        first, last = _seq_edges(chunk, n_lat_chunks, n_chunks)
        p_ref, pp_ref, pn_ref = (pb_ref, pbp_ref, pbn_ref) if d else (pf_ref, pfp_ref, pfn_ref)
        prev_row = jnp.where(first, 0.0, pp_ref[i, SUBLANE - 1:SUBLANE, :])
        next_row = jnp.where(last, 0.0, pn_ref[i, 0:1, :])
        cs.append(_rwkv_prepare(p_ref[i], prev_row, next_row, prm, d, bool(d)))
    ss = [s_ref[i, d] for i, d in idx]
    lhs = [jnp.concatenate([c['at'], c['rt']], axis=0) for c in cs]
    m_b = [_dot_nt(l, _to_block_diag(c['bt'], bd_mask)) for l, c in zip(lhs, cs)]
    m_k = [_dot_nt(l, _to_block_diag(c['kt'], bd_mask)) for l, c in zip(lhs, cs)]
    a_ak = [jnp.where(c['strict'], m[:CHUNK], 0.0) for c, m in zip(cs, m_k)]
    a_rb = [jnp.where(c['incl'], m[CHUNK:], 0.0) for c, m in zip(cs, m_b)]
    a_rk = [jnp.where(c['incl'], m[CHUNK:], 0.0) for c, m in zip(cs, m_k)]
    tinv = _unit_tri_inverse([jnp.where(c['strict'], -m[:CHUNK], 0.0) for c, m in zip(cs, m_b)], bd_mask)
    v_bd = [_to_block_diag(c['v'], bd_mask) for c in cs]
    rhs = [_dot_nt(c['at'], s) + _dot(a, vb) for c, s, a, vb in zip(cs, ss, a_ak, v_bd)]
    us = [_dot(t, _to_block_diag(r, bd_mask)) for t, r in zip(tinv, rhs)]
    ys = [_dot_nt(c['rt'], s) + _dot(a, vb) for c, s, a, vb in zip(cs, ss, a_rk, v_bd)]
    ys = [y + _dot(a, _to_block_diag(u, bd_mask)) for y, a, u in zip(ys, a_rb, us)]
    upds = [_dot_tn(jnp.concatenate([u, c['v']], axis=0), jnp.concatenate([c['bt'], c['kt']], axis=0))
            for u, c in zip(us, cs)]
    for n_, (i, d) in enumerate(idx):
        (yb_ref if d else yf_ref)[i] = ys[n_]
        (bb_ref if d else bf_ref)[i] = cs[n_]['bonus']
        if not d:
            g_ref[i] = cs[n_]['gate']
        s_ref[i, d] = cs[n_]['g_last'] * (ss[n_] + jnp.where(bd_mask, upds[n_], 0.0))


def rwkv7_mixer(p3d, n_lat, mu, w_up, w0, a_up, a0, g_up, k_k, k_a, r_k):
    bsz, t, width = p3d.shape
    G = GROUP_W
    n_chunks = t // CHUNK
    n_lat_chunks = n_lat // CHUNK
    mu_pad = _pad_cols(mu, width)
    wup = jnp.zeros((2, LANE, G), F32)
    aup = jnp.zeros((2, LANE, G), F32)
    for d in range(2):
        wup = wup.at[d, d * RWKV_LORA_W:(d + 1) * RWKV_LORA_W].set(w_up[d])
        lo = 2 * RWKV_LORA_W + d * RWKV_LORA_A
        aup = aup.at[d, lo:lo + RWKV_LORA_A].set(a_up[d])
    gup = jnp.zeros((LANE, G), F32).at[:RWKV_LORA_G].set(g_up)
    consts = [mu_pad, wup, w0, aup, a0, gup, k_k.reshape(1, G), k_a.reshape(1, G), r_k.reshape(1, G)]
    nb = math.gcd(SCAN_SAMPLES, bsz)
    ins, f_out, b_out = _natural_scan_specs(nb, width, G, n_lat_chunks, n_chunks, t // SUBLANE - 1)
    out_sds = jax.ShapeDtypeStruct((bsz, t, G), F32)
    yf, bf, gate, yb, bb = pl.pallas_call(
        functools.partial(_rwkv_kernel, n_lat_chunks=n_lat_chunks, n_chunks=n_chunks),
        grid=(bsz // nb, n_chunks),
        in_specs=ins + [_const_spec(c.shape) for c in consts],
        out_specs=[f_out] * 3 + [b_out] * 2,
        out_shape=[out_sds] * 5,
        scratch_shapes=[pltpu.VMEM((nb, 2, G, G), F32)],
        compiler_params=_params("arbitrary", "arbitrary"),
        name="rwkv_scan",
    )(p3d, p3d, p3d, p3d, p3d, p3d, *consts)
    return [a.reshape(bsz * t, G) for a in (yf, yb, bf, bb, gate)]


def _neg_expm1(x):
    series = -x * (1.0 + x * (1 / 2 + x * (1 / 6 + x * (1 / 24 + x * (1 / 120 + x * (1 / 720 + x / 5040))))))
    return jnp.where(x > -0.05, series, 1.0 - jnp.exp(x))


def _lru_chunk(p, prev8, next8, cw_ref, cb_ref, w_ref, bias_ref, lam_ref, h_ref, d, reverse):
    G = GROUP_W
    x = p[:, :G]
    u = (cw_ref[0:1, :] * _shift_rows(x, prev8, next8, 2) + cw_ref[1:2, :] * _shift_rows(x, prev8, next8, 1)
         + cw_ref[2:3, :] * x + cw_ref[3:4, :] * _shift_rows(x, prev8, next8, -1)) + cb_ref[...]
    pre = _dot(u, w_ref[d]) + bias_ref[d:d + 1, :]
    r = jax.nn.sigmoid(pre[:, :G])
    i = jax.nn.sigmoid(pre[:, G:])
    log_a = -LRU_C * r * jax.nn.softplus(-lam_ref[d:d + 1, :])
    a = jnp.exp(log_a)
    b = jnp.sqrt(_neg_expm1(2.0 * log_a)) * (i * u)
    row = lax.broadcasted_iota(jnp.int32, (CHUNK, 1), 0)
    k = 1
    while k < CHUNK:
        shift = CHUNK - k if reverse else k
        valid = (row < CHUNK - k) if reverse else (row >= k)
        b = jnp.where(valid, a * pltpu.roll(b, shift, 0) + b, b)
        a = jnp.where(valid, a * pltpu.roll(a, shift, 0), a)
        k *= 2
    h = b + a * h_ref[0:1, :]
    h_ref[0:1, :] = h[0:1, :] if reverse else h[CHUNK - 1:CHUNK, :]
    return h


def _lru_kernel(pf_ref, pfp_ref, pfn_ref, pb_ref, pbp_ref, pbn_ref, cw_ref, cb_ref, w_ref, bias_ref, lam_ref,
                hf_ref, hb_ref, st_ref, *, n_lat_chunks, n_chunks):
    j = pl.program_id(1)

    @pl.when(j == 0)
    def _():
        st_ref[...] = jnp.zeros_like(st_ref)

    G = GROUP_W
    for i in range(pf_ref.shape[0]):
        first, last = _seq_edges(_fwd_chunk(j, n_lat_chunks, n_chunks), n_lat_chunks, n_chunks)
        prev8 = jnp.where(first, 0.0, pfp_ref[i, :, :G])
        next8 = jnp.where(last, 0.0, pfn_ref[i, :, :G])
        hf_ref[i] = _lru_chunk(pf_ref[i], prev8, next8, cw_ref, cb_ref, w_ref, bias_ref, lam_ref,
                               st_ref.at[i, 0], 0, False)
        first, last = _seq_edges(_bwd_chunk(j, n_lat_chunks, n_chunks), n_lat_chunks, n_chunks)
        prev8 = jnp.where(first, 0.0, pbp_ref[i, :, :G])
        next8 = jnp.where(last, 0.0, pbn_ref[i, :, :G])
        hb_ref[i] = _lru_chunk(pb_ref[i], prev8, next8, cw_ref, cb_ref, w_ref, bias_ref, lam_ref,
                               st_ref.at[i, 1], 1, True)


def rglru_mixer(p3d, n_lat, conv_w, conv_b, w_a, b_a, w_x, b_x, lam):
    bsz, t, width = p3d.shape
    G = GROUP_W
    n_chunks = t // CHUNK
    n_lat_chunks = n_lat // CHUNK
    w = jnp.zeros((2, G, 2 * G), F32)
    for h in range(N_HEADS):
        sl = slice(h * HEAD_DIM, (h + 1) * HEAD_DIM)
        w = w.at[:, sl, sl].set(w_a[:, h])
        w = w.at[:, sl, G + h * HEAD_DIM:G + (h + 1) * HEAD_DIM].set(w_x[:, h])
    consts = [jnp.pad(conv_w, ((0, SUBLANE - CONV_W), (0, 0))), conv_b.reshape(1, G), w,
              jnp.concatenate([b_a, b_x], axis=1), lam]
    nb = math.gcd(SCAN_SAMPLES, bsz)
    ins, f_out, b_out = _natural_scan_specs(nb, width, G, n_lat_chunks, n_chunks, t // SUBLANE - 1)
    h_sds = jax.ShapeDtypeStruct((bsz, t, G), F32)
    hf, hb = pl.pallas_call(
        functools.partial(_lru_kernel, n_lat_chunks=n_lat_chunks, n_chunks=n_chunks),
        grid=(bsz // nb, n_chunks),
        in_specs=ins + [_const_spec(c.shape) for c in consts],
        out_specs=[f_out, b_out],
        out_shape=[h_sds, h_sds],
        scratch_shapes=[pltpu.VMEM((nb, 2, SUBLANE, G), F32)],
        compiler_params=_params("arbitrary", "arbitrary"),
        name="lru_scan",
    )(p3d, p3d, p3d, p3d, p3d, p3d, *consts)
    return hf.reshape(bsz * t, G), hb.reshape(bsz * t, G)


def _gdn_prepare(x, gates, prev8, next8, cw_ref, al_ref, dt_ref, d, reverse):
    G = GROUP_W
    conv = (cw_ref[0:1, :] * _shift_rows(x, prev8, next8, 2) + cw_ref[1:2, :] * _shift_rows(x, prev8, next8, 1)
            + cw_ref[2:3, :] * x + cw_ref[3:4, :] * _shift_rows(x, prev8, next8, -1))
    qkv = conv * jax.nn.sigmoid(conv)
    bd_mask = _same_head((G, G))
    head_ones = jnp.where(bd_mask, 1.0, 0.0).astype(F32)
    q, k, v = qkv[:, :G], qkv[:, G:2 * G], qkv[:, 2 * G:]
    q = q * lax.rsqrt(_dot_sel(q * q, head_ones) + EPS) * (HEAD_DIM ** -0.5)
    k = k * lax.rsqrt(_dot_sel(k * k, head_ones) + EPS)
    la = -al_ref[d:d + 1, :] * jax.nn.softplus(_dot_sel(gates, _head_lane_expander(d * N_HEADS)) + dt_ref[d:d + 1, :])
    beta = jax.nn.sigmoid(_dot_sel(gates, _head_lane_expander((2 + d) * N_HEADS)))
    gam = _sel_dot(_cum_matrix(reverse), la)
    strict, incl, diag = _token_masks(reverse)
    gam_row = jnp.sum(jnp.where(diag, gam, 0.0), axis=0, keepdims=True)
    decay = jnp.exp(jnp.where(incl, gam - gam_row, -jnp.inf))
    k_beta = k * beta
    k_bd = _to_block_diag(k, bd_mask)
    lower = jnp.where(strict, _dot_nt(k_beta, k_bd) * decay, 0.0)
    e_gam = jnp.exp(gam)
    gam_last = gam[0:1, :] if reverse else gam[CHUNK - 1:CHUNK, :]
    return dict(lower=lower, vb=v * beta, kbe=k_beta * e_gam, qk=_dot_nt(q, k_bd) * decay, q_dec=q * e_gam,
                k_dec=k * jnp.exp(gam_last - gam), g_last=jnp.exp(gam_last))


def _gdn_kernel(pf_ref, gf_ref, pfp_ref, pfn_ref, pb_ref, gb_ref, pbp_ref, pbn_ref, cw_ref, al_ref, dt_ref, s0_ref,
                of_ref, ob_ref, s_ref, *, n_steps, column_view):
    j = pl.program_id(1)

    @pl.when(j == 0)
    def _():
        s_ref[...] = s0_ref[...]

    bd_mask = _same_head((GROUP_W, GROUP_W))
    cf, cb = j, n_steps - 1 - j
    col = (lambda c, off: c + off) if column_view else (lambda c, off: None)
    cs, ss = [], []
    for i in range(pf_ref.shape[0]):
        prev8 = jnp.where(j == 0, 0.0, _take_column(pfp_ref, i, col(cf, -1)))
        next8 = jnp.where(j == n_steps - 1, 0.0, _take_column(pfn_ref, i, col(cf, 1)))
        cs.append(_gdn_prepare(_take_column(pf_ref, i, col(cf, 0)), _take_column(gf_ref, i, col(cf, 0)),
                               prev8, next8, cw_ref, al_ref, dt_ref, 0, False))
        prev8 = jnp.where(j == n_steps - 1, 0.0, _take_column(pbp_ref, i, col(cb, -1)))
        next8 = jnp.where(j == 0, 0.0, _take_column(pbn_ref, i, col(cb, 1)))
        cs.append(_gdn_prepare(_take_column(pb_ref, i, col(cb, 0)), _take_column(gb_ref, i, col(cb, 0)),
                               prev8, next8, cw_ref, al_ref, dt_ref, 1, True))
        ss += [s_ref[i, 0], s_ref[i, 1]]
    tinv = _unit_tri_inverse([c['lower'] for c in cs], bd_mask)
    us = [_dot(t, _to_block_diag(c['vb'], bd_mask)) for t, c in zip(tinv, cs)]
    ws = [_dot(t, _to_block_diag(c['kbe'], bd_mask)) for t, c in zip(tinv, cs)]
    vn = [u - _dot(w, s) for u, w, s in zip(us, ws, ss)]
    outs = [_dot(c['q_dec'], s) + _dot(c['qk'], _to_block_diag(v, bd_mask)) for c, s, v in zip(cs, ss, vn)]
    sn = [c['g_last'] * s + jnp.where(bd_mask, _dot_tn(c['k_dec'], v), 0.0) for c, s, v in zip(cs, ss, vn)]
    for i in range(pf_ref.shape[0]):
        _put_column(of_ref, i, col(cf, 0), outs[2 * i])
        _put_column(ob_ref, i, col(cb, 0), outs[2 * i + 1])
        s_ref[i, 0] = sn[2 * i]
        s_ref[i, 1] = sn[2 * i + 1]


def gdn_scan(p, n_steps, first_row_block, column_view, last8, cw, al, dt, s0):
    bsz = p.shape[0]
    G = GROUP_W
    nb = math.gcd(SCAN_SAMPLES, bsz)
    qkv_w = 3 * G
    f_in, b_in = _chunk_specs(nb, qkv_w, n_steps, first_row_block, column_view)
    f_g, b_g = _chunk_specs(nb, LANE, n_steps, first_row_block, column_view, 4 * G // LANE)
    fp, fn, bp, bn = _halo_specs(nb, qkv_w, n_steps, first_row_block, column_view, last8)
    f_out, b_out = _chunk_specs(nb, G, n_steps, 0, column_view)
    h_sds = jax.ShapeDtypeStruct((bsz, CHUNK, GRID_W, G) if column_view else (bsz, n_steps * CHUNK, G), F32)
    state_spec = pl.BlockSpec((nb, 2, G, G), lambda b, j: (b, 0, 0, 0))
    return pl.pallas_call(
        functools.partial(_gdn_kernel, n_steps=n_steps, column_view=column_view),
        grid=(bsz // nb, n_steps),
        in_specs=[f_in, f_g, fp, fn, b_in, b_g, bp, bn, _const_spec(cw.shape), _const_spec(al.shape),
                  _const_spec(dt.shape),
                  pl.BlockSpec((nb, 2, G, G), lambda b, j: (b, 0, 0, 0), pipeline_mode=pl.Buffered(1))],
        out_specs=[f_out, b_out, state_spec],
        out_shape=[h_sds, h_sds, jax.ShapeDtypeStruct(s0.shape, F32)],
        compiler_params=_params("arbitrary", "arbitrary"),
        name="gdn_scan",
    )(p, p, p, p, p, p, p, p, cw, al, dt, s0)


def gdn_mixer(p3d, n_lat, conv_w, a_log, dt_bias):
    bsz, t, width = p3d.shape
    G = GROUP_W
    n_ctx = t - n_lat
    assert n_lat == CHUNK * GRID_W
    al = jnp.repeat(jnp.exp(a_log), HEAD_DIM, axis=1)
    dt = jnp.repeat(dt_bias, HEAD_DIM, axis=1)
    cw = jnp.pad(conv_w, ((0, SUBLANE - CONV_W), (0, 0)))
    s0 = jnp.zeros((bsz, 2, G, G), F32)
    last8 = t // SUBLANE - 1
    of_c, ob_c, s1 = gdn_scan(p3d, n_ctx // CHUNK, n_lat // CHUNK, False, last8, cw, al, dt, s0)
    of_l, ob_l, _ = gdn_scan(p3d.reshape(bsz, t // GRID_W, GRID_W, width), GRID_W, 0, True, last8, cw, al, dt, s1)
    return (of_l.reshape(bsz * n_lat, G), ob_l.reshape(bsz * n_lat, G),
            of_c.reshape(bsz * n_ctx, G), ob_c.reshape(bsz * n_ctx, G))


def _head_max(x):
    lane_head = lax.broadcasted_iota(jnp.int32, x.shape, 1) // HEAD_DIM
    out = jnp.zeros_like(x)
    for h in range(N_HEADS):
        mh = jnp.max(jnp.where(lane_head == h, x, -jnp.inf), axis=-1, keepdims=True)
        out = jnp.where(lane_head == h, mh, out)
    return out


def _mlstm_prepare(p, gates, ib_ref, fb_ref, d, reverse):
    G = GROUP_W
    q, k, v = p[:, :G], p[:, G:2 * G] * (HEAD_DIM ** -0.5), p[:, 2 * G:3 * G]
    bd_mask = _same_head((G, G))
    li = _dot_sel(gates, _head_lane_expander(d * N_HEADS)) + ib_ref[d:d + 1, :]
    lf = jax.nn.log_sigmoid(_dot_sel(gates, _head_lane_expander((2 + d) * N_HEADS)) + fb_ref[d:d + 1, :])
    b = _sel_dot(_cum_matrix(reverse), lf)
    _, incl, diag = _token_masks(reverse)
    g = li - b
    g_row = jnp.sum(jnp.where(diag, g, 0.0), axis=0, keepdims=True)
    log_d = jnp.where(incl, b + g_row, -jnp.inf)
    b_last = b[0:1, :] if reverse else b[CHUNK - 1:CHUNK, :]
    log_end = b_last + g
    return dict(q=q, k=k, v=v, b=b, log_d=log_d, m_intra=_head_max(log_d), b_last=b_last, log_end=log_end,
                end_max=jnp.max(log_end, axis=0, keepdims=True), qk=_dot_nt(q, _to_block_diag(k, bd_mask)))


def _mlstm_kernel(pf_ref, gf_ref, pb_ref, gb_ref, ib_ref, fb_ref, c0_ref, nm0_ref, hf_ref, hb_ref, c_ref, nm_ref,
                  *, n_steps, column_view):
    @pl.when(pl.program_id(1) == 0)
    def _():
        c_ref[...] = c0_ref[...]
        nm_ref[...] = nm0_ref[...]

    nb = pf_ref.shape[0]
    bd_mask = _same_head((GROUP_W, GROUP_W))
    head_ones = jnp.where(bd_mask, 1.0, 0.0).astype(F32)
    idx = [(i, d) for i in range(nb) for d in range(2)]
    j = pl.program_id(1)
    cols = (j, n_steps - 1 - j) if column_view else (None, None)
    cs = [_mlstm_prepare(_take_column(pb_ref if d else pf_ref, i, cols[d]),
                         _take_column(gb_ref if d else gf_ref, i, cols[d]), ib_ref, fb_ref, d, bool(d))
          for i, d in idx]
    mems = [c_ref[i, d] for i, d in idx]
    nrms = [nm_ref[i, d, 0:1, :] for i, d in idx]
    m_prevs = [nm_ref[i, d, 1:2, :] for i, d in idx]
    m_inters = [c['b'] + m for c, m in zip(cs, m_prevs)]
    m_ts = [jnp.maximum(mi, c['m_intra']) for mi, c in zip(m_inters, cs)]
    w_inters = [jnp.exp(mi - mt) for mi, mt in zip(m_inters, m_ts)]
    pms = [jnp.exp(c['log_d'] - mt) * c['qk'] for c, mt in zip(cs, m_ts)]
    inter = [_dot_nt(c['q'], mem) for c, mem in zip(cs, mems)]
    nums = [_dot(pm, _to_block_diag(c['v'], bd_mask)) + w * it for pm, c, w, it in zip(pms, cs, w_inters, inter)]
    dens = [_dot_sel(pm + w * (c['q'] * nrm), head_ones) for pm, w, c, nrm in zip(pms, w_inters, cs, nrms)]
    hs = [num / jnp.maximum(jnp.abs(den), jnp.exp(-mt)) for num, den, mt in zip(nums, dens, m_ts)]
    m_news = [jnp.maximum(c['b_last'] + m, c['end_max']) for c, m in zip(cs, m_prevs)]
    w_prevs = [jnp.exp(c['b_last'] + m - mn) for c, m, mn in zip(cs, m_prevs, m_news)]
    w_news = [jnp.exp(c['log_end'] - mn) for c, mn in zip(cs, m_news)]
    upds = [_dot_tn(wn * c['v'], c['k']) for wn, c in zip(w_news, cs)]
    for n_, (i, d) in enumerate(idx):
        _put_column(hb_ref if d else hf_ref, i, cols[d], hs[n_])
        c_ref[i, d] = w_prevs[n_] * mems[n_] + jnp.where(bd_mask, upds[n_], 0.0)
        nm_ref[i, d, 0:1, :] = w_prevs[n_] * nrms[n_] + jnp.sum(w_news[n_] * cs[n_]['k'], axis=0, keepdims=True)
        nm_ref[i, d, 1:2, :] = m_news[n_]


def mlstm_scan(p, n_steps, first_row_block, column_view, ib, fb, c0, nm0):
    bsz = p.shape[0]
    G = GROUP_W
    nb = math.gcd(SCAN_SAMPLES, bsz)
    qkv_w = 3 * G
    f_in, b_in = _chunk_specs(nb, qkv_w, n_steps, first_row_block, column_view)
    f_g, b_g = _chunk_specs(nb, LANE, n_steps, first_row_block, column_view, 4 * G // LANE)
    f_out, b_out = _chunk_specs(nb, G, n_steps, 0, column_view)
    h_sds = jax.ShapeDtypeStruct((bsz, CHUNK, GRID_W, G) if column_view else (bsz, n_steps * CHUNK, G), F32)
    state_specs = [pl.BlockSpec((nb, 2, G, G), lambda b, j: (b, 0, 0, 0)),
                   pl.BlockSpec((nb, 2, SUBLANE, G), lambda b, j: (b, 0, 0, 0))]
    return pl.pallas_call(
        functools.partial(_mlstm_kernel, n_steps=n_steps, column_view=column_view),
        grid=(bsz // nb, n_steps),
        in_specs=[f_in, f_g, b_in, b_g, _const_spec(ib.shape), _const_spec(fb.shape)] + state_specs,
        out_specs=[f_out, b_out] + state_specs,
        out_shape=[h_sds, h_sds, jax.ShapeDtypeStruct(c0.shape, F32), jax.ShapeDtypeStruct(nm0.shape, F32)],
        compiler_params=_params("arbitrary", "arbitrary"),
        name="mlstm_scan",
    )(p, p, p, p, ib, fb, c0, nm0)


def mlstm_mixer(p3d, n_lat, i_b, f_b):
    bsz, t, width = p3d.shape
    G = GROUP_W
    n_ctx = t - n_lat
    assert n_lat == CHUNK * GRID_W
    ib = jnp.repeat(i_b, HEAD_DIM, axis=1)
    fb = jnp.repeat(f_b, HEAD_DIM, axis=1)
    c0 = jnp.zeros((bsz, 2, G, G), F32)
    nm0 = jnp.zeros((bsz, 2, SUBLANE, G), F32)
    hf_c, hb_c, c1, nm1 = mlstm_scan(p3d, n_ctx // CHUNK, n_lat // CHUNK, False, ib, fb, c0, nm0)
    hf_l, hb_l, _, _ = mlstm_scan(p3d.reshape(bsz, t // GRID_W, GRID_W, width), GRID_W, 0, True, ib, fb, c1, nm1)
    return (hf_l.reshape(bsz * n_lat, G), hb_l.reshape(bsz * n_lat, G),
            hf_c.reshape(bsz * n_ctx, G), hb_c.reshape(bsz * n_ctx, G))


def kernel(x, c, ctx, c_ctx, mod_w, mod_b, norm_mix_g, norm_ffn_g, w_in, w_out, lru_conv_w, lru_conv_b, lru_w_a, lru_b_a, lru_w_x, lru_b_x, lru_lambda, rwkv_mu, rwkv_w_up, rwkv_w0, rwkv_a_up, rwkv_a0, rwkv_g_up, rwkv_k_k, rwkv_k_a, rwkv_r_k, rwkv_ln_g, rwkv_ln_b, mlstm_i_b, mlstm_f_b, mlstm_norm_g, gdn_conv_w, gdn_a_log, gdn_dt_bias, gdn_norm_g, ffn_w_gate, ffn_w_up, ffn_w_down, moe_router, moe_w_gate, moe_w_up, moe_w_down, final_norm_g):
    bsz, seq, d = x.shape
    n_ctx = ctx.shape[1]
    t = n_ctx + seq
    n = bsz * t
    tiles_per_seq = t // ROW_TILE
    assert n_ctx == ROW_TILE and seq % ROW_TILE == 0
    x2 = jnp.concatenate([x, ctx], axis=1).reshape(n, d)
    mod = jnp.einsum('bd,ldk->lbk', jax.nn.silu(c), mod_w) + mod_b[:, None]
    mod_c = jnp.einsum('d,ldk->lk', jax.nn.silu(c_ctx), mod_w) + mod_b
    mods = jnp.stack([mod, jnp.broadcast_to(mod_c[:, None], mod.shape)], axis=2).reshape(DEPTH, 2 * bsz, 6, 1, d)
    w_groups = [jnp.pad(w_in[:, :, lo:hi], ((0, 0), (0, 0), (0, _round_up(hi - lo, LANE) - (hi - lo)))).astype(BF16)
                for lo, hi in ((0, OFF_B), (OFF_B, OFF_C), (OFF_C, OFF_D), (OFF_D, IN_COLS))]
    w_out_bf = w_out.astype(BF16)
    ffn_bf = [w.astype(BF16) for w in (ffn_w_gate, ffn_w_up, ffn_w_down)]
    moe_bf = [w.astype(BF16) for w in (moe_w_gate, moe_w_up, moe_w_down)]
    router_pad = jnp.pad(moe_router, ((0, 0), (0, 0), (0, LANE - N_EXPERTS)))
    for l in range(DEPTH):
        sh1, sc1, gt1, sh2, sc2, gt2 = [mods[l, :, j] for j in range(6)]
        pa, pb, pc, pd = norm_proj(x2, norm_mix_g[l], sc1, sh1, w_groups, l, tiles_per_seq)
        lru = rglru_mixer(pa.reshape(bsz, t, -1), seq, lru_conv_w[l], lru_conv_b[l], lru_w_a[l], lru_b_a[l],
                          lru_w_x[l], lru_b_x[l], lru_lambda[l])
        rwkv = rwkv7_mixer(pb.reshape(bsz, t, -1), seq, rwkv_mu[l], rwkv_w_up[l], rwkv_w0[l], rwkv_a_up[l],
                           rwkv_a0[l], rwkv_g_up[l], rwkv_k_k[l], rwkv_k_a[l], rwkv_r_k[l])
        mlstm = mlstm_mixer(pc.reshape(bsz, t, -1), seq, mlstm_i_b[l], mlstm_f_b[l])
        gdn = gdn_mixer(pd.reshape(bsz, t, -1), seq, gdn_conv_w[l], gdn_a_log[l], gdn_dt_bias[l])
        x2 = out_proj(lru, rwkv, mlstm, gdn, pa, pc, pd, rwkv_ln_g[l], rwkv_ln_b[l], mlstm_norm_g[l],
                      gdn_norm_g[l], w_out_bf, l, x2, gt1, tiles_per_seq)
        if l % 2 == 0:
            x2 = dense_ffn(x2, norm_ffn_g[l], sc2, sh2, gt2, *ffn_bf, l // 2, tiles_per_seq)
        else:
            h2, lg = norm_router(x2, norm_ffn_g[l], sc2, sh2, router_pad, l // 2, tiles_per_seq)
            y = moe_ffn(h2, lg[:, :N_EXPERTS], *moe_bf, l // 2)
            gt_pair = gt2.reshape(bsz, 2, 1, d)
            is_latent = (jnp.arange(t) < seq)[None, :, None]
            x2 = (x2.reshape(bsz, t, d)
                  + jnp.where(is_latent, gt_pair[:, 0], gt_pair[:, 1]) * y.reshape(bsz, t, d)).reshape(n, d)
    return final_norm(x2.reshape(bsz, t, d), final_norm_g, seq)
```

```python
import functools
import math

import jax
import jax.numpy as jnp
from jax import lax
from jax.experimental import pallas as pl
from jax.experimental.pallas import tpu as pltpu

D_MODEL = 1024
DEPTH = 4
GRID_W = 64
N_MIXERS = 4
GROUP_W = D_MODEL // N_MIXERS
HEAD_DIM = 64
N_HEADS = GROUP_W // HEAD_DIM
CONV_W = 4
LRU_C = 8.0
RWKV_LORA_W = 32
RWKV_LORA_A = 32
RWKV_LORA_G = 64
RWKV_GN_EPS = 64e-5
CHUNK = 64
D_FF = 2816
N_EXPERTS = 8
TOP_K = 2
D_FF_EXPERT = 1408
MOE_BLOCK = 256
EPS = 1e-6

A_COLS = 2 * GROUP_W
B_COLS = 3 * GROUP_W + 2 * RWKV_LORA_W + 2 * RWKV_LORA_A + RWKV_LORA_G
C_COLS = 4 * GROUP_W + 4 * N_HEADS
D_COLS = 4 * GROUP_W + 4 * N_HEADS
OFF_B = A_COLS
OFF_C = OFF_B + B_COLS
OFF_D = OFF_C + C_COLS
IN_COLS = OFF_D + D_COLS

LANE = 128
SUBLANE = 8
HALO_ROWS = 16
VMEM_LIMIT = 56 * 1024 * 1024
ROW_TILE = 256
SCAN_SAMPLES = 4
F32 = jnp.float32
BF16 = jnp.bfloat16
HI = lax.Precision.HIGHEST


def _round_up(n, m):
    return (n + m - 1) // m * m


def _pad_cols(w, n):
    return jnp.pad(w, ((0, 0), (0, n - w.shape[1])))


def _const_spec(shape):
    nd = len(shape)
    return pl.BlockSpec(shape, lambda *_: (0,) * nd, pipeline_mode=pl.Buffered(1))


def _params(*sem):
    return pltpu.CompilerParams(dimension_semantics=sem, vmem_limit_bytes=VMEM_LIMIT)


def _mod_spec(d, tiles_per_seq):
    def index(i):
        return (2 * (i // tiles_per_seq) + (i % tiles_per_seq) // (tiles_per_seq - 1), 0, 0)
    return pl.BlockSpec((1, 1, d), index)


def _norm_mod(x, g, sc, sh):
    ms = jnp.mean(x * x, axis=-1, keepdims=True)
    return (x * lax.rsqrt(ms + EPS) * g) * (1.0 + sc) + sh


def _dot_dims(a, b, dims, prec):
    if prec is None:
        a, b, prec = a.astype(BF16), b.astype(BF16), None
    return lax.dot_general(a, b, (dims, ((), ())), preferred_element_type=F32, precision=prec)


def _dot(a, b, prec=None):
    return _dot_dims(a, b, ((1,), (0,)), prec)


def _dot_nt(a, b, prec=None):
    return _dot_dims(a, b, ((1,), (1,)), prec)


def _dot_tn(a, b, prec=None):
    return _dot_dims(a, b, ((0,), (0,)), prec)


def _split3(x):
    x1 = x.astype(BF16)
    r = x - x1.astype(F32)
    x2 = r.astype(BF16)
    return x1, x2, (r - x2.astype(F32)).astype(BF16)


def _dot_sel(a, sel):
    s = sel.astype(BF16)
    p1, p2, p3 = [lax.dot_general(p, s, (((1,), (0,)), ((), ())), preferred_element_type=F32) for p in _split3(a)]
    return p1 + p2 + p3


def _sel_dot(sel, b):
    s = sel.astype(BF16)
    p1, p2, p3 = [lax.dot_general(s, p, (((1,), (0,)), ((), ())), preferred_element_type=F32) for p in _split3(b)]
    return p1 + p2 + p3


def _norm_proj_kernel(x_ref, g_ref, sc_ref, sh_ref, *refs, n_out):
    h = _norm_mod(x_ref[...], g_ref[...], sc_ref[0], sh_ref[0]).astype(BF16)
    for w_ref, o_ref in zip(refs[:n_out], refs[n_out:]):
        o_ref[...] = _dot(h, w_ref[...]).astype(o_ref.dtype)


def _layer_spec(w, layer):
    return pl.BlockSpec((None,) + w.shape[1:], lambda *_: (layer, 0, 0), pipeline_mode=pl.Buffered(1))


def norm_proj(x2d, g, sc, sh, ws, out_dtypes, layer, tiles_per_seq):
    n, d = x2d.shape
    tm = ROW_TILE
    mod_spec = _mod_spec(d, tiles_per_seq)
    return pl.pallas_call(
        functools.partial(_norm_proj_kernel, n_out=len(ws)),
        grid=(n // tm,),
        in_specs=[pl.BlockSpec((tm, d), lambda i: (i, 0)), _const_spec((1, d)), mod_spec, mod_spec]
        + [_layer_spec(w, layer) for w in ws],
        out_specs=[pl.BlockSpec((tm, w.shape[2]), lambda i: (i, 0)) for w in ws],
        out_shape=[jax.ShapeDtypeStruct((n, w.shape[2]), dt) for w, dt in zip(ws, out_dtypes)],
        compiler_params=_params("arbitrary"),
        name="norm_proj",
    )(x2d, g.reshape(1, d), sc, sh, *ws)


def _out_proj_kernel(a_hf, a_hb, a_gate, b_yf, b_yb, b_bf, b_bb, b_g, c_fl, c_bl, c_fc, c_bc, c_gate,
                     d_fl, d_bl, d_fc, d_bc, d_gate, lng_ref, lnb_ref, cg_ref, dg_ref, w_ref, x_ref, gt_ref,
                     o_ref, *, tiles_per_seq):
    is_ctx = pl.program_id(0) % tiles_per_seq == tiles_per_seq - 1
    seg_mean = jnp.where(_same_head((GROUP_W, GROUP_W)), 1.0 / HEAD_DIM, 0.0).astype(F32)
    f32 = lambda ref: ref[...].astype(F32)
    ya = (a_hf[...] + a_hb[...]) * jax.nn.gelu(a_gate[...])
    yb = b_yf[...] + b_yb[...]
    mean = _dot_sel(yb, seg_mean)
    var = _dot_sel(jnp.square(yb - mean), seg_mean)
    yb = (yb - mean) * lax.rsqrt(var + RWKV_GN_EPS) * lng_ref[...] + lnb_ref[...]
    yb = (yb + b_bf[...] + b_bb[...]) * b_g[...]
    hc = jnp.where(is_ctx, f32(c_fc) + f32(c_bc), f32(c_fl) + f32(c_bl))
    yc = hc * lax.rsqrt(_dot_sel(hc * hc, seg_mean) + EPS) * cg_ref[...] * jax.nn.sigmoid(f32(c_gate))
    hd = jnp.where(is_ctx, f32(d_fc) + f32(d_bc), f32(d_fl) + f32(d_bl))
    gate = f32(d_gate)
    yd = hd * lax.rsqrt(_dot_sel(hd * hd, seg_mean) + EPS) * dg_ref[...] * (gate * jax.nn.sigmoid(gate))
    y = jnp.concatenate([ya, yb, yc, yd], axis=-1).astype(BF16)
    o_ref[...] = x_ref[...] + gt_ref[0] * _dot(y, w_ref[...])


def out_proj(lru, rwkv, mlstm, gdn, pa, pc, pd, ln_g, ln_b, c_g, d_g, w, layer, x2d, gt, tiles_per_seq):
    n, d = x2d.shape
    G = GROUP_W
    tm = ROW_TILE
    lat_tiles = tiles_per_seq - 1
    row = pl.BlockSpec((tm, G), lambda i: (i, 0))
    lat = pl.BlockSpec((tm, G), lambda i: ((i // tiles_per_seq) * lat_tiles
                                             + jnp.minimum(i % tiles_per_seq, lat_tiles - 1), 0))
    ctx = pl.BlockSpec((tm, G), lambda i: (i // tiles_per_seq, 0))
    col = lambda c: pl.BlockSpec((tm, G), lambda i: (i, c))
    vec = _const_spec((1, G))
    return pl.pallas_call(
        functools.partial(_out_proj_kernel, tiles_per_seq=tiles_per_seq),
        grid=(n // tm,),
        in_specs=[row, row, col(1)] + [row] * 5 + [lat, lat, ctx, ctx, col(3)] * 2 + [vec] * 4
        + [_layer_spec(w, layer), pl.BlockSpec((tm, d), lambda i: (i, 0)), _mod_spec(d, tiles_per_seq)],
        out_specs=pl.BlockSpec((tm, d), lambda i: (i, 0)),
        out_shape=jax.ShapeDtypeStruct((n, d), F32),
        compiler_params=_params("arbitrary"),
        name="out_proj",
    )(*lru, pa, *rwkv, *mlstm, pc, *gdn, pd, ln_g.reshape(1, G), ln_b.reshape(1, G), c_g.reshape(1, G),
      d_g.reshape(1, G), w, x2d, gt)


def _swiglu_chunks(hb, wg_ref, wu_ref, wd_ref, n_chunks, fc):
    acc = None
    for c in range(n_chunks):
        a = _dot(hb, wg_ref[:, c * fc:(c + 1) * fc])
        u = _dot(hb, wu_ref[:, c * fc:(c + 1) * fc])
        z = (a * jax.nn.sigmoid(a) * u).astype(BF16)
        part = _dot(z, wd_ref[c * fc:(c + 1) * fc, :])
        acc = part if acc is None else acc + part
    return acc


def _dense_ffn_kernel(x_ref, g_ref, sc_ref, sh_ref, gt_ref, wg_ref, wu_ref, wd_ref, o_ref, *, n_chunks, fc):
    x = x_ref[...]
    hb = _norm_mod(x, g_ref[...], sc_ref[0], sh_ref[0]).astype(BF16)
    o_ref[...] = x + gt_ref[0] * _swiglu_chunks(hb, wg_ref, wu_ref, wd_ref, n_chunks, fc)


def dense_ffn(x2d, g, sc, sh, gt, wg, wu, wd, layer, tiles_per_seq):
    n, d = x2d.shape
    tm = ROW_TILE
    fc = D_FF // 2
    mod_spec = _mod_spec(d, tiles_per_seq)
    return pl.pallas_call(
        functools.partial(_dense_ffn_kernel, n_chunks=D_FF // fc, fc=fc),
        grid=(n // tm,),
        in_specs=[pl.BlockSpec((tm, d), lambda i: (i, 0)), _const_spec((1, d)), mod_spec, mod_spec, mod_spec,
                  _layer_spec(wg, layer), _layer_spec(wu, layer), _layer_spec(wd, layer)],
        out_specs=pl.BlockSpec((tm, d), lambda i: (i, 0)),
        out_shape=jax.ShapeDtypeStruct((n, d), F32),
        compiler_params=_params("arbitrary"),
        name="dense_ffn",
    )(x2d, g.reshape(1, d), sc, sh, gt, wg, wu, wd)


def _router_kernel(x_ref, g_ref, sc_ref, sh_ref, r_ref, h_ref, l_ref):
    h = _norm_mod(x_ref[...], g_ref[...], sc_ref[0], sh_ref[0])
    h_ref[...] = h.astype(BF16)
    l_ref[...] = _dot(h, r_ref[...], HI)


def norm_router(x2d, g, sc, sh, router_pad, layer, tiles_per_seq):
    n, d = x2d.shape
    tm = ROW_TILE
    mod_spec = _mod_spec(d, tiles_per_seq)
    return pl.pallas_call(
        _router_kernel,
        grid=(n // tm,),
        in_specs=[pl.BlockSpec((tm, d), lambda i: (i, 0)), _const_spec((1, d)), mod_spec, mod_spec,
                  _layer_spec(router_pad, layer)],
        out_specs=[pl.BlockSpec((tm, d), lambda i: (i, 0)), pl.BlockSpec((tm, LANE), lambda i: (i, 0))],
        out_shape=[jax.ShapeDtypeStruct((n, d), BF16), jax.ShapeDtypeStruct((n, LANE), F32)],
        compiler_params=_params("arbitrary"),
        name="norm_router",
    )(x2d, g.reshape(1, d), sc, sh, router_pad)


def _expert_kernel(be_ref, nb_ref, xb_ref, wg_ref, wu_ref, wd_ref, o_ref):
    i = pl.program_id(0)

    @pl.when(i < nb_ref[0])
    def _():
        o_ref[...] = _swiglu_chunks(xb_ref[...], wg_ref.at[0], wu_ref.at[0], wd_ref.at[0], 1, D_FF_EXPERT)

    @pl.when(i >= nb_ref[0])
    def _():
        o_ref[...] = jnp.zeros_like(o_ref)


def expert_ffn(xb, block_e, n_used, wg, wu, wd, layer, tm):
    npad, d = xb.shape
    f = wg.shape[3]
    grid_spec = pltpu.PrefetchScalarGridSpec(
        num_scalar_prefetch=2,
        grid=(npad // tm,),
        in_specs=[pl.BlockSpec((tm, d), lambda i, be, nb: (i, 0)),
                  pl.BlockSpec((None, 1, d, f), lambda i, be, nb: (layer, be[i], 0, 0)),
                  pl.BlockSpec((None, 1, d, f), lambda i, be, nb: (layer, be[i], 0, 0)),
                  pl.BlockSpec((None, 1, f, d), lambda i, be, nb: (layer, be[i], 0, 0))],
        out_specs=pl.BlockSpec((tm, d), lambda i, be, nb: (i, 0)),
    )
    return pl.pallas_call(
        _expert_kernel,
        grid_spec=grid_spec,
        out_shape=jax.ShapeDtypeStruct((npad, d), F32),
        compiler_params=_params("arbitrary"),
        name="expert_ffn",
    )(block_e, n_used, xb, wg, wu, wd)


def moe_ffn(h_bf, logits, wg, wu, wd, layer):
    t, d = h_bf.shape
    top_logits, top_idx = lax.top_k(logits, TOP_K)
    gate = jax.nn.softmax(top_logits, axis=-1)
    tk = t * TOP_K
    onehot = (top_idx.reshape(1, tk) == jnp.arange(N_EXPERTS, dtype=jnp.int32)[:, None]).astype(jnp.int32)
    csum = jnp.cumsum(onehot, axis=1)
    counts = csum[:, -1]
    padded = (counts + MOE_BLOCK - 1) // MOE_BLOCK * MOE_BLOCK
    pad_end = jnp.cumsum(padded)
    pad_start = pad_end - padded
    pos = jnp.sum(onehot * (csum - 1 + pad_start[:, None]), axis=0)
    n_blocks = -(-tk // MOE_BLOCK) + N_EXPERTS
    buf_tok = jnp.zeros((n_blocks * MOE_BLOCK,), jnp.int32).at[pos].set(jnp.arange(tk, dtype=jnp.int32) // TOP_K)
    slot_of = pos.reshape(t, TOP_K)
    block_e = jnp.clip(jnp.searchsorted(pad_end, jnp.arange(n_blocks, dtype=jnp.int32) * MOE_BLOCK,
                                        side='right'), 0, N_EXPERTS - 1).astype(jnp.int32)
    n_used = (pad_end[-1] // MOE_BLOCK).astype(jnp.int32).reshape(1)
    yb = expert_ffn(h_bf[buf_tok], block_e, n_used, wg, wu, wd, layer, MOE_BLOCK)
    return gate[:, 0:1] * yb[slot_of[:, 0]] + gate[:, 1:2] * yb[slot_of[:, 1]]


def _final_norm_kernel(x_ref, g_ref, o_ref):
    x = x_ref[...]
    ms = jnp.mean(x * x, axis=-1, keepdims=True)
    o_ref[...] = x * lax.rsqrt(ms + EPS) * g_ref[...]


def final_norm(x3d, g, n_lat):
    bsz, t, d = x3d.shape
    tm = ROW_TILE
    return pl.pallas_call(
        _final_norm_kernel,
        grid=(bsz, n_lat // tm),
        in_specs=[pl.BlockSpec((None, tm, d), lambda b, i: (b, i, 0)), _const_spec((1, d))],
        out_specs=pl.BlockSpec((None, tm, d), lambda b, i: (b, i, 0)),
        out_shape=jax.ShapeDtypeStruct((bsz, n_lat, d), F32),
        compiler_params=_params("arbitrary", "arbitrary"),
        name="final_norm",
    )(x3d, g.reshape(1, d))


def _same_head(shape):
    r = lax.broadcasted_iota(jnp.int32, shape, 0) // HEAD_DIM
    c = lax.broadcasted_iota(jnp.int32, shape, 1) // HEAD_DIM
    return r == c


def _to_block_diag(x, mask):
    xb = x.astype(BF16)
    return jnp.where(mask, jnp.concatenate([xb] * N_HEADS, axis=0), jnp.zeros((), BF16))


def _token_masks(reverse):
    t = lax.broadcasted_iota(jnp.int32, (CHUNK, GROUP_W), 0)
    s = lax.broadcasted_iota(jnp.int32, (CHUNK, GROUP_W), 1) % HEAD_DIM
    if reverse:
        return s > t, s >= t, s == t
    return s < t, s <= t, s == t


def _cum_matrix(reverse):
    t = lax.broadcasted_iota(jnp.int32, (CHUNK, CHUNK), 0)
    s = lax.broadcasted_iota(jnp.int32, (CHUNK, CHUNK), 1)
    return jnp.where((s >= t) if reverse else (s <= t), 1.0, 0.0).astype(F32)


def _unit_tri_inverse(ls, bd_mask):
    t = lax.broadcasted_iota(jnp.int32, (CHUNK, GROUP_W), 0)
    s = lax.broadcasted_iota(jnp.int32, (CHUNK, GROUP_W), 1) % HEAD_DIM
    blk16 = (t // 16) == (s // 16)
    blk32 = (t // 32) == (s // 32)
    eye = jnp.where(t == s, 1.0, 0.0)
    ps = [jnp.where(blk16, -l, 0.0) for l in ls]
    xs = [eye + p for p in ps]
    for _ in range(3):
        ps = [_dot(p, _to_block_diag(p, bd_mask)) for p in ps]
        xs = [x + _dot(x, _to_block_diag(p, bd_mask)) for x, p in zip(xs, ps)]
    for off_diag in (jnp.logical_and(blk32, jnp.logical_not(blk16)), jnp.logical_not(blk32)):
        xe = [_dot(x, _to_block_diag(jnp.where(off_diag, l, 0.0), bd_mask)) for x, l in zip(xs, ls)]
        xs = [x - _dot(y, _to_block_diag(x, bd_mask)) for x, y in zip(xs, xe)]
    return xs


def _head_lane_expander(col0):
    r = lax.broadcasted_iota(jnp.int32, (LANE, GROUP_W), 0)
    c = lax.broadcasted_iota(jnp.int32, (LANE, GROUP_W), 1) // HEAD_DIM
    return jnp.where(r == col0 + c, 1.0, 0.0).astype(F32)


def _fwd_chunk(j, n_lat_chunks, n_chunks):
    n_ctx_chunks = n_chunks - n_lat_chunks
    return jnp.where(j < n_ctx_chunks, n_lat_chunks + j, j - n_ctx_chunks)


def _bwd_chunk(j, n_lat_chunks, n_chunks):
    return n_chunks - 1 - j


def _seq_edges(c, n_lat_chunks, n_chunks):
    first = jnp.logical_or(c == 0, c == n_lat_chunks)
    last = jnp.logical_or(c == n_lat_chunks - 1, c == n_chunks - 1)
    return first, last


def _shift_rows(x, prev, nxt, back):
    row = lax.broadcasted_iota(jnp.int32, (CHUNK, 1), 0)
    if back == -1:
        return jnp.where(row == CHUNK - 1, nxt[0:1, :], pltpu.roll(x, CHUNK - 1, 0))
    out = pltpu.roll(x, back, 0)
    n_prev = prev.shape[0]
    for i in range(back):
        out = jnp.where(row == i, prev[n_prev - back + i:n_prev - back + i + 1, :], out)
    return out


def _natural_scan_specs(nb, width, out_width, n_lat_chunks, n_chunks, last8):
    r8 = CHUNK // SUBLANE
    fwd = functools.partial(_fwd_chunk, n_lat_chunks=n_lat_chunks, n_chunks=n_chunks)
    bwd = functools.partial(_bwd_chunk, n_lat_chunks=n_lat_chunks, n_chunks=n_chunks)
    chunk = lambda f, w: pl.BlockSpec((nb, CHUNK, w), lambda b, j: (b, f(j), 0))
    prev = lambda f: pl.BlockSpec((nb, SUBLANE, width), lambda b, j: (b, jnp.maximum(f(j) * r8 - 1, 0), 0))
    nxt = lambda f: pl.BlockSpec((nb, SUBLANE, width), lambda b, j: (b, jnp.minimum((f(j) + 1) * r8, last8), 0))
    ins = [chunk(fwd, width), prev(fwd), nxt(fwd), chunk(bwd, width), prev(bwd), nxt(bwd)]
    return ins, chunk(fwd, out_width), chunk(bwd, out_width)


def _chunk_specs(nb, width, n_steps, first_row_block, column_view):
    if column_view:
        return (pl.BlockSpec((nb, CHUNK, width), lambda b, j: (b, 0, j)),
                pl.BlockSpec((nb, CHUNK, width), lambda b, j: (b, 0, n_steps - 1 - j)))
    return (pl.BlockSpec((nb, CHUNK, width), lambda b, j: (b, first_row_block + j, 0)),
            pl.BlockSpec((nb, CHUNK, width), lambda b, j: (b, first_row_block + n_steps - 1 - j, 0)))


def _halo_specs(nb, width, n_steps, first_row_block, column_view, last_halo):
    rh = CHUNK // HALO_ROWS
    fwd = lambda j: j
    bwd = lambda j: n_steps - 1 - j
    if column_view:
        prev = lambda f: pl.BlockSpec((nb, HALO_ROWS, width), lambda b, j: (b, rh - 1, jnp.maximum(f(j) - 1, 0)))
        nxt = lambda f: pl.BlockSpec((nb, HALO_ROWS, width), lambda b, j: (b, 0, jnp.minimum(f(j) + 1, n_steps - 1)))
    else:
        prev = lambda f: pl.BlockSpec((nb, HALO_ROWS, width),
                                      lambda b, j: (b, jnp.maximum((first_row_block + f(j)) * rh - 1, 0), 0))
        nxt = lambda f: pl.BlockSpec((nb, HALO_ROWS, width),
                                     lambda b, j: (b, jnp.minimum((first_row_block + f(j) + 1) * rh, last_halo), 0))
    return prev(fwd), nxt(fwd), prev(bwd), nxt(bwd)


def _rwkv_prepare(p, prev_row, next_row, prm, d, reverse):
    (mu_ref, wup_ref, w0_ref, aup_ref, a0_ref, gup_ref, kk_ref, ka_ref, rk_ref) = prm
    G = GROUP_W
    row = lax.broadcasted_iota(jnp.int32, (CHUNK, 1), 0)
    prev = jnp.where(row == 0, prev_row, pltpu.roll(p, 1, 0))
    nxt = jnp.where(row == CHUNK - 1, next_row, pltpu.roll(p, CHUNK - 1, 0))
    ps = p + mu_ref[0:1, :] * (prev - p) + mu_ref[1:2, :] * (nxt - p)
    r, k, v = ps[:, :G], ps[:, G:2 * G], ps[:, 2 * G:3 * G]
    lora_in = ps[:, 3 * G:3 * G + LANE]
    gate_in = ps[:, 3 * G + LANE:3 * G + 2 * LANE]
    bd_mask = _same_head((G, G))
    head_ones = jnp.where(bd_mask, 1.0, 0.0).astype(F32)
    log_w = -math.exp(-0.5) * jax.nn.sigmoid(_dot(jnp.tanh(lora_in), wup_ref[d]) + w0_ref[d:d + 1, :])
    a = jax.nn.sigmoid(_dot(lora_in, aup_ref[d]) + a0_ref[d:d + 1, :])
    kr = k * kk_ref[...]
    kk = kr * lax.rsqrt(_dot_sel(kr * kr, head_ones) + EPS)
    k_mod = k * (1.0 + (a - 1.0) * ka_ref[...])
    bonus = _dot_sel(r * k_mod * rk_ref[...], head_ones) * v
    lg = _sel_dot(_cum_matrix(reverse), log_w)
    e_neg = jnp.exp(-lg)
    at = kk * jnp.exp(lg - log_w)
    bt = -(kk * a) * e_neg
    kt = k_mod * e_neg
    rt = r * jnp.exp(lg)
    g_last = jnp.exp(lg[0:1, :] if reverse else lg[CHUNK - 1:CHUNK, :])
    strict, incl, _ = _token_masks(reverse)
    gate = None if reverse else _dot(jax.nn.sigmoid(gate_in), gup_ref[...])
    return dict(at=at, rt=rt, bt=bt, kt=kt, v=v, g_last=g_last, bonus=bonus, gate=gate, strict=strict, incl=incl)


def _rwkv_kernel(pf_ref, pfp_ref, pfn_ref, pb_ref, pbp_ref, pbn_ref,
                 mu_ref, wup_ref, w0_ref, aup_ref, a0_ref, gup_ref, kk_ref, ka_ref, rk_ref,
                 yf_ref, bf_ref, g_ref, yb_ref, bb_ref, s_ref, *, n_lat_chunks, n_chunks):
    j = pl.program_id(1)

    @pl.when(j == 0)
    def _():
        s_ref[...] = jnp.zeros_like(s_ref)

    prm = (mu_ref, wup_ref, w0_ref, aup_ref, a0_ref, gup_ref, kk_ref, ka_ref, rk_ref)
    nb = pf_ref.shape[0]
    bd_mask = _same_head((GROUP_W, GROUP_W))
    idx = [(i, d) for i in range(nb) for d in range(2)]
    cs = []
    for i, d in idx:
        chunk = (_bwd_chunk if d else _fwd_chunk)(j, n_lat_chunks, n_chunks)
        first, last = _seq_edges(chunk, n_lat_chunks, n_chunks)
        p_ref, pp_ref, pn_ref = (pb_ref, pbp_ref, pbn_ref) if d else (pf_ref, pfp_ref, pfn_ref)
        prev_row = jnp.where(first, 0.0, pp_ref[i, SUBLANE - 1:SUBLANE, :])
        next_row = jnp.where(last, 0.0, pn_ref[i, 0:1, :])
        cs.append(_rwkv_prepare(p_ref[i], prev_row, next_row, prm, d, bool(d)))
    ss = [s_ref[i, d] for i, d in idx]
    lhs = [jnp.concatenate([c['at'], c['rt']], axis=0) for c in cs]
    m_b = [_dot_nt(l, _to_block_diag(c['bt'], bd_mask)) for l, c in zip(lhs, cs)]
    m_k = [_dot_nt(l, _to_block_diag(c['kt'], bd_mask)) for l, c in zip(lhs, cs)]
    a_ak = [jnp.where(c['strict'], m[:CHUNK], 0.0) for c, m in zip(cs, m_k)]
    a_rb = [jnp.where(c['incl'], m[CHUNK:], 0.0) for c, m in zip(cs, m_b)]
    a_rk = [jnp.where(c['incl'], m[CHUNK:], 0.0) for c, m in zip(cs, m_k)]
    tinv = _unit_tri_inverse([jnp.where(c['strict'], -m[:CHUNK], 0.0) for c, m in zip(cs, m_b)], bd_mask)
    v_bd = [_to_block_diag(c['v'], bd_mask) for c in cs]
    rhs = [_dot_nt(c['at'], s) + _dot(a, vb) for c, s, a, vb in zip(cs, ss, a_ak, v_bd)]
    us = [_dot(t, _to_block_diag(r, bd_mask)) for t, r in zip(tinv, rhs)]
    ys = [_dot_nt(c['rt'], s) + _dot(a, vb) for c, s, a, vb in zip(cs, ss, a_rk, v_bd)]
    ys = [y + _dot(a, _to_block_diag(u, bd_mask)) for y, a, u in zip(ys, a_rb, us)]
    upds = [_dot_tn(jnp.concatenate([u, c['v']], axis=0), jnp.concatenate([c['bt'], c['kt']], axis=0))
            for u, c in zip(us, cs)]
    for n_, (i, d) in enumerate(idx):
        (yb_ref if d else yf_ref)[i] = ys[n_]
        (bb_ref if d else bf_ref)[i] = cs[n_]['bonus']
        if not d:
            g_ref[i] = cs[n_]['gate']
        s_ref[i, d] = cs[n_]['g_last'] * (ss[n_] + jnp.where(bd_mask, upds[n_], 0.0))


def rwkv7_mixer(p3d, n_lat, mu, w_up, w0, a_up, a0, g_up, k_k, k_a, r_k):
    bsz, t, width = p3d.shape
    G = GROUP_W
    n_chunks = t // CHUNK
    n_lat_chunks = n_lat // CHUNK
    mu_pad = _pad_cols(mu, width)
    wup = jnp.zeros((2, LANE, G), F32)
    aup = jnp.zeros((2, LANE, G), F32)
    for d in range(2):
        wup = wup.at[d, d * RWKV_LORA_W:(d + 1) * RWKV_LORA_W].set(w_up[d])
        lo = 2 * RWKV_LORA_W + d * RWKV_LORA_A
        aup = aup.at[d, lo:lo + RWKV_LORA_A].set(a_up[d])
    gup = jnp.zeros((LANE, G), F32).at[:RWKV_LORA_G].set(g_up)
    consts = [mu_pad, wup, w0, aup, a0, gup, k_k.reshape(1, G), k_a.reshape(1, G), r_k.reshape(1, G)]
    nb = math.gcd(SCAN_SAMPLES, bsz)
    ins, f_out, b_out = _natural_scan_specs(nb, width, G, n_lat_chunks, n_chunks, t // SUBLANE - 1)
    out_sds = jax.ShapeDtypeStruct((bsz, t, G), F32)
    yf, bf, gate, yb, bb = pl.pallas_call(
        functools.partial(_rwkv_kernel, n_lat_chunks=n_lat_chunks, n_chunks=n_chunks),
        grid=(bsz // nb, n_chunks),
        in_specs=ins + [_const_spec(c.shape) for c in consts],
        out_specs=[f_out] * 3 + [b_out] * 2,
        out_shape=[out_sds] * 5,
        scratch_shapes=[pltpu.VMEM((nb, 2, G, G), F32)],
        compiler_params=_params("arbitrary", "arbitrary"),
        name="rwkv_scan",
    )(p3d, p3d, p3d, p3d, p3d, p3d, *consts)
    return [a.reshape(bsz * t, G) for a in (yf, yb, bf, bb, gate)]


def _neg_expm1(x):
    series = -x * (1.0 + x * (1 / 2 + x * (1 / 6 + x * (1 / 24 + x * (1 / 120 + x * (1 / 720 + x / 5040))))))
    return jnp.where(x > -0.05, series, 1.0 - jnp.exp(x))


def _lru_chunk(p, prev8, next8, cw_ref, cb_ref, w_ref, bias_ref, lam_ref, h_ref, d, reverse):
    G = GROUP_W
    x = p[:, :G]
    u = (cw_ref[0:1, :] * _shift_rows(x, prev8, next8, 2) + cw_ref[1:2, :] * _shift_rows(x, prev8, next8, 1)
         + cw_ref[2:3, :] * x + cw_ref[3:4, :] * _shift_rows(x, prev8, next8, -1)) + cb_ref[...]
    pre = _dot(u, w_ref[d]) + bias_ref[d:d + 1, :]
    r = jax.nn.sigmoid(pre[:, :G])
    i = jax.nn.sigmoid(pre[:, G:])
    log_a = -LRU_C * r * jax.nn.softplus(-lam_ref[d:d + 1, :])
    a = jnp.exp(log_a)
    b = jnp.sqrt(_neg_expm1(2.0 * log_a)) * (i * u)
    row = lax.broadcasted_iota(jnp.int32, (CHUNK, 1), 0)
    k = 1
    while k < CHUNK:
        shift = CHUNK - k if reverse else k
        valid = (row < CHUNK - k) if reverse else (row >= k)
        b = jnp.where(valid, a * pltpu.roll(b, shift, 0) + b, b)
        a = jnp.where(valid, a * pltpu.roll(a, shift, 0), a)
        k *= 2
    h = b + a * h_ref[0:1, :]
    h_ref[0:1, :] = h[0:1, :] if reverse else h[CHUNK - 1:CHUNK, :]
    return h


def _lru_kernel(pf_ref, pfp_ref, pfn_ref, pb_ref, pbp_ref, pbn_ref, cw_ref, cb_ref, w_ref, bias_ref, lam_ref,
                hf_ref, hb_ref, st_ref, *, n_lat_chunks, n_chunks):
    j = pl.program_id(1)

    @pl.when(j == 0)
    def _():
        st_ref[...] = jnp.zeros_like(st_ref)

    G = GROUP_W
    for i in range(pf_ref.shape[0]):
        first, last = _seq_edges(_fwd_chunk(j, n_lat_chunks, n_chunks), n_lat_chunks, n_chunks)
        prev8 = jnp.where(first, 0.0, pfp_ref[i, :, :G])
        next8 = jnp.where(last, 0.0, pfn_ref[i, :, :G])
        hf_ref[i] = _lru_chunk(pf_ref[i], prev8, next8, cw_ref, cb_ref, w_ref, bias_ref, lam_ref,
                               st_ref.at[i, 0], 0, False)
        first, last = _seq_edges(_bwd_chunk(j, n_lat_chunks, n_chunks), n_lat_chunks, n_chunks)
        prev8 = jnp.where(first, 0.0, pbp_ref[i, :, :G])
        next8 = jnp.where(last, 0.0, pbn_ref[i, :, :G])
        hb_ref[i] = _lru_chunk(pb_ref[i], prev8, next8, cw_ref, cb_ref, w_ref, bias_ref, lam_ref,
                               st_ref.at[i, 1], 1, True)


def rglru_mixer(p3d, n_lat, conv_w, conv_b, w_a, b_a, w_x, b_x, lam):
    bsz, t, width = p3d.shape
    G = GROUP_W
    n_chunks = t // CHUNK
    n_lat_chunks = n_lat // CHUNK
    w = jnp.zeros((2, G, 2 * G), F32)
    for h in range(N_HEADS):
        sl = slice(h * HEAD_DIM, (h + 1) * HEAD_DIM)
        w = w.at[:, sl, sl].set(w_a[:, h])
        w = w.at[:, sl, G + h * HEAD_DIM:G + (h + 1) * HEAD_DIM].set(w_x[:, h])
    consts = [jnp.pad(conv_w, ((0, SUBLANE - CONV_W), (0, 0))), conv_b.reshape(1, G), w,
              jnp.concatenate([b_a, b_x], axis=1), lam]
    nb = math.gcd(SCAN_SAMPLES, bsz)
    ins, f_out, b_out = _natural_scan_specs(nb, width, G, n_lat_chunks, n_chunks, t // SUBLANE - 1)
    h_sds = jax.ShapeDtypeStruct((bsz, t, G), F32)
    hf, hb = pl.pallas_call(
        functools.partial(_lru_kernel, n_lat_chunks=n_lat_chunks, n_chunks=n_chunks),
        grid=(bsz // nb, n_chunks),
        in_specs=ins + [_const_spec(c.shape) for c in consts],
        out_specs=[f_out, b_out],
        out_shape=[h_sds, h_sds],
        scratch_shapes=[pltpu.VMEM((nb, 2, SUBLANE, G), F32)],
        compiler_params=_params("arbitrary", "arbitrary"),
        name="lru_scan",
    )(p3d, p3d, p3d, p3d, p3d, p3d, *consts)
    return hf.reshape(bsz * t, G), hb.reshape(bsz * t, G)


def _gdn_prepare(p, gates, prev, nxt, cw_ref, al_ref, dt_ref, d, reverse):
    G = GROUP_W
    x = p[:, :3 * G].astype(F32)
    prev = prev[:, :3 * G].astype(F32)
    nxt = nxt[:, :3 * G].astype(F32)
    conv = (cw_ref[0:1, :] * _shift_rows(x, prev, nxt, 2) + cw_ref[1:2, :] * _shift_rows(x, prev, nxt, 1)
            + cw_ref[2:3, :] * x + cw_ref[3:4, :] * _shift_rows(x, prev, nxt, -1))
    qkv = conv * jax.nn.sigmoid(conv)
    bd_mask = _same_head((G, G))
    head_ones = jnp.where(bd_mask, 1.0, 0.0).astype(F32)
    q, k, v = qkv[:, :G], qkv[:, G:2 * G], qkv[:, 2 * G:]
    q = q * lax.rsqrt(_dot_sel(q * q, head_ones) + EPS) * (HEAD_DIM ** -0.5)
    k = k * lax.rsqrt(_dot_sel(k * k, head_ones) + EPS)
    la = -al_ref[d:d + 1, :] * jax.nn.softplus(_dot_sel(gates, _head_lane_expander(d * N_HEADS)) + dt_ref[d:d + 1, :])
    beta = jax.nn.sigmoid(_dot_sel(gates, _head_lane_expander((2 + d) * N_HEADS)))
    gam = _sel_dot(_cum_matrix(reverse), la)
    strict, incl, diag = _token_masks(reverse)
    gam_row = jnp.sum(jnp.where(diag, gam, 0.0), axis=0, keepdims=True)
    decay = jnp.exp(jnp.where(incl, gam - gam_row, -jnp.inf))
    k_beta = k * beta
    k_bd = _to_block_diag(k, bd_mask)
    lower = jnp.where(strict, _dot_nt(k_beta, k_bd) * decay, 0.0)
    e_gam = jnp.exp(gam)
    gam_last = gam[0:1, :] if reverse else gam[CHUNK - 1:CHUNK, :]
    return dict(lower=lower, vb=v * beta, kbe=k_beta * e_gam, qk=_dot_nt(q, k_bd) * decay, q_dec=q * e_gam,
                k_dec=k * jnp.exp(gam_last - gam), g_last=jnp.exp(gam_last))


def _gdn_kernel(pf_ref, gf_ref, pfp_ref, pfn_ref, pb_ref, gb_ref, pbp_ref, pbn_ref, cw_ref, al_ref, dt_ref,
                s0_ref, of_ref, ob_ref, s_ref, *, n_steps):
    j = pl.program_id(1)

    @pl.when(j == 0)
    def _():
        s_ref[...] = s0_ref[...]

    bd_mask = _same_head((GROUP_W, GROUP_W))
    cs, ss = [], []
    for i in range(pf_ref.shape[0]):
        prev = jnp.where(j == 0, jnp.zeros((), BF16), pfp_ref[i])
        nxt = jnp.where(j == n_steps - 1, jnp.zeros((), BF16), pfn_ref[i])
        cs.append(_gdn_prepare(pf_ref[i], gf_ref[i], prev, nxt, cw_ref, al_ref, dt_ref, 0, False))
        prev = jnp.where(j == n_steps - 1, jnp.zeros((), BF16), pbp_ref[i])
        nxt = jnp.where(j == 0, jnp.zeros((), BF16), pbn_ref[i])
        cs.append(_gdn_prepare(pb_ref[i], gb_ref[i], prev, nxt, cw_ref, al_ref, dt_ref, 1, True))
        ss += [s_ref[i, 0], s_ref[i, 1]]
    tinv = _unit_tri_inverse([c['lower'] for c in cs], bd_mask)
    us = [_dot(t, _to_block_diag(c['vb'], bd_mask)) for t, c in zip(tinv, cs)]
    ws = [_dot(t, _to_block_diag(c['kbe'], bd_mask)) for t, c in zip(tinv, cs)]
    vn = [u - _dot(w, s) for u, w, s in zip(us, ws, ss)]
    outs = [_dot(c['q_dec'], s) + _dot(c['qk'], _to_block_diag(v, bd_mask)) for c, s, v in zip(cs, ss, vn)]
    sn = [c['g_last'] * s + jnp.where(bd_mask, _dot_tn(c['k_dec'], v), 0.0) for c, s, v in zip(cs, ss, vn)]
    for i in range(pf_ref.shape[0]):
        of_ref[i] = outs[2 * i].astype(of_ref.dtype)
        ob_ref[i] = outs[2 * i + 1].astype(ob_ref.dtype)
        s_ref[i, 0] = sn[2 * i]
        s_ref[i, 1] = sn[2 * i + 1]


def gdn_scan(p, gates, n_steps, first_row_block, column_view, last_halo, cw, al, dt, s0):
    bsz = p.shape[0]
    G = GROUP_W
    nb = math.gcd(SCAN_SAMPLES, bsz)
    f_in, b_in = _chunk_specs(nb, 4 * G, n_steps, first_row_block, column_view)
    f_g, b_g = _chunk_specs(nb, LANE, n_steps, first_row_block, column_view)
    fp, fn, bp, bn = _halo_specs(nb, 4 * G, n_steps, first_row_block, column_view, last_halo)
    f_out, b_out = _chunk_specs(nb, G, n_steps, 0, column_view)
    h_sds = jax.ShapeDtypeStruct((bsz, CHUNK, GRID_W * G) if column_view else (bsz, n_steps * CHUNK, G), BF16)
    state_spec = pl.BlockSpec((nb, 2, G, G), lambda b, j: (b, 0, 0, 0))
    return pl.pallas_call(
        functools.partial(_gdn_kernel, n_steps=n_steps),
        grid=(bsz // nb, n_steps),
        in_specs=[f_in, f_g, fp, fn, b_in, b_g, bp, bn, _const_spec(cw.shape), _const_spec(al.shape),
                  _const_spec(dt.shape), state_spec],
        out_specs=[f_out, b_out, state_spec],
        out_shape=[h_sds, h_sds, jax.ShapeDtypeStruct(s0.shape, F32)],
        compiler_params=_params("arbitrary", "arbitrary"),
        name="gdn_scan",
    )(p, gates, p, p, p, gates, p, p, cw, al, dt, s0)


def gdn_mixer(p3d, g3d, n_lat, conv_w, a_log, dt_bias):
    bsz, t, width = p3d.shape
    G = GROUP_W
    n_ctx = t - n_lat
    assert n_lat == CHUNK * GRID_W
    al = jnp.repeat(jnp.exp(a_log), HEAD_DIM, axis=1)
    dt = jnp.repeat(dt_bias, HEAD_DIM, axis=1)
    cw = jnp.pad(conv_w, ((0, SUBLANE - CONV_W), (0, 0)))
    s0 = jnp.zeros((bsz, 2, G, G), F32)
    last_halo = t // HALO_ROWS - 1
    of_c, ob_c, s1 = gdn_scan(p3d, g3d, n_ctx // CHUNK, n_lat // CHUNK, False, last_halo, cw, al, dt, s0)
    of_l, ob_l, _ = gdn_scan(p3d.reshape(bsz, t // GRID_W, GRID_W * width), g3d.reshape(bsz, t // GRID_W, GRID_W * LANE),
                             GRID_W, 0, True, last_halo, cw, al, dt, s1)
    return (of_l.reshape(bsz * n_lat, G), ob_l.reshape(bsz * n_lat, G),
            of_c.reshape(bsz * n_ctx, G), ob_c.reshape(bsz * n_ctx, G))


def _head_max(x):
    lane_head = lax.broadcasted_iota(jnp.int32, x.shape, 1) // HEAD_DIM
    out = jnp.zeros_like(x)
    for h in range(N_HEADS):
        mh = jnp.max(jnp.where(lane_head == h, x, -jnp.inf), axis=-1, keepdims=True)
        out = jnp.where(lane_head == h, mh, out)
    return out


def _mlstm_prepare(p, gates, ib_ref, fb_ref, d, reverse):
    G = GROUP_W
    p = p[:, :3 * G].astype(F32)
    q, k, v = p[:, :G], p[:, G:2 * G] * (HEAD_DIM ** -0.5), p[:, 2 * G:3 * G]
    bd_mask = _same_head((G, G))
    li = _dot_sel(gates, _head_lane_expander(d * N_HEADS)) + ib_ref[d:d + 1, :]
    lf = jax.nn.log_sigmoid(_dot_sel(gates, _head_lane_expander((2 + d) * N_HEADS)) + fb_ref[d:d + 1, :])
    b = _sel_dot(_cum_matrix(reverse), lf)
    _, incl, diag = _token_masks(reverse)
    g = li - b
    g_row = jnp.sum(jnp.where(diag, g, 0.0), axis=0, keepdims=True)
    log_d = jnp.where(incl, b + g_row, -jnp.inf)
    b_last = b[0:1, :] if reverse else b[CHUNK - 1:CHUNK, :]
    log_end = b_last + g
    return dict(q=q, k=k, v=v, b=b, log_d=log_d, m_intra=_head_max(log_d), b_last=b_last, log_end=log_end,
                end_max=jnp.max(log_end, axis=0, keepdims=True), qk=_dot_nt(q, _to_block_diag(k, bd_mask)))


def _mlstm_kernel(pf_ref, gf_ref, pb_ref, gb_ref, ib_ref, fb_ref, c0_ref, nm0_ref, hf_ref, hb_ref, c_ref, nm_ref):
    @pl.when(pl.program_id(1) == 0)
    def _():
        c_ref[...] = c0_ref[...]
        nm_ref[...] = nm0_ref[...]

    nb = pf_ref.shape[0]
    bd_mask = _same_head((GROUP_W, GROUP_W))
    head_ones = jnp.where(bd_mask, 1.0, 0.0).astype(F32)
    idx = [(i, d) for i in range(nb) for d in range(2)]
    cs = [_mlstm_prepare((pb_ref if d else pf_ref)[i], (gb_ref if d else gf_ref)[i], ib_ref, fb_ref, d, bool(d))
          for i, d in idx]
    mems = [c_ref[i, d] for i, d in idx]
    nrms = [nm_ref[i, d, 0:1, :] for i, d in idx]
    m_prevs = [nm_ref[i, d, 1:2, :] for i, d in idx]
    m_inters = [c['b'] + m for c, m in zip(cs, m_prevs)]
    m_ts = [jnp.maximum(mi, c['m_intra']) for mi, c in zip(m_inters, cs)]
    w_inters = [jnp.exp(mi - mt) for mi, mt in zip(m_inters, m_ts)]
    pms = [jnp.exp(c['log_d'] - mt) * c['qk'] for c, mt in zip(cs, m_ts)]
    inter = [_dot_nt(c['q'], mem) for c, mem in zip(cs, mems)]
    nums = [_dot(pm, _to_block_diag(c['v'], bd_mask)) + w * it for pm, c, w, it in zip(pms, cs, w_inters, inter)]
    dens = [_dot_sel(pm + w * (c['q'] * nrm), head_ones) for pm, w, c, nrm in zip(pms, w_inters, cs, nrms)]
    hs = [num / jnp.maximum(jnp.abs(den), jnp.exp(-mt)) for num, den, mt in zip(nums, dens, m_ts)]
    m_news = [jnp.maximum(c['b_last'] + m, c['end_max']) for c, m in zip(cs, m_prevs)]
    w_prevs = [jnp.exp(c['b_last'] + m - mn) for c, m, mn in zip(cs, m_prevs, m_news)]
    w_news = [jnp.exp(c['log_end'] - mn) for c, mn in zip(cs, m_news)]
    upds = [_dot_tn(wn * c['v'], c['k']) for wn, c in zip(w_news, cs)]
    for n_, (i, d) in enumerate(idx):
        (hb_ref if d else hf_ref)[i] = hs[n_].astype(hf_ref.dtype)
        c_ref[i, d] = w_prevs[n_] * mems[n_] + jnp.where(bd_mask, upds[n_], 0.0)
        nm_ref[i, d, 0:1, :] = w_prevs[n_] * nrms[n_] + jnp.sum(w_news[n_] * cs[n_]['k'], axis=0, keepdims=True)
        nm_ref[i, d, 1:2, :] = m_news[n_]


def mlstm_scan(p, gates, n_steps, first_row_block, column_view, ib, fb, c0, nm0):
    bsz = p.shape[0]
    G = GROUP_W
    nb = math.gcd(SCAN_SAMPLES, bsz)
    f_in, b_in = _chunk_specs(nb, 4 * G, n_steps, first_row_block, column_view)
    f_g, b_g = _chunk_specs(nb, LANE, n_steps, first_row_block, column_view)
    f_out, b_out = _chunk_specs(nb, G, n_steps, 0, column_view)
    h_sds = jax.ShapeDtypeStruct((bsz, CHUNK, GRID_W * G) if column_view else (bsz, n_steps * CHUNK, G), BF16)
    state_specs = [pl.BlockSpec((nb, 2, G, G), lambda b, j: (b, 0, 0, 0)),
                   pl.BlockSpec((nb, 2, SUBLANE, G), lambda b, j: (b, 0, 0, 0))]
    return pl.pallas_call(
        _mlstm_kernel,
        grid=(bsz // nb, n_steps),
        in_specs=[f_in, f_g, b_in, b_g, _const_spec(ib.shape), _const_spec(fb.shape)] + state_specs,
        out_specs=[f_out, b_out] + state_specs,
        out_shape=[h_sds, h_sds, jax.ShapeDtypeStruct(c0.shape, F32), jax.ShapeDtypeStruct(nm0.shape, F32)],
        compiler_params=_params("arbitrary", "arbitrary"),
        name="mlstm_scan",
    )(p, gates, p, gates, ib, fb, c0, nm0)


def mlstm_mixer(p3d, g3d, n_lat, i_b, f_b):
    bsz, t, width = p3d.shape
    G = GROUP_W
    n_ctx = t - n_lat
    assert n_lat == CHUNK * GRID_W
    ib = jnp.repeat(i_b, HEAD_DIM, axis=1)
    fb = jnp.repeat(f_b, HEAD_DIM, axis=1)
    c0 = jnp.zeros((bsz, 2, G, G), F32)
    nm0 = jnp.zeros((bsz, 2, SUBLANE, G), F32)
    hf_c, hb_c, c1, nm1 = mlstm_scan(p3d, g3d, n_ctx // CHUNK, n_lat // CHUNK, False, ib, fb, c0, nm0)
    hf_l, hb_l, _, _ = mlstm_scan(p3d.reshape(bsz, t // GRID_W, GRID_W * width),
                                  g3d.reshape(bsz, t // GRID_W, GRID_W * LANE), GRID_W, 0, True, ib, fb, c1, nm1)
    return (hf_l.reshape(bsz * n_lat, G), hb_l.reshape(bsz * n_lat, G),
            hf_c.reshape(bsz * n_ctx, G), hb_c.reshape(bsz * n_ctx, G))


def kernel(x, c, ctx, c_ctx, mod_w, mod_b, norm_mix_g, norm_ffn_g, w_in, w_out, lru_conv_w, lru_conv_b, lru_w_a, lru_b_a, lru_w_x, lru_b_x, lru_lambda, rwkv_mu, rwkv_w_up, rwkv_w0, rwkv_a_up, rwkv_a0, rwkv_g_up, rwkv_k_k, rwkv_k_a, rwkv_r_k, rwkv_ln_g, rwkv_ln_b, mlstm_i_b, mlstm_f_b, mlstm_norm_g, gdn_conv_w, gdn_a_log, gdn_dt_bias, gdn_norm_g, ffn_w_gate, ffn_w_up, ffn_w_down, moe_router, moe_w_gate, moe_w_up, moe_w_down, final_norm_g):
    bsz, seq, d = x.shape
    n_ctx = ctx.shape[1]
    t = n_ctx + seq
    n = bsz * t
    tiles_per_seq = t // ROW_TILE
    assert n_ctx == ROW_TILE and seq % ROW_TILE == 0
    G = GROUP_W
    x2 = jnp.concatenate([x, ctx], axis=1).reshape(n, d)
    mod = jnp.einsum('bd,ldk->lbk', jax.nn.silu(c), mod_w) + mod_b[:, None]
    mod_c = jnp.einsum('d,ldk->lk', jax.nn.silu(c_ctx), mod_w) + mod_b
    mods = jnp.stack([mod, jnp.broadcast_to(mod_c[:, None], mod.shape)], axis=2).reshape(DEPTH, 2 * bsz, 6, 1, d)
    col_groups = ((0, OFF_B), (OFF_B, OFF_C), (OFF_C, OFF_C + 4 * G), (OFF_C + 4 * G, OFF_D),
                  (OFF_D, OFF_D + 4 * G), (OFF_D + 4 * G, IN_COLS))
    w_groups = [jnp.pad(w_in[:, :, lo:hi], ((0, 0), (0, 0), (0, _round_up(hi - lo, LANE) - (hi - lo)))).astype(BF16)
                for lo, hi in col_groups]
    proj_dtypes = (F32, F32, BF16, F32, BF16, F32)
    w_out_bf = w_out.astype(BF16)
    ffn_bf = [w.astype(BF16) for w in (ffn_w_gate, ffn_w_up, ffn_w_down)]
    moe_bf = [w.astype(BF16) for w in (moe_w_gate, moe_w_up, moe_w_down)]
    router_pad = jnp.pad(moe_router, ((0, 0), (0, 0), (0, LANE - N_EXPERTS)))
    is_latent = (jnp.arange(t) < seq)[None, :, None]
    for l in range(DEPTH):
        sh1, sc1, gt1, sh2, sc2, gt2 = [mods[l, :, j] for j in range(6)]
        pa, pb, pc, pcg, pd, pdg = norm_proj(x2, norm_mix_g[l], sc1, sh1, w_groups, proj_dtypes, l, tiles_per_seq)
        lru = rglru_mixer(pa.reshape(bsz, t, -1), seq, lru_conv_w[l], lru_conv_b[l], lru_w_a[l], lru_b_a[l],
                          lru_w_x[l], lru_b_x[l], lru_lambda[l])
        rwkv = rwkv7_mixer(pb.reshape(bsz, t, -1), seq, rwkv_mu[l], rwkv_w_up[l], rwkv_w0[l], rwkv_a_up[l],
                           rwkv_a0[l], rwkv_g_up[l], rwkv_k_k[l], rwkv_k_a[l], rwkv_r_k[l])
        mlstm = mlstm_mixer(pc.reshape(bsz, t, -1), pcg.reshape(bsz, t, -1), seq, mlstm_i_b[l], mlstm_f_b[l])
        gdn = gdn_mixer(pd.reshape(bsz, t, -1), pdg.reshape(bsz, t, -1), seq, gdn_conv_w[l], gdn_a_log[l],
                        gdn_dt_bias[l])
        x2 = out_proj(lru, rwkv, mlstm, gdn, pa, pc, pd, rwkv_ln_g[l], rwkv_ln_b[l], mlstm_norm_g[l],
                      gdn_norm_g[l], w_out_bf, l, x2, gt1, tiles_per_seq)
        if l % 2 == 0:
            x2 = dense_ffn(x2, norm_ffn_g[l], sc2, sh2, gt2, *ffn_bf, l // 2, tiles_per_seq)
        else:
            h2, lg = norm_router(x2, norm_ffn_g[l], sc2, sh2, router_pad, l // 2, tiles_per_seq)
            y = moe_ffn(h2, lg[:, :N_EXPERTS], *moe_bf, l // 2)
            gt_pair = gt2.reshape(bsz, 2, 1, d)
            x2 = (x2.reshape(bsz, t, d)
                  + jnp.where(is_latent, gt_pair[:, 0], gt_pair[:, 1]) * y.reshape(bsz, t, d)).reshape(n, d)
    return final_norm(x2.reshape(bsz, t, d), final_norm_g, seq)
```

```python
import functools
import math

import jax
import jax.numpy as jnp
from jax import lax
from jax.experimental import pallas as pl
from jax.experimental.pallas import tpu as pltpu

D_MODEL = 1024
DEPTH = 4
GRID_W = 64
N_MIXERS = 4
GROUP_W = D_MODEL // N_MIXERS
HEAD_DIM = 64
N_HEADS = GROUP_W // HEAD_DIM
CONV_W = 4
LRU_C = 8.0
RWKV_LORA_W = 32
RWKV_LORA_A = 32
RWKV_LORA_G = 64
RWKV_GN_EPS = 64e-5
CHUNK = 64
D_FF = 2816
N_EXPERTS = 8
TOP_K = 2
D_FF_EXPERT = 1408
MOE_BLOCK = 256
EPS = 1e-6

A_COLS = 2 * GROUP_W
B_COLS = 3 * GROUP_W + 2 * RWKV_LORA_W + 2 * RWKV_LORA_A + RWKV_LORA_G
C_COLS = 4 * GROUP_W + 4 * N_HEADS
D_COLS = 4 * GROUP_W + 4 * N_HEADS
OFF_B = A_COLS
OFF_C = OFF_B + B_COLS
OFF_D = OFF_C + C_COLS
IN_COLS = OFF_D + D_COLS

LANE = 128
SUBLANE = 8
HALO_ROWS = 16
VMEM_LIMIT = 56 * 1024 * 1024
ROW_TILE = 256
SCAN_SAMPLES = 8
MLSTM_SCAN_SAMPLES = 4
F32 = jnp.float32
BF16 = jnp.bfloat16
HI = lax.Precision.HIGHEST


def _round_up(n, m):
    return (n + m - 1) // m * m


def _pad_cols(w, n):
    return jnp.pad(w, ((0, 0), (0, n - w.shape[1])))


def _const_spec(shape):
    nd = len(shape)
    return pl.BlockSpec(shape, lambda *_: (0,) * nd, pipeline_mode=pl.Buffered(1))


def _params(*sem):
    return pltpu.CompilerParams(dimension_semantics=sem, vmem_limit_bytes=VMEM_LIMIT)


def _mod_spec(d, tiles_per_seq):
    def index(i):
        return (2 * (i // tiles_per_seq) + (i % tiles_per_seq) // (tiles_per_seq - 1), 0, 0)
    return pl.BlockSpec((1, 1, d), index)


def _norm_mod(x, g, sc, sh):
    ms = jnp.mean(x * x, axis=-1, keepdims=True)
    return (x * lax.rsqrt(ms + EPS) * g) * (1.0 + sc) + sh


def _dot_dims(a, b, dims, prec):
    if prec is None:
        a, b, prec = a.astype(BF16), b.astype(BF16), None
    return lax.dot_general(a, b, (dims, ((), ())), preferred_element_type=F32, precision=prec)


def _dot(a, b, prec=None):
    return _dot_dims(a, b, ((1,), (0,)), prec)


def _dot_nt(a, b, prec=None):
    return _dot_dims(a, b, ((1,), (1,)), prec)


def _dot_tn(a, b, prec=None):
    return _dot_dims(a, b, ((0,), (0,)), prec)


def _split3(x):
    x1 = x.astype(BF16)
    r = x - x1.astype(F32)
    x2 = r.astype(BF16)
    return x1, x2, (r - x2.astype(F32)).astype(BF16)


def _dot_sel(a, sel):
    s = sel.astype(BF16)
    p1, p2, p3 = [lax.dot_general(p, s, (((1,), (0,)), ((), ())), preferred_element_type=F32) for p in _split3(a)]
    return p1 + p2 + p3


def _sel_dot(sel, b):
    s = sel.astype(BF16)
    p1, p2, p3 = [lax.dot_general(s, p, (((1,), (0,)), ((), ())), preferred_element_type=F32) for p in _split3(b)]
    return p1 + p2 + p3


def _norm_proj_kernel(x_ref, g_ref, sc_ref, sh_ref, *refs, n_out):
    h = _norm_mod(x_ref[...], g_ref[...], sc_ref[0], sh_ref[0]).astype(BF16)
    for w_ref, o_ref in zip(refs[:n_out], refs[n_out:]):
        o_ref[...] = _dot(h, w_ref[...]).astype(o_ref.dtype)


def _layer_spec(w, layer):
    return pl.BlockSpec((None,) + w.shape[1:], lambda *_: (layer, 0, 0), pipeline_mode=pl.Buffered(1))


def norm_proj(x2d, g, sc, sh, ws, out_dtypes, layer, tiles_per_seq):
    n, d = x2d.shape
    tm = ROW_TILE
    mod_spec = _mod_spec(d, tiles_per_seq)
    return pl.pallas_call(
        functools.partial(_norm_proj_kernel, n_out=len(ws)),
        grid=(n // tm,),
        in_specs=[pl.BlockSpec((tm, d), lambda i: (i, 0)), _const_spec((1, d)), mod_spec, mod_spec]
        + [_layer_spec(w, layer) for w in ws],
        out_specs=[pl.BlockSpec((tm, w.shape[2]), lambda i: (i, 0)) for w in ws],
        out_shape=[jax.ShapeDtypeStruct((n, w.shape[2]), dt) for w, dt in zip(ws, out_dtypes)],
        compiler_params=_params("arbitrary"),
        name="norm_proj",
    )(x2d, g.reshape(1, d), sc, sh, *ws)


def _out_proj_kernel(a_hf, a_hb, a_gate, b_yf, b_yb, b_bf, b_bb, b_g, c_fl, c_bl, c_fc, c_bc, c_gate,
                     d_fl, d_bl, d_fc, d_bc, d_gate, lng_ref, lnb_ref, cg_ref, dg_ref, w_ref, x_ref, gt_ref,
                     o_ref, *, tiles_per_seq):
    is_ctx = pl.program_id(0) % tiles_per_seq == tiles_per_seq - 1
    seg_mean = jnp.where(_same_head((GROUP_W, GROUP_W)), 1.0 / HEAD_DIM, 0.0).astype(F32)
    f32 = lambda ref: ref[...].astype(F32)
    ya = (a_hf[...] + a_hb[...]) * jax.nn.gelu(a_gate[...])
    yb = b_yf[...] + b_yb[...]
    mean = _dot_sel(yb, seg_mean)
    var = _dot_sel(jnp.square(yb - mean), seg_mean)
    yb = (yb - mean) * lax.rsqrt(var + RWKV_GN_EPS) * lng_ref[...] + lnb_ref[...]
    yb = (yb + b_bf[...] + b_bb[...]) * b_g[...]
    hc = jnp.where(is_ctx, f32(c_fc) + f32(c_bc), f32(c_fl) + f32(c_bl))
    yc = hc * lax.rsqrt(_dot_sel(hc * hc, seg_mean) + EPS) * cg_ref[...] * jax.nn.sigmoid(f32(c_gate))
    hd = jnp.where(is_ctx, f32(d_fc) + f32(d_bc), f32(d_fl) + f32(d_bl))
    gate = f32(d_gate)
    yd = hd * lax.rsqrt(_dot_sel(hd * hd, seg_mean) + EPS) * dg_ref[...] * (gate * jax.nn.sigmoid(gate))
    y = jnp.concatenate([ya, yb, yc, yd], axis=-1).astype(BF16)
    o_ref[...] = x_ref[...] + gt_ref[0] * _dot(y, w_ref[...])


def out_proj(lru, rwkv, mlstm, gdn, pa, pc, pd, ln_g, ln_b, c_g, d_g, w, layer, x2d, gt, tiles_per_seq):
    n, d = x2d.shape
    G = GROUP_W
    tm = ROW_TILE
    lat_tiles = tiles_per_seq - 1
    row = pl.BlockSpec((tm, G), lambda i: (i, 0))
    lat = pl.BlockSpec((tm, G), lambda i: ((i // tiles_per_seq) * lat_tiles
                                             + jnp.minimum(i % tiles_per_seq, lat_tiles - 1), 0))
    ctx = pl.BlockSpec((tm, G), lambda i: (i // tiles_per_seq, 0))
    col = lambda c: pl.BlockSpec((tm, G), lambda i: (i, c))
    vec = _const_spec((1, G))
    return pl.pallas_call(
        functools.partial(_out_proj_kernel, tiles_per_seq=tiles_per_seq),
        grid=(n // tm,),
        in_specs=[row, row, col(1)] + [row] * 5 + [lat, lat, ctx, ctx, col(3)] * 2 + [vec] * 4
        + [_layer_spec(w, layer), pl.BlockSpec((tm, d), lambda i: (i, 0)), _mod_spec(d, tiles_per_seq)],
        out_specs=pl.BlockSpec((tm, d), lambda i: (i, 0)),
        out_shape=jax.ShapeDtypeStruct((n, d), F32),
        compiler_params=_params("arbitrary"),
        name="out_proj",
    )(*lru, pa, *rwkv, *mlstm, pc, *gdn, pd, ln_g.reshape(1, G), ln_b.reshape(1, G), c_g.reshape(1, G),
      d_g.reshape(1, G), w, x2d, gt)


def _swiglu_chunks(hb, wg_ref, wu_ref, wd_ref, n_chunks, fc):
    acc = None
    for c in range(n_chunks):
        a = _dot(hb, wg_ref[:, c * fc:(c + 1) * fc])
        u = _dot(hb, wu_ref[:, c * fc:(c + 1) * fc])
        z = (a * jax.nn.sigmoid(a) * u).astype(BF16)
        part = _dot(z, wd_ref[c * fc:(c + 1) * fc, :])
        acc = part if acc is None else acc + part
    return acc


def _dense_ffn_kernel(x_ref, g_ref, sc_ref, sh_ref, gt_ref, wg_ref, wu_ref, wd_ref, o_ref, *, n_chunks, fc):
    x = x_ref[...]
    hb = _norm_mod(x, g_ref[...], sc_ref[0], sh_ref[0]).astype(BF16)
    o_ref[...] = x + gt_ref[0] * _swiglu_chunks(hb, wg_ref, wu_ref, wd_ref, n_chunks, fc)


def dense_ffn(x2d, g, sc, sh, gt, wg, wu, wd, layer, tiles_per_seq):
    n, d = x2d.shape
    tm = ROW_TILE
    fc = D_FF // 2
    mod_spec = _mod_spec(d, tiles_per_seq)
    return pl.pallas_call(
        functools.partial(_dense_ffn_kernel, n_chunks=D_FF // fc, fc=fc),
        grid=(n // tm,),
        in_specs=[pl.BlockSpec((tm, d), lambda i: (i, 0)), _const_spec((1, d)), mod_spec, mod_spec, mod_spec,
                  _layer_spec(wg, layer), _layer_spec(wu, layer), _layer_spec(wd, layer)],
        out_specs=pl.BlockSpec((tm, d), lambda i: (i, 0)),
        out_shape=jax.ShapeDtypeStruct((n, d), F32),
        compiler_params=_params("arbitrary"),
        name="dense_ffn",
    )(x2d, g.reshape(1, d), sc, sh, gt, wg, wu, wd)


def _router_kernel(x_ref, g_ref, sc_ref, sh_ref, r_ref, h_ref, l_ref):
    h = _norm_mod(x_ref[...], g_ref[...], sc_ref[0], sh_ref[0])
    h_ref[...] = h.astype(BF16)
    l_ref[...] = _dot(h, r_ref[...], HI)


def norm_router(x2d, g, sc, sh, router_pad, layer, tiles_per_seq):
    n, d = x2d.shape
    tm = ROW_TILE
    mod_spec = _mod_spec(d, tiles_per_seq)
    return pl.pallas_call(
        _router_kernel,
        grid=(n // tm,),
        in_specs=[pl.BlockSpec((tm, d), lambda i: (i, 0)), _const_spec((1, d)), mod_spec, mod_spec,
                  _layer_spec(router_pad, layer)],
        out_specs=[pl.BlockSpec((tm, d), lambda i: (i, 0)), pl.BlockSpec((tm, LANE), lambda i: (i, 0))],
        out_shape=[jax.ShapeDtypeStruct((n, d), BF16), jax.ShapeDtypeStruct((n, LANE), F32)],
        compiler_params=_params("arbitrary"),
        name="norm_router",
    )(x2d, g.reshape(1, d), sc, sh, router_pad)


def _expert_kernel(be_ref, nb_ref, xb_ref, wg_ref, wu_ref, wd_ref, o_ref):
    i = pl.program_id(0)

    @pl.when(i < nb_ref[0])
    def _():
        o_ref[...] = _swiglu_chunks(xb_ref[...], wg_ref.at[0], wu_ref.at[0], wd_ref.at[0], 1, D_FF_EXPERT)

    @pl.when(i >= nb_ref[0])
    def _():
        o_ref[...] = jnp.zeros_like(o_ref)


def expert_ffn(xb, block_e, n_used, wg, wu, wd, layer, tm):
    npad, d = xb.shape
    f = wg.shape[3]
    grid_spec = pltpu.PrefetchScalarGridSpec(
        num_scalar_prefetch=2,
        grid=(npad // tm,),
        in_specs=[pl.BlockSpec((tm, d), lambda i, be, nb: (i, 0)),
                  pl.BlockSpec((None, 1, d, f), lambda i, be, nb: (layer, be[i], 0, 0)),
                  pl.BlockSpec((None, 1, d, f), lambda i, be, nb: (layer, be[i], 0, 0)),
                  pl.BlockSpec((None, 1, f, d), lambda i, be, nb: (layer, be[i], 0, 0))],
        out_specs=pl.BlockSpec((tm, d), lambda i, be, nb: (i, 0)),
    )
    return pl.pallas_call(
        _expert_kernel,
        grid_spec=grid_spec,
        out_shape=jax.ShapeDtypeStruct((npad, d), F32),
        compiler_params=_params("arbitrary"),
        name="expert_ffn",
    )(block_e, n_used, xb, wg, wu, wd)


def moe_ffn(h_bf, logits, wg, wu, wd, layer):
    t, d = h_bf.shape
    top_logits, top_idx = lax.top_k(logits, TOP_K)
    gate = jax.nn.softmax(top_logits, axis=-1)
    tk = t * TOP_K
    onehot = (top_idx.reshape(1, tk) == jnp.arange(N_EXPERTS, dtype=jnp.int32)[:, None]).astype(jnp.int32)
    csum = jnp.cumsum(onehot, axis=1)
    counts = csum[:, -1]
    padded = (counts + MOE_BLOCK - 1) // MOE_BLOCK * MOE_BLOCK
    pad_end = jnp.cumsum(padded)
    pad_start = pad_end - padded
    pos = jnp.sum(onehot * (csum - 1 + pad_start[:, None]), axis=0)
    n_blocks = -(-tk // MOE_BLOCK) + N_EXPERTS
    buf_tok = jnp.zeros((n_blocks * MOE_BLOCK,), jnp.int32).at[pos].set(jnp.arange(tk, dtype=jnp.int32) // TOP_K)
    slot_of = pos.reshape(t, TOP_K)
    block_e = jnp.clip(jnp.searchsorted(pad_end, jnp.arange(n_blocks, dtype=jnp.int32) * MOE_BLOCK,
                                        side='right'), 0, N_EXPERTS - 1).astype(jnp.int32)
    n_used = (pad_end[-1] // MOE_BLOCK).astype(jnp.int32).reshape(1)
    yb = expert_ffn(h_bf[buf_tok], block_e, n_used, wg, wu, wd, layer, MOE_BLOCK)
    return gate[:, 0:1] * yb[slot_of[:, 0]] + gate[:, 1:2] * yb[slot_of[:, 1]]


def _final_norm_kernel(x_ref, g_ref, o_ref):
    x = x_ref[...]
    ms = jnp.mean(x * x, axis=-1, keepdims=True)
    o_ref[...] = x * lax.rsqrt(ms + EPS) * g_ref[...]


def final_norm(x3d, g, n_lat):
    bsz, t, d = x3d.shape
    tm = ROW_TILE
    return pl.pallas_call(
        _final_norm_kernel,
        grid=(bsz, n_lat // tm),
        in_specs=[pl.BlockSpec((None, tm, d), lambda b, i: (b, i, 0)), _const_spec((1, d))],
        out_specs=pl.BlockSpec((None, tm, d), lambda b, i: (b, i, 0)),
        out_shape=jax.ShapeDtypeStruct((bsz, n_lat, d), F32),
        compiler_params=_params("arbitrary", "arbitrary"),
        name="final_norm",
    )(x3d, g.reshape(1, d))


def _same_head(shape):
    r = lax.broadcasted_iota(jnp.int32, shape, 0) // HEAD_DIM
    c = lax.broadcasted_iota(jnp.int32, shape, 1) // HEAD_DIM
    return r == c


def _to_block_diag(x, mask):
    xb = x.astype(BF16)
    return jnp.where(mask, jnp.concatenate([xb] * N_HEADS, axis=0), jnp.zeros((), BF16))


def _token_masks(reverse):
    t = lax.broadcasted_iota(jnp.int32, (CHUNK, GROUP_W), 0)
    s = lax.broadcasted_iota(jnp.int32, (CHUNK, GROUP_W), 1) % HEAD_DIM
    if reverse:
        return s > t, s >= t, s == t
    return s < t, s <= t, s == t


def _cum_matrix(reverse):
    t = lax.broadcasted_iota(jnp.int32, (CHUNK, CHUNK), 0)
    s = lax.broadcasted_iota(jnp.int32, (CHUNK, CHUNK), 1)
    return jnp.where((s >= t) if reverse else (s <= t), 1.0, 0.0).astype(F32)


def _unit_tri_inverse(ls, bd_mask):
    t = lax.broadcasted_iota(jnp.int32, (CHUNK, GROUP_W), 0)
    s = lax.broadcasted_iota(jnp.int32, (CHUNK, GROUP_W), 1) % HEAD_DIM
    blk16 = (t // 16) == (s // 16)
    blk32 = (t // 32) == (s // 32)
    eye = jnp.where(t == s, 1.0, 0.0)
    ps = [jnp.where(blk16, -l, 0.0) for l in ls]
    xs = [eye + p for p in ps]
    for _ in range(3):
        ps = [_dot(p, _to_block_diag(p, bd_mask)) for p in ps]
        xs = [x + _dot(x, _to_block_diag(p, bd_mask)) for x, p in zip(xs, ps)]
    for off_diag in (jnp.logical_and(blk32, jnp.logical_not(blk16)), jnp.logical_not(blk32)):
        xe = [_dot(x, _to_block_diag(jnp.where(off_diag, l, 0.0), bd_mask)) for x, l in zip(xs, ls)]
        xs = [x - _dot(y, _to_block_diag(x, bd_mask)) for x, y in zip(xs, xe)]
    return xs


def _head_lane_expander(col0):
    r = lax.broadcasted_iota(jnp.int32, (LANE, GROUP_W), 0)
    c = lax.broadcasted_iota(jnp.int32, (LANE, GROUP_W), 1) // HEAD_DIM
    return jnp.where(r == col0 + c, 1.0, 0.0).astype(F32)


def _fwd_chunk(j, n_lat_chunks, n_chunks):
    n_ctx_chunks = n_chunks - n_lat_chunks
    return jnp.where(j < n_ctx_chunks, n_lat_chunks + j, j - n_ctx_chunks)


def _bwd_chunk(j, n_lat_chunks, n_chunks):
    return n_chunks - 1 - j


def _seq_edges(c, n_lat_chunks, n_chunks):
    first = jnp.logical_or(c == 0, c == n_lat_chunks)
    last = jnp.logical_or(c == n_lat_chunks - 1, c == n_chunks - 1)
    return first, last


def _shift_rows(x, prev, nxt, back):
    row = lax.broadcasted_iota(jnp.int32, (CHUNK, 1), 0)
    if back == -1:
        return jnp.where(row == CHUNK - 1, nxt[0:1, :], pltpu.roll(x, CHUNK - 1, 0))
    out = pltpu.roll(x, back, 0)
    n_prev = prev.shape[0]
    for i in range(back):
        out = jnp.where(row == i, prev[n_prev - back + i:n_prev - back + i + 1, :], out)
    return out


def _natural_scan_specs(nb, width, out_width, n_lat_chunks, n_chunks, last8):
    r8 = CHUNK // SUBLANE
    fwd = functools.partial(_fwd_chunk, n_lat_chunks=n_lat_chunks, n_chunks=n_chunks)
    bwd = functools.partial(_bwd_chunk, n_lat_chunks=n_lat_chunks, n_chunks=n_chunks)
    chunk = lambda f, w: pl.BlockSpec((nb, CHUNK, w), lambda b, j: (b, f(j), 0))
    prev = lambda f: pl.BlockSpec((nb, SUBLANE, width), lambda b, j: (b, jnp.maximum(f(j) * r8 - 1, 0), 0))
    nxt = lambda f: pl.BlockSpec((nb, SUBLANE, width), lambda b, j: (b, jnp.minimum((f(j) + 1) * r8, last8), 0))
    ins = [chunk(fwd, width), prev(fwd), nxt(fwd), chunk(bwd, width), prev(bwd), nxt(bwd)]
    return ins, chunk(fwd, out_width), chunk(bwd, out_width)


def _chunk_specs(nb, width, n_steps, first_row_block, column_view):
    if column_view:
        return (pl.BlockSpec((nb, CHUNK, width), lambda b, j: (b, 0, j)),
                pl.BlockSpec((nb, CHUNK, width), lambda b, j: (b, 0, n_steps - 1 - j)))
    return (pl.BlockSpec((nb, CHUNK, width), lambda b, j: (b, first_row_block + j, 0)),
            pl.BlockSpec((nb, CHUNK, width), lambda b, j: (b, first_row_block + n_steps - 1 - j, 0)))


def _halo_specs(nb, width, n_steps, first_row_block, column_view, last_halo):
    rh = CHUNK // HALO_ROWS
    fwd = lambda j: j
    bwd = lambda j: n_steps - 1 - j
    if column_view:
        prev = lambda f: pl.BlockSpec((nb, HALO_ROWS, width), lambda b, j: (b, rh - 1, jnp.maximum(f(j) - 1, 0)))
        nxt = lambda f: pl.BlockSpec((nb, HALO_ROWS, width), lambda b, j: (b, 0, jnp.minimum(f(j) + 1, n_steps - 1)))
    else:
        prev = lambda f: pl.BlockSpec((nb, HALO_ROWS, width),
                                      lambda b, j: (b, jnp.maximum((first_row_block + f(j)) * rh - 1, 0), 0))
        nxt = lambda f: pl.BlockSpec((nb, HALO_ROWS, width),
                                     lambda b, j: (b, jnp.minimum((first_row_block + f(j) + 1) * rh, last_halo), 0))
    return prev(fwd), nxt(fwd), prev(bwd), nxt(bwd)


def _rwkv_prepare(p, prev_row, next_row, prm, d, reverse):
    (mu_ref, wup_ref, w0_ref, aup_ref, a0_ref, gup_ref, kk_ref, ka_ref, rk_ref) = prm
    G = GROUP_W
    row = lax.broadcasted_iota(jnp.int32, (CHUNK, 1), 0)
    prev = jnp.where(row == 0, prev_row, pltpu.roll(p, 1, 0))
    nxt = jnp.where(row == CHUNK - 1, next_row, pltpu.roll(p, CHUNK - 1, 0))
    ps = p + mu_ref[0:1, :] * (prev - p) + mu_ref[1:2, :] * (nxt - p)
    r, k, v = ps[:, :G], ps[:, G:2 * G], ps[:, 2 * G:3 * G]
    lora_in = ps[:, 3 * G:3 * G + LANE]
    gate_in = ps[:, 3 * G + LANE:3 * G + 2 * LANE]
    bd_mask = _same_head((G, G))
    head_ones = jnp.where(bd_mask, 1.0, 0.0).astype(F32)
    log_w = -math.exp(-0.5) * jax.nn.sigmoid(_dot(jnp.tanh(lora_in), wup_ref[d]) + w0_ref[d:d + 1, :])
    a = jax.nn.sigmoid(_dot(lora_in, aup_ref[d]) + a0_ref[d:d + 1, :])
    kr = k * kk_ref[...]
    kk = kr * lax.rsqrt(_dot_sel(kr * kr, head_ones) + EPS)
    k_mod = k * (1.0 + (a - 1.0) * ka_ref[...])
    bonus = _dot_sel(r * k_mod * rk_ref[...], head_ones) * v
    lg = _sel_dot(_cum_matrix(reverse), log_w)
    e_neg = jnp.exp(-lg)
    at = kk * jnp.exp(lg - log_w)
    bt = -(kk * a) * e_neg
    kt = k_mod * e_neg
    rt = r * jnp.exp(lg)
    g_last = jnp.exp(lg[0:1, :] if reverse else lg[CHUNK - 1:CHUNK, :])
    strict, incl, _ = _token_masks(reverse)
    gate = None if reverse else _dot(jax.nn.sigmoid(gate_in), gup_ref[...])
    return dict(at=at, rt=rt, bt=bt, kt=kt, v=v, g_last=g_last, bonus=bonus, gate=gate, strict=strict, incl=incl)


def _rwkv_kernel(pf_ref, pfp_ref, pfn_ref, pb_ref, pbp_ref, pbn_ref,
                 mu_ref, wup_ref, w0_ref, aup_ref, a0_ref, gup_ref, kk_ref, ka_ref, rk_ref,
                 yf_ref, bf_ref, g_ref, yb_ref, bb_ref, s_ref, *, n_lat_chunks, n_chunks):
    j = pl.program_id(1)

    @pl.when(j == 0)
    def _():
        s_ref[...] = jnp.zeros_like(s_ref)

    prm = (mu_ref, wup_ref, w0_ref, aup_ref, a0_ref, gup_ref, kk_ref, ka_ref, rk_ref)
    nb = pf_ref.shape[0]
    bd_mask = _same_head((GROUP_W, GROUP_W))
    idx = [(i, d) for i in range(nb) for d in range(2)]
    cs = []
    for i, d in idx:
        chunk = (_bwd_chunk if d else _fwd_chunk)(j, n_lat_chunks, n_chunks)
        first, last = _seq_edges(chunk, n_lat_chunks, n_chunks)
        p_ref, pp_ref, pn_ref = (pb_ref, pbp_ref, pbn_ref) if d else (pf_ref, pfp_ref, pfn_ref)
        prev_row = jnp.where(first, 0.0, pp_ref[i, SUBLANE - 1:SUBLANE, :])
        next_row = jnp.where(last, 0.0, pn_ref[i, 0:1, :])
        cs.append(_rwkv_prepare(p_ref[i], prev_row, next_row, prm, d, bool(d)))
    ss = [s_ref[i, d] for i, d in idx]
    lhs = [jnp.concatenate([c['at'], c['rt']], axis=0) for c in cs]
    m_b = [_dot_nt(l, _to_block_diag(c['bt'], bd_mask)) for l, c in zip(lhs, cs)]
    m_k = [_dot_nt(l, _to_block_diag(c['kt'], bd_mask)) for l, c in zip(lhs, cs)]
    a_ak = [jnp.where(c['strict'], m[:CHUNK], 0.0) for c, m in zip(cs, m_k)]
    a_rb = [jnp.where(c['incl'], m[CHUNK:], 0.0) for c, m in zip(cs, m_b)]
    a_rk = [jnp.where(c['incl'], m[CHUNK:], 0.0) for c, m in zip(cs, m_k)]
    tinv = _unit_tri_inverse([jnp.where(c['strict'], -m[:CHUNK], 0.0) for c, m in zip(cs, m_b)], bd_mask)
    v_bd = [_to_block_diag(c['v'], bd_mask) for c in cs]
    rhs = [_dot_nt(c['at'], s) + _dot(a, vb) for c, s, a, vb in zip(cs, ss, a_ak, v_bd)]
    us = [_dot(t, _to_block_diag(r, bd_mask)) for t, r in zip(tinv, rhs)]
    ys = [_dot_nt(c['rt'], s) + _dot(a, vb) for c, s, a, vb in zip(cs, ss, a_rk, v_bd)]
    ys = [y + _dot(a, _to_block_diag(u, bd_mask)) for y, a, u in zip(ys, a_rb, us)]
    upds = [_dot_tn(jnp.concatenate([u, c['v']], axis=0), jnp.concatenate([c['bt'], c['kt']], axis=0))
            for u, c in zip(us, cs)]
    for n_, (i, d) in enumerate(idx):
        (yb_ref if d else yf_ref)[i] = ys[n_]
        (bb_ref if d else bf_ref)[i] = cs[n_]['bonus']
        if not d:
            g_ref[i] = cs[n_]['gate']
        s_ref[i, d] = cs[n_]['g_last'] * (ss[n_] + jnp.where(bd_mask, upds[n_], 0.0))


def rwkv7_mixer(p3d, n_lat, mu, w_up, w0, a_up, a0, g_up, k_k, k_a, r_k):
    bsz, t, width = p3d.shape
    G = GROUP_W
    n_chunks = t // CHUNK
    n_lat_chunks = n_lat // CHUNK
    mu_pad = _pad_cols(mu, width)
    wup = jnp.zeros((2, LANE, G), F32)
    aup = jnp.zeros((2, LANE, G), F32)
    for d in range(2):
        wup = wup.at[d, d * RWKV_LORA_W:(d + 1) * RWKV_LORA_W].set(w_up[d])
        lo = 2 * RWKV_LORA_W + d * RWKV_LORA_A
        aup = aup.at[d, lo:lo + RWKV_LORA_A].set(a_up[d])
    gup = jnp.zeros((LANE, G), F32).at[:RWKV_LORA_G].set(g_up)
    consts = [mu_pad, wup, w0, aup, a0, gup, k_k.reshape(1, G), k_a.reshape(1, G), r_k.reshape(1, G)]
    nb = math.gcd(SCAN_SAMPLES, bsz)
    ins, f_out, b_out = _natural_scan_specs(nb, width, G, n_lat_chunks, n_chunks, t // SUBLANE - 1)
    out_sds = jax.ShapeDtypeStruct((bsz, t, G), F32)
    yf, bf, gate, yb, bb = pl.pallas_call(
        functools.partial(_rwkv_kernel, n_lat_chunks=n_lat_chunks, n_chunks=n_chunks),
        grid=(bsz // nb, n_chunks),
        in_specs=ins + [_const_spec(c.shape) for c in consts],
        out_specs=[f_out] * 3 + [b_out] * 2,
        out_shape=[out_sds] * 5,
        scratch_shapes=[pltpu.VMEM((nb, 2, G, G), F32)],
        compiler_params=_params("arbitrary", "arbitrary"),
        name="rwkv_scan",
    )(p3d, p3d, p3d, p3d, p3d, p3d, *consts)
    return [a.reshape(bsz * t, G) for a in (yf, yb, bf, bb, gate)]


def _neg_expm1(x):
    series = -x * (1.0 + x * (1 / 2 + x * (1 / 6 + x * (1 / 24 + x * (1 / 120 + x * (1 / 720 + x / 5040))))))
    return jnp.where(x > -0.05, series, 1.0 - jnp.exp(x))


def _lru_chunk(p, prev8, next8, cw_ref, cb_ref, w_ref, bias_ref, lam_ref, h_ref, d, reverse):
    G = GROUP_W
    x = p[:, :G]
    u = (cw_ref[0:1, :] * _shift_rows(x, prev8, next8, 2) + cw_ref[1:2, :] * _shift_rows(x, prev8, next8, 1)
         + cw_ref[2:3, :] * x + cw_ref[3:4, :] * _shift_rows(x, prev8, next8, -1)) + cb_ref[...]
    pre = _dot(u, w_ref[d]) + bias_ref[d:d + 1, :]
    r = jax.nn.sigmoid(pre[:, :G])
    i = jax.nn.sigmoid(pre[:, G:])
    log_a = -LRU_C * r * jax.nn.softplus(-lam_ref[d:d + 1, :])
    a = jnp.exp(log_a)
    b = jnp.sqrt(_neg_expm1(2.0 * log_a)) * (i * u)
    row = lax.broadcasted_iota(jnp.int32, (CHUNK, 1), 0)
    k = 1
    while k < CHUNK:
        shift = CHUNK - k if reverse else k
        valid = (row < CHUNK - k) if reverse else (row >= k)
        b = jnp.where(valid, a * pltpu.roll(b, shift, 0) + b, b)
        a = jnp.where(valid, a * pltpu.roll(a, shift, 0), a)
        k *= 2
    h = b + a * h_ref[0:1, :]
    h_ref[0:1, :] = h[0:1, :] if reverse else h[CHUNK - 1:CHUNK, :]
    return h


def _lru_kernel(pf_ref, pfp_ref, pfn_ref, pb_ref, pbp_ref, pbn_ref, cw_ref, cb_ref, w_ref, bias_ref, lam_ref,
                hf_ref, hb_ref, st_ref, *, n_lat_chunks, n_chunks):
    j = pl.program_id(1)

    @pl.when(j == 0)
    def _():
        st_ref[...] = jnp.zeros_like(st_ref)

    G = GROUP_W
    for i in range(pf_ref.shape[0]):
        first, last = _seq_edges(_fwd_chunk(j, n_lat_chunks, n_chunks), n_lat_chunks, n_chunks)
        prev8 = jnp.where(first, 0.0, pfp_ref[i, :, :G])
        next8 = jnp.where(last, 0.0, pfn_ref[i, :, :G])
        hf_ref[i] = _lru_chunk(pf_ref[i], prev8, next8, cw_ref, cb_ref, w_ref, bias_ref, lam_ref,
                               st_ref.at[i, 0], 0, False)
        first, last = _seq_edges(_bwd_chunk(j, n_lat_chunks, n_chunks), n_lat_chunks, n_chunks)
        prev8 = jnp.where(first, 0.0, pbp_ref[i, :, :G])
        next8 = jnp.where(last, 0.0, pbn_ref[i, :, :G])
        hb_ref[i] = _lru_chunk(pb_ref[i], prev8, next8, cw_ref, cb_ref, w_ref, bias_ref, lam_ref,
                               st_ref.at[i, 1], 1, True)


def rglru_mixer(p3d, n_lat, conv_w, conv_b, w_a, b_a, w_x, b_x, lam):
    bsz, t, width = p3d.shape
    G = GROUP_W
    n_chunks = t // CHUNK
    n_lat_chunks = n_lat // CHUNK
    w = jnp.zeros((2, G, 2 * G), F32)
    for h in range(N_HEADS):
        sl = slice(h * HEAD_DIM, (h + 1) * HEAD_DIM)
        w = w.at[:, sl, sl].set(w_a[:, h])
        w = w.at[:, sl, G + h * HEAD_DIM:G + (h + 1) * HEAD_DIM].set(w_x[:, h])
    consts = [jnp.pad(conv_w, ((0, SUBLANE - CONV_W), (0, 0))), conv_b.reshape(1, G), w,
              jnp.concatenate([b_a, b_x], axis=1), lam]
    nb = math.gcd(SCAN_SAMPLES, bsz)
    ins, f_out, b_out = _natural_scan_specs(nb, width, G, n_lat_chunks, n_chunks, t // SUBLANE - 1)
    h_sds = jax.ShapeDtypeStruct((bsz, t, G), F32)
    hf, hb = pl.pallas_call(
        functools.partial(_lru_kernel, n_lat_chunks=n_lat_chunks, n_chunks=n_chunks),
        grid=(bsz // nb, n_chunks),
        in_specs=ins + [_const_spec(c.shape) for c in consts],
        out_specs=[f_out, b_out],
        out_shape=[h_sds, h_sds],
        scratch_shapes=[pltpu.VMEM((nb, 2, SUBLANE, G), F32)],
        compiler_params=_params("arbitrary", "arbitrary"),
        name="lru_scan",
    )(p3d, p3d, p3d, p3d, p3d, p3d, *consts)
    return hf.reshape(bsz * t, G), hb.reshape(bsz * t, G)


def _gdn_prepare(p, gates, prev, nxt, cw_ref, al_ref, dt_ref, d, reverse):
    G = GROUP_W
    x = p[:, :3 * G].astype(F32)
    prev = prev[:, :3 * G].astype(F32)
    nxt = nxt[:, :3 * G].astype(F32)
    conv = (cw_ref[0:1, :] * _shift_rows(x, prev, nxt, 2) + cw_ref[1:2, :] * _shift_rows(x, prev, nxt, 1)
            + cw_ref[2:3, :] * x + cw_ref[3:4, :] * _shift_rows(x, prev, nxt, -1))
    qkv = conv * jax.nn.sigmoid(conv)
    bd_mask = _same_head((G, G))
    head_ones = jnp.where(bd_mask, 1.0, 0.0).astype(F32)
    q, k, v = qkv[:, :G], qkv[:, G:2 * G], qkv[:, 2 * G:]
    q = q * lax.rsqrt(_dot_sel(q * q, head_ones) + EPS) * (HEAD_DIM ** -0.5)
    k = k * lax.rsqrt(_dot_sel(k * k, head_ones) + EPS)
    la = -al_ref[d:d + 1, :] * jax.nn.softplus(_dot_sel(gates, _head_lane_expander(d * N_HEADS)) + dt_ref[d:d + 1, :])
    beta = jax.nn.sigmoid(_dot_sel(gates, _head_lane_expander((2 + d) * N_HEADS)))
    gam = _sel_dot(_cum_matrix(reverse), la)
    strict, incl, diag = _token_masks(reverse)
    gam_row = jnp.sum(jnp.where(diag, gam, 0.0), axis=0, keepdims=True)
    decay = jnp.exp(jnp.where(incl, gam - gam_row, -jnp.inf))
    k_beta = k * beta
    k_bd = _to_block_diag(k, bd_mask)
    lower = jnp.where(strict, _dot_nt(k_beta, k_bd) * decay, 0.0)
    e_gam = jnp.exp(gam)
    gam_last = gam[0:1, :] if reverse else gam[CHUNK - 1:CHUNK, :]
    return dict(lower=lower, vb=v * beta, kbe=k_beta * e_gam, qk=_dot_nt(q, k_bd) * decay, q_dec=q * e_gam,
                k_dec=k * jnp.exp(gam_last - gam), g_last=jnp.exp(gam_last))


def _gdn_kernel(pf_ref, gf_ref, pfp_ref, pfn_ref, pb_ref, gb_ref, pbp_ref, pbn_ref, cw_ref, al_ref, dt_ref,
                s0_ref, of_ref, ob_ref, s_ref, *, n_steps):
    j = pl.program_id(1)

    @pl.when(j == 0)
    def _():
        s_ref[...] = s0_ref[...]

    bd_mask = _same_head((GROUP_W, GROUP_W))
    cs, ss = [], []
    for i in range(pf_ref.shape[0]):
        prev = jnp.where(j == 0, jnp.zeros((), BF16), pfp_ref[i])
        nxt = jnp.where(j == n_steps - 1, jnp.zeros((), BF16), pfn_ref[i])
        cs.append(_gdn_prepare(pf_ref[i], gf_ref[i], prev, nxt, cw_ref, al_ref, dt_ref, 0, False))
        prev = jnp.where(j == n_steps - 1, jnp.zeros((), BF16), pbp_ref[i])
        nxt = jnp.where(j == 0, jnp.zeros((), BF16), pbn_ref[i])
        cs.append(_gdn_prepare(pb_ref[i], gb_ref[i], prev, nxt, cw_ref, al_ref, dt_ref, 1, True))
        ss += [s_ref[i, 0], s_ref[i, 1]]
    tinv = _unit_tri_inverse([c['lower'] for c in cs], bd_mask)
    us = [_dot(t, _to_block_diag(c['vb'], bd_mask)) for t, c in zip(tinv, cs)]
    ws = [_dot(t, _to_block_diag(c['kbe'], bd_mask)) for t, c in zip(tinv, cs)]
    vn = [u - _dot(w, s) for u, w, s in zip(us, ws, ss)]
    outs = [_dot(c['q_dec'], s) + _dot(c['qk'], _to_block_diag(v, bd_mask)) for c, s, v in zip(cs, ss, vn)]
    sn = [c['g_last'] * s + jnp.where(bd_mask, _dot_tn(c['k_dec'], v), 0.0) for c, s, v in zip(cs, ss, vn)]
    for i in range(pf_ref.shape[0]):
        of_ref[i] = outs[2 * i].astype(of_ref.dtype)
        ob_ref[i] = outs[2 * i + 1].astype(ob_ref.dtype)
        s_ref[i, 0] = sn[2 * i]
        s_ref[i, 1] = sn[2 * i + 1]


def gdn_scan(p, gates, n_steps, first_row_block, column_view, last_halo, cw, al, dt, s0):
    bsz = p.shape[0]
    G = GROUP_W
    nb = math.gcd(SCAN_SAMPLES, bsz)
    f_in, b_in = _chunk_specs(nb, 4 * G, n_steps, first_row_block, column_view)
    f_g, b_g = _chunk_specs(nb, LANE, n_steps, first_row_block, column_view)
    fp, fn, bp, bn = _halo_specs(nb, 4 * G, n_steps, first_row_block, column_view, last_halo)
    f_out, b_out = _chunk_specs(nb, G, n_steps, 0, column_view)
    h_sds = jax.ShapeDtypeStruct((bsz, CHUNK, GRID_W * G) if column_view else (bsz, n_steps * CHUNK, G), BF16)
    state_spec = pl.BlockSpec((nb, 2, G, G), lambda b, j: (b, 0, 0, 0))
    return pl.pallas_call(
        functools.partial(_gdn_kernel, n_steps=n_steps),
        grid=(bsz // nb, n_steps),
        in_specs=[f_in, f_g, fp, fn, b_in, b_g, bp, bn, _const_spec(cw.shape), _const_spec(al.shape),
                  _const_spec(dt.shape), state_spec],
        out_specs=[f_out, b_out, state_spec],
        out_shape=[h_sds, h_sds, jax.ShapeDtypeStruct(s0.shape, F32)],
        compiler_params=_params("arbitrary", "arbitrary"),
        name="gdn_scan",
    )(p, gates, p, p, p, gates, p, p, cw, al, dt, s0)


def gdn_mixer(p3d, g3d, n_lat, conv_w, a_log, dt_bias):
    bsz, t, width = p3d.shape
    G = GROUP_W
    n_ctx = t - n_lat
    assert n_lat == CHUNK * GRID_W
    al = jnp.repeat(jnp.exp(a_log), HEAD_DIM, axis=1)
    dt = jnp.repeat(dt_bias, HEAD_DIM, axis=1)
    cw = jnp.pad(conv_w, ((0, SUBLANE - CONV_W), (0, 0)))
    s0 = jnp.zeros((bsz, 2, G, G), F32)
    last_halo = t // HALO_ROWS - 1
    of_c, ob_c, s1 = gdn_scan(p3d, g3d, n_ctx // CHUNK, n_lat // CHUNK, False, last_halo, cw, al, dt, s0)
    of_l, ob_l, _ = gdn_scan(p3d.reshape(bsz, t // GRID_W, GRID_W * width), g3d.reshape(bsz, t // GRID_W, GRID_W * LANE),
                             GRID_W, 0, True, last_halo, cw, al, dt, s1)
    return (of_l.reshape(bsz * n_lat, G), ob_l.reshape(bsz * n_lat, G),
            of_c.reshape(bsz * n_ctx, G), ob_c.reshape(bsz * n_ctx, G))


def _head_max(x):
    lane_head = lax.broadcasted_iota(jnp.int32, x.shape, 1) // HEAD_DIM
    out = jnp.zeros_like(x)
    for h in range(N_HEADS):
        mh = jnp.max(jnp.where(lane_head == h, x, -jnp.inf), axis=-1, keepdims=True)
        out = jnp.where(lane_head == h, mh, out)
    return out


def _mlstm_prepare(p, gates, ib_ref, fb_ref, d, reverse):
    G = GROUP_W
    p = p[:, :3 * G].astype(F32)
    q, k, v = p[:, :G], p[:, G:2 * G] * (HEAD_DIM ** -0.5), p[:, 2 * G:3 * G]
    bd_mask = _same_head((G, G))
    li = _dot_sel(gates, _head_lane_expander(d * N_HEADS)) + ib_ref[d:d + 1, :]
    lf = jax.nn.log_sigmoid(_dot_sel(gates, _head_lane_expander((2 + d) * N_HEADS)) + fb_ref[d:d + 1, :])
    b = _sel_dot(_cum_matrix(reverse), lf)
    _, incl, diag = _token_masks(reverse)
    g = li - b
    g_row = jnp.sum(jnp.where(diag, g, 0.0), axis=0, keepdims=True)
    log_d = jnp.where(incl, b + g_row, -jnp.inf)
    b_last = b[0:1, :] if reverse else b[CHUNK - 1:CHUNK, :]
    log_end = b_last + g
    return dict(q=q, k=k, v=v, b=b, log_d=log_d, m_intra=_head_max(log_d), b_last=b_last, log_end=log_end,
                end_max=jnp.max(log_end, axis=0, keepdims=True), qk=_dot_nt(q, _to_block_diag(k, bd_mask)))


def _mlstm_kernel(pf_ref, gf_ref, pb_ref, gb_ref, ib_ref, fb_ref, c0_ref, nm0_ref, hf_ref, hb_ref, c_ref, nm_ref):
    @pl.when(pl.program_id(1) == 0)
    def _():
        c_ref[...] = c0_ref[...]
        nm_ref[...] = nm0_ref[...]

    nb = pf_ref.shape[0]
    bd_mask = _same_head((GROUP_W, GROUP_W))
    head_ones = jnp.where(bd_mask, 1.0, 0.0).astype(F32)
    idx = [(i, d) for i in range(nb) for d in range(2)]
    cs = [_mlstm_prepare((pb_ref if d else pf_ref)[i], (gb_ref if d else gf_ref)[i], ib_ref, fb_ref, d, bool(d))
          for i, d in idx]
    mems = [c_ref[i, d] for i, d in idx]
    nrms = [nm_ref[i, d, 0:1, :] for i, d in idx]
    m_prevs = [nm_ref[i, d, 1:2, :] for i, d in idx]
    m_inters = [c['b'] + m for c, m in zip(cs, m_prevs)]
    m_ts = [jnp.maximum(mi, c['m_intra']) for mi, c in zip(m_inters, cs)]
    w_inters = [jnp.exp(mi - mt) for mi, mt in zip(m_inters, m_ts)]
    pms = [jnp.exp(c['log_d'] - mt) * c['qk'] for c, mt in zip(cs, m_ts)]
    inter = [_dot_nt(c['q'], mem) for c, mem in zip(cs, mems)]
    nums = [_dot(pm, _to_block_diag(c['v'], bd_mask)) + w * it for pm, c, w, it in zip(pms, cs, w_inters, inter)]
    dens = [_dot_sel(pm + w * (c['q'] * nrm), head_ones) for pm, w, c, nrm in zip(pms, w_inters, cs, nrms)]
    hs = [num / jnp.maximum(jnp.abs(den), jnp.exp(-mt)) for num, den, mt in zip(nums, dens, m_ts)]
    m_news = [jnp.maximum(c['b_last'] + m, c['end_max']) for c, m in zip(cs, m_prevs)]
    w_prevs = [jnp.exp(c['b_last'] + m - mn) for c, m, mn in zip(cs, m_prevs, m_news)]
    w_news = [jnp.exp(c['log_end'] - mn) for c, mn in zip(cs, m_news)]
    upds = [_dot_tn(wn * c['v'], c['k']) for wn, c in zip(w_news, cs)]
    for n_, (i, d) in enumerate(idx):
        (hb_ref if d else hf_ref)[i] = hs[n_].astype(hf_ref.dtype)
        c_ref[i, d] = w_prevs[n_] * mems[n_] + jnp.where(bd_mask, upds[n_], 0.0)
        nm_ref[i, d, 0:1, :] = w_prevs[n_] * nrms[n_] + jnp.sum(w_news[n_] * cs[n_]['k'], axis=0, keepdims=True)
        nm_ref[i, d, 1:2, :] = m_news[n_]


def mlstm_scan(p, gates, n_steps, first_row_block, column_view, ib, fb, c0, nm0):
    bsz = p.shape[0]
    G = GROUP_W
    nb = math.gcd(MLSTM_SCAN_SAMPLES, bsz)
    f_in, b_in = _chunk_specs(nb, 4 * G, n_steps, first_row_block, column_view)
    f_g, b_g = _chunk_specs(nb, LANE, n_steps, first_row_block, column_view)
    f_out, b_out = _chunk_specs(nb, G, n_steps, 0, column_view)
    h_sds = jax.ShapeDtypeStruct((bsz, CHUNK, GRID_W * G) if column_view else (bsz, n_steps * CHUNK, G), BF16)
    state_specs = [pl.BlockSpec((nb, 2, G, G), lambda b, j: (b, 0, 0, 0)),
                   pl.BlockSpec((nb, 2, SUBLANE, G), lambda b, j: (b, 0, 0, 0))]
    return pl.pallas_call(
        _mlstm_kernel,
        grid=(bsz // nb, n_steps),
        in_specs=[f_in, f_g, b_in, b_g, _const_spec(ib.shape), _const_spec(fb.shape)] + state_specs,
        out_specs=[f_out, b_out] + state_specs,
        out_shape=[h_sds, h_sds, jax.ShapeDtypeStruct(c0.shape, F32), jax.ShapeDtypeStruct(nm0.shape, F32)],
        compiler_params=_params("arbitrary", "arbitrary"),
        name="mlstm_scan",
    )(p, gates, p, gates, ib, fb, c0, nm0)


def mlstm_mixer(p3d, g3d, n_lat, i_b, f_b):
    bsz, t, width = p3d.shape
    G = GROUP_W
    n_ctx = t - n_lat
    assert n_lat == CHUNK * GRID_W
    ib = jnp.repeat(i_b, HEAD_DIM, axis=1)
    fb = jnp.repeat(f_b, HEAD_DIM, axis=1)
    c0 = jnp.zeros((bsz, 2, G, G), F32)
    nm0 = jnp.zeros((bsz, 2, SUBLANE, G), F32)
    hf_c, hb_c, c1, nm1 = mlstm_scan(p3d, g3d, n_ctx // CHUNK, n_lat // CHUNK, False, ib, fb, c0, nm0)
    hf_l, hb_l, _, _ = mlstm_scan(p3d.reshape(bsz, t // GRID_W, GRID_W * width),
                                  g3d.reshape(bsz, t // GRID_W, GRID_W * LANE), GRID_W, 0, True, ib, fb, c1, nm1)
    return (hf_l.reshape(bsz * n_lat, G), hb_l.reshape(bsz * n_lat, G),
            hf_c.reshape(bsz * n_ctx, G), hb_c.reshape(bsz * n_ctx, G))


def kernel(x, c, ctx, c_ctx, mod_w, mod_b, norm_mix_g, norm_ffn_g, w_in, w_out, lru_conv_w, lru_conv_b, lru_w_a, lru_b_a, lru_w_x, lru_b_x, lru_lambda, rwkv_mu, rwkv_w_up, rwkv_w0, rwkv_a_up, rwkv_a0, rwkv_g_up, rwkv_k_k, rwkv_k_a, rwkv_r_k, rwkv_ln_g, rwkv_ln_b, mlstm_i_b, mlstm_f_b, mlstm_norm_g, gdn_conv_w, gdn_a_log, gdn_dt_bias, gdn_norm_g, ffn_w_gate, ffn_w_up, ffn_w_down, moe_router, moe_w_gate, moe_w_up, moe_w_down, final_norm_g):
    bsz, seq, d = x.shape
    n_ctx = ctx.shape[1]
    t = n_ctx + seq
    n = bsz * t
    tiles_per_seq = t // ROW_TILE
    assert n_ctx == ROW_TILE and seq % ROW_TILE == 0
    G = GROUP_W
    x2 = jnp.concatenate([x, ctx], axis=1).reshape(n, d)
    mod = jnp.einsum('bd,ldk->lbk', jax.nn.silu(c), mod_w) + mod_b[:, None]
    mod_c = jnp.einsum('d,ldk->lk', jax.nn.silu(c_ctx), mod_w) + mod_b
    mods = jnp.stack([mod, jnp.broadcast_to(mod_c[:, None], mod.shape)], axis=2).reshape(DEPTH, 2 * bsz, 6, 1, d)
    col_groups = ((0, OFF_B), (OFF_B, OFF_C), (OFF_C, OFF_C + 4 * G), (OFF_C + 4 * G, OFF_D),
                  (OFF_D, OFF_D + 4 * G), (OFF_D + 4 * G, IN_COLS))
    w_groups = [jnp.pad(w_in[:, :, lo:hi], ((0, 0), (0, 0), (0, _round_up(hi - lo, LANE) - (hi - lo)))).astype(BF16)
                for lo, hi in col_groups]
    proj_dtypes = (F32, F32, BF16, F32, BF16, F32)
    w_out_bf = w_out.astype(BF16)
    ffn_bf = [w.astype(BF16) for w in (ffn_w_gate, ffn_w_up, ffn_w_down)]
    moe_bf = [w.astype(BF16) for w in (moe_w_gate, moe_w_up, moe_w_down)]
    router_pad = jnp.pad(moe_router, ((0, 0), (0, 0), (0, LANE - N_EXPERTS)))
    is_latent = (jnp.arange(t) < seq)[None, :, None]
    for l in range(DEPTH):
        sh1, sc1, gt1, sh2, sc2, gt2 = [mods[l, :, j] for j in range(6)]
        pa, pb, pc, pcg, pd, pdg = norm_proj(x2, norm_mix_g[l], sc1, sh1, w_groups, proj_dtypes, l, tiles_per_seq)
        lru = rglru_mixer(pa.reshape(bsz, t, -1), seq, lru_conv_w[l], lru_conv_b[l], lru_w_a[l], lru_b_a[l],
                          lru_w_x[l], lru_b_x[l], lru_lambda[l])
        rwkv = rwkv7_mixer(pb.reshape(bsz, t, -1), seq, rwkv_mu[l], rwkv_w_up[l], rwkv_w0[l], rwkv_a_up[l],
                           rwkv_a0[l], rwkv_g_up[l], rwkv_k_k[l], rwkv_k_a[l], rwkv_r_k[l])
        mlstm = mlstm_mixer(pc.reshape(bsz, t, -1), pcg.reshape(bsz, t, -1), seq, mlstm_i_b[l], mlstm_f_b[l])
        gdn = gdn_mixer(pd.reshape(bsz, t, -1), pdg.reshape(bsz, t, -1), seq, gdn_conv_w[l], gdn_a_log[l],
                        gdn_dt_bias[l])
        x2 = out_proj(lru, rwkv, mlstm, gdn, pa, pc, pd, rwkv_ln_g[l], rwkv_ln_b[l], mlstm_norm_g[l],
                      gdn_norm_g[l], w_out_bf, l, x2, gt1, tiles_per_seq)
        if l % 2 == 0:
            x2 = dense_ffn(x2, norm_ffn_g[l], sc2, sh2, gt2, *ffn_bf, l // 2, tiles_per_seq)
        else:
            h2, lg = norm_router(x2, norm_ffn_g[l], sc2, sh2, router_pad, l // 2, tiles_per_seq)
            y = moe_ffn(h2, lg[:, :N_EXPERTS], *moe_bf, l // 2)
            gt_pair = gt2.reshape(bsz, 2, 1, d)
            x2 = (x2.reshape(bsz, t, d)
                  + jnp.where(is_latent, gt_pair[:, 0], gt_pair[:, 1]) * y.reshape(bsz, t, d)).reshape(n, d)
    return final_norm(x2.reshape(bsz, t, d), final_norm_g, seq)
```

```python
import functools
import math

import jax
import jax.numpy as jnp
from jax import lax
from jax.experimental import pallas as pl
from jax.experimental.pallas import tpu as pltpu

D_MODEL = 1024
DEPTH = 4
GRID_W = 64
N_MIXERS = 4
GROUP_W = D_MODEL // N_MIXERS
HEAD_DIM = 64
N_HEADS = GROUP_W // HEAD_DIM
CONV_W = 4
LRU_C = 8.0
RWKV_LORA_W = 32
RWKV_LORA_A = 32
RWKV_LORA_G = 64
RWKV_GN_EPS = 64e-5
CHUNK = 64
D_FF = 2816
N_EXPERTS = 8
TOP_K = 2
D_FF_EXPERT = 1408
MOE_BLOCK = 256
EPS = 1e-6

A_COLS = 2 * GROUP_W
B_COLS = 3 * GROUP_W + 2 * RWKV_LORA_W + 2 * RWKV_LORA_A + RWKV_LORA_G
C_COLS = 4 * GROUP_W + 4 * N_HEADS
D_COLS = 4 * GROUP_W + 4 * N_HEADS
OFF_B = A_COLS
OFF_C = OFF_B + B_COLS
OFF_D = OFF_C + C_COLS
IN_COLS = OFF_D + D_COLS

LANE = 128
SUBLANE = 8
HALO_ROWS = 16
VMEM_LIMIT = 56 * 1024 * 1024
ROW_TILE = 256
SCAN_SAMPLES = 8
MLSTM_SCAN_SAMPLES = 4
F32 = jnp.float32
BF16 = jnp.bfloat16
HI = lax.Precision.HIGHEST


def _round_up(n, m):
    return (n + m - 1) // m * m


def _pad_cols(w, n):
    return jnp.pad(w, ((0, 0), (0, n - w.shape[1])))


def _const_spec(shape):
    nd = len(shape)
    return pl.BlockSpec(shape, lambda *_: (0,) * nd, pipeline_mode=pl.Buffered(1))


def _params(*sem):
    return pltpu.CompilerParams(dimension_semantics=sem, vmem_limit_bytes=VMEM_LIMIT)


def _mod_spec(d, tiles_per_seq):
    def index(i):
        return (2 * (i // tiles_per_seq) + (i % tiles_per_seq) // (tiles_per_seq - 1), 0, 0)
    return pl.BlockSpec((1, 1, d), index)


def _norm_mod(x, g, sc, sh):
    ms = jnp.mean(x * x, axis=-1, keepdims=True)
    return (x * lax.rsqrt(ms + EPS) * g) * (1.0 + sc) + sh


def _dot_dims(a, b, dims, prec):
    if prec is None:
        a, b, prec = a.astype(BF16), b.astype(BF16), None
    return lax.dot_general(a, b, (dims, ((), ())), preferred_element_type=F32, precision=prec)


def _dot(a, b, prec=None):
    return _dot_dims(a, b, ((1,), (0,)), prec)


def _dot_nt(a, b, prec=None):
    return _dot_dims(a, b, ((1,), (1,)), prec)


def _dot_tn(a, b, prec=None):
    return _dot_dims(a, b, ((0,), (0,)), prec)


def _split3(x):
    x1 = x.astype(BF16)
    r = x - x1.astype(F32)
    x2 = r.astype(BF16)
    return x1, x2, (r - x2.astype(F32)).astype(BF16)


def _dot_sel(a, sel):
    s = sel.astype(BF16)
    p1, p2, p3 = [lax.dot_general(p, s, (((1,), (0,)), ((), ())), preferred_element_type=F32) for p in _split3(a)]
    return p1 + p2 + p3


def _sel_dot(sel, b):
    s = sel.astype(BF16)
    p1, p2, p3 = [lax.dot_general(s, p, (((1,), (0,)), ((), ())), preferred_element_type=F32) for p in _split3(b)]
    return p1 + p2 + p3


def _norm_proj_kernel(x_ref, g_ref, sc_ref, sh_ref, *refs, n_out):
    h = _norm_mod(x_ref[...], g_ref[...], sc_ref[0], sh_ref[0]).astype(BF16)
    for w_ref, o_ref in zip(refs[:n_out], refs[n_out:]):
        o_ref[...] = _dot(h, w_ref[...]).astype(o_ref.dtype)


def _layer_spec(w, layer):
    return pl.BlockSpec((None,) + w.shape[1:], lambda *_: (layer, 0, 0), pipeline_mode=pl.Buffered(1))


def norm_proj(x2d, g, sc, sh, ws, out_dtypes, layer, tiles_per_seq):
    n, d = x2d.shape
    tm = ROW_TILE
    mod_spec = _mod_spec(d, tiles_per_seq)
    return pl.pallas_call(
        functools.partial(_norm_proj_kernel, n_out=len(ws)),
        grid=(n // tm,),
        in_specs=[pl.BlockSpec((tm, d), lambda i: (i, 0)), _const_spec((1, d)), mod_spec, mod_spec]
        + [_layer_spec(w, layer) for w in ws],
        out_specs=[pl.BlockSpec((tm, w.shape[2]), lambda i: (i, 0)) for w in ws],
        out_shape=[jax.ShapeDtypeStruct((n, w.shape[2]), dt) for w, dt in zip(ws, out_dtypes)],
        compiler_params=_params("arbitrary"),
        name="norm_proj",
    )(x2d, g.reshape(1, d), sc, sh, *ws)


def _out_proj_kernel(a_hf, a_hb, a_gate, b_yf, b_yb, b_bf, b_bb, b_g, c_fl, c_bl, c_fc, c_bc, c_gate,
                     d_fl, d_bl, d_fc, d_bc, d_gate, lng_ref, lnb_ref, cg_ref, dg_ref, w_ref, x_ref, gt_ref,
                     o_ref, *, tiles_per_seq):
    is_ctx = pl.program_id(0) % tiles_per_seq == tiles_per_seq - 1
    seg_mean = jnp.where(_same_head((GROUP_W, GROUP_W)), 1.0 / HEAD_DIM, 0.0).astype(F32)
    f32 = lambda ref: ref[...].astype(F32)
    ya = (a_hf[...] + a_hb[...]) * jax.nn.gelu(a_gate[...])
    yb = b_yf[...] + b_yb[...]
    mean = _dot_sel(yb, seg_mean)
    var = _dot_sel(jnp.square(yb - mean), seg_mean)
    yb = (yb - mean) * lax.rsqrt(var + RWKV_GN_EPS) * lng_ref[...] + lnb_ref[...]
    yb = (yb + b_bf[...] + b_bb[...]) * b_g[...]
    hc = jnp.where(is_ctx, f32(c_fc) + f32(c_bc), f32(c_fl) + f32(c_bl))
    yc = hc * lax.rsqrt(_dot_sel(hc * hc, seg_mean) + EPS) * cg_ref[...] * jax.nn.sigmoid(f32(c_gate))
    hd = jnp.where(is_ctx, f32(d_fc) + f32(d_bc), f32(d_fl) + f32(d_bl))
    gate = f32(d_gate)
    yd = hd * lax.rsqrt(_dot_sel(hd * hd, seg_mean) + EPS) * dg_ref[...] * (gate * jax.nn.sigmoid(gate))
    y = jnp.concatenate([ya, yb, yc, yd], axis=-1).astype(BF16)
    o_ref[...] = x_ref[...] + gt_ref[0] * _dot(y, w_ref[...])


def out_proj(lru, rwkv, mlstm, gdn, pa, pc, pd, ln_g, ln_b, c_g, d_g, w, layer, x2d, gt, tiles_per_seq):
    n, d = x2d.shape
    G = GROUP_W
    tm = ROW_TILE
    lat_tiles = tiles_per_seq - 1
    row = pl.BlockSpec((tm, G), lambda i: (i, 0))
    lat = pl.BlockSpec((tm, G), lambda i: ((i // tiles_per_seq) * lat_tiles
                                             + jnp.minimum(i % tiles_per_seq, lat_tiles - 1), 0))
    ctx = pl.BlockSpec((tm, G), lambda i: (i // tiles_per_seq, 0))
    col = lambda c: pl.BlockSpec((tm, G), lambda i: (i, c))
    vec = _const_spec((1, G))
    return pl.pallas_call(
        functools.partial(_out_proj_kernel, tiles_per_seq=tiles_per_seq),
        grid=(n // tm,),
        in_specs=[row, row, col(1)] + [row] * 5 + [lat, lat, ctx, ctx, col(3)] * 2 + [vec] * 4
        + [_layer_spec(w, layer), pl.BlockSpec((tm, d), lambda i: (i, 0)), _mod_spec(d, tiles_per_seq)],
        out_specs=pl.BlockSpec((tm, d), lambda i: (i, 0)),
        out_shape=jax.ShapeDtypeStruct((n, d), F32),
        compiler_params=_params("arbitrary"),
        name="out_proj",
    )(*lru, pa, *rwkv, *mlstm, pc, *gdn, pd, ln_g.reshape(1, G), ln_b.reshape(1, G), c_g.reshape(1, G),
      d_g.reshape(1, G), w, x2d, gt)


def _swiglu_chunks(hb, wg_ref, wu_ref, wd_ref, n_chunks, fc):
    acc = None
    for c in range(n_chunks):
        a = _dot(hb, wg_ref[:, c * fc:(c + 1) * fc])
        u = _dot(hb, wu_ref[:, c * fc:(c + 1) * fc])
        z = (a * jax.nn.sigmoid(a) * u).astype(BF16)
        part = _dot(z, wd_ref[c * fc:(c + 1) * fc, :])
        acc = part if acc is None else acc + part
    return acc


def _dense_ffn_kernel(x_ref, g_ref, sc_ref, sh_ref, gt_ref, wg_ref, wu_ref, wd_ref, o_ref, *, n_chunks, fc):
    x = x_ref[...]
    hb = _norm_mod(x, g_ref[...], sc_ref[0], sh_ref[0]).astype(BF16)
    o_ref[...] = x + gt_ref[0] * _swiglu_chunks(hb, wg_ref, wu_ref, wd_ref, n_chunks, fc)


def dense_ffn(x2d, g, sc, sh, gt, wg, wu, wd, layer, tiles_per_seq):
    n, d = x2d.shape
    tm = ROW_TILE
    fc = D_FF // 2
    mod_spec = _mod_spec(d, tiles_per_seq)
    return pl.pallas_call(
        functools.partial(_dense_ffn_kernel, n_chunks=D_FF // fc, fc=fc),
        grid=(n // tm,),
        in_specs=[pl.BlockSpec((tm, d), lambda i: (i, 0)), _const_spec((1, d)), mod_spec, mod_spec, mod_spec,
                  _layer_spec(wg, layer), _layer_spec(wu, layer), _layer_spec(wd, layer)],
        out_specs=pl.BlockSpec((tm, d), lambda i: (i, 0)),
        out_shape=jax.ShapeDtypeStruct((n, d), F32),
        compiler_params=_params("arbitrary"),
        name="dense_ffn",
    )(x2d, g.reshape(1, d), sc, sh, gt, wg, wu, wd)


def _router_kernel(x_ref, g_ref, sc_ref, sh_ref, r_ref, h_ref, l_ref):
    h = _norm_mod(x_ref[...], g_ref[...], sc_ref[0], sh_ref[0])
    h_ref[...] = h.astype(BF16)
    l_ref[...] = _dot(h, r_ref[...], HI)


def norm_router(x2d, g, sc, sh, router_pad, layer, tiles_per_seq):
    n, d = x2d.shape
    tm = ROW_TILE
    mod_spec = _mod_spec(d, tiles_per_seq)
    return pl.pallas_call(
        _router_kernel,
        grid=(n // tm,),
        in_specs=[pl.BlockSpec((tm, d), lambda i: (i, 0)), _const_spec((1, d)), mod_spec, mod_spec,
                  _layer_spec(router_pad, layer)],
        out_specs=[pl.BlockSpec((tm, d), lambda i: (i, 0)), pl.BlockSpec((tm, LANE), lambda i: (i, 0))],
        out_shape=[jax.ShapeDtypeStruct((n, d), BF16), jax.ShapeDtypeStruct((n, LANE), F32)],
        compiler_params=_params("arbitrary"),
        name="norm_router",
    )(x2d, g.reshape(1, d), sc, sh, router_pad)


def _expert_kernel(be_ref, nb_ref, xb_ref, wg_ref, wu_ref, wd_ref, o_ref, wg_bf, wu_bf, wd_bf):
    i = pl.program_id(0)
    new_expert = jnp.logical_or(i == 0, be_ref[i] != be_ref[jnp.maximum(i - 1, 0)])

    @pl.when(jnp.logical_and(new_expert, i < nb_ref[0]))
    def _():
        wg_bf[...] = wg_ref[0].astype(BF16)
        wu_bf[...] = wu_ref[0].astype(BF16)
        wd_bf[...] = wd_ref[0].astype(BF16)

    @pl.when(i < nb_ref[0])
    def _():
        o_ref[...] = _swiglu_chunks(xb_ref[...], wg_bf, wu_bf, wd_bf, 1, D_FF_EXPERT).astype(o_ref.dtype)

    @pl.when(i >= nb_ref[0])
    def _():
        o_ref[...] = jnp.zeros_like(o_ref)


def expert_ffn(xb, block_e, n_used, wg, wu, wd, layer, tm):
    npad, d = xb.shape
    f = wg.shape[3]
    grid_spec = pltpu.PrefetchScalarGridSpec(
        num_scalar_prefetch=2,
        grid=(npad // tm,),
        in_specs=[pl.BlockSpec((tm, d), lambda i, be, nb: (i, 0)),
                  pl.BlockSpec((None, 1, d, f), lambda i, be, nb: (layer, be[i], 0, 0)),
                  pl.BlockSpec((None, 1, d, f), lambda i, be, nb: (layer, be[i], 0, 0)),
                  pl.BlockSpec((None, 1, f, d), lambda i, be, nb: (layer, be[i], 0, 0))],
        out_specs=pl.BlockSpec((tm, d), lambda i, be, nb: (i, 0)),
        scratch_shapes=[pltpu.VMEM((d, f), BF16), pltpu.VMEM((d, f), BF16), pltpu.VMEM((f, d), BF16)],
    )
    return pl.pallas_call(
        _expert_kernel,
        grid_spec=grid_spec,
        out_shape=jax.ShapeDtypeStruct((npad, d), BF16),
        compiler_params=_params("arbitrary"),
        name="expert_ffn",
    )(block_e, n_used, xb, wg, wu, wd)


def moe_ffn(h_bf, logits, wg, wu, wd, layer):
    t, d = h_bf.shape
    top_logits, top_idx = lax.top_k(logits, TOP_K)
    gate = jax.nn.softmax(top_logits, axis=-1)
    tk = t * TOP_K
    onehot = (top_idx.reshape(1, tk) == jnp.arange(N_EXPERTS, dtype=jnp.int32)[:, None]).astype(jnp.int32)
    csum = jnp.cumsum(onehot, axis=1)
    counts = csum[:, -1]
    padded = (counts + MOE_BLOCK - 1) // MOE_BLOCK * MOE_BLOCK
    pad_end = jnp.cumsum(padded)
    pad_start = pad_end - padded
    pos = jnp.sum(onehot * (csum - 1 + pad_start[:, None]), axis=0)
    n_blocks = -(-tk // MOE_BLOCK) + N_EXPERTS
    buf_tok = jnp.zeros((n_blocks * MOE_BLOCK,), jnp.int32).at[pos].set(jnp.arange(tk, dtype=jnp.int32) // TOP_K)
    slot_of = pos.reshape(t, TOP_K)
    block_e = jnp.clip(jnp.searchsorted(pad_end, jnp.arange(n_blocks, dtype=jnp.int32) * MOE_BLOCK,
                                        side='right'), 0, N_EXPERTS - 1).astype(jnp.int32)
    n_used = (pad_end[-1] // MOE_BLOCK).astype(jnp.int32).reshape(1)
    yb = expert_ffn(h_bf[buf_tok], block_e, n_used, wg, wu, wd, layer, MOE_BLOCK)
    return gate[:, 0:1] * yb[slot_of[:, 0]] + gate[:, 1:2] * yb[slot_of[:, 1]]


def _final_norm_kernel(x_ref, g_ref, o_ref):
    x = x_ref[...]
    ms = jnp.mean(x * x, axis=-1, keepdims=True)
    o_ref[...] = x * lax.rsqrt(ms + EPS) * g_ref[...]


def final_norm(x3d, g, n_lat):
    bsz, t, d = x3d.shape
    tm = ROW_TILE
    return pl.pallas_call(
        _final_norm_kernel,
        grid=(bsz, n_lat // tm),
        in_specs=[pl.BlockSpec((None, tm, d), lambda b, i: (b, i, 0)), _const_spec((1, d))],
        out_specs=pl.BlockSpec((None, tm, d), lambda b, i: (b, i, 0)),
        out_shape=jax.ShapeDtypeStruct((bsz, n_lat, d), F32),
        compiler_params=_params("arbitrary", "arbitrary"),
        name="final_norm",
    )(x3d, g.reshape(1, d))


def _same_head(shape):
    r = lax.broadcasted_iota(jnp.int32, shape, 0) // HEAD_DIM
    c = lax.broadcasted_iota(jnp.int32, shape, 1) // HEAD_DIM
    return r == c


def _to_block_diag(x, mask):
    xb = x.astype(BF16)
    return jnp.where(mask, jnp.concatenate([xb] * N_HEADS, axis=0), jnp.zeros((), BF16))


def _token_masks(reverse):
    t = lax.broadcasted_iota(jnp.int32, (CHUNK, GROUP_W), 0)
    s = lax.broadcasted_iota(jnp.int32, (CHUNK, GROUP_W), 1) % HEAD_DIM
    if reverse:
        return s > t, s >= t, s == t
    return s < t, s <= t, s == t


def _cum_matrix(reverse):
    t = lax.broadcasted_iota(jnp.int32, (CHUNK, CHUNK), 0)
    s = lax.broadcasted_iota(jnp.int32, (CHUNK, CHUNK), 1)
    return jnp.where((s >= t) if reverse else (s <= t), 1.0, 0.0).astype(F32)


def _unit_tri_inverse(ls, bd_mask):
    t = lax.broadcasted_iota(jnp.int32, (CHUNK, GROUP_W), 0)
    s = lax.broadcasted_iota(jnp.int32, (CHUNK, GROUP_W), 1) % HEAD_DIM
    blk16 = (t // 16) == (s // 16)
    blk32 = (t // 32) == (s // 32)
    eye = jnp.where(t == s, 1.0, 0.0)
    ps = [jnp.where(blk16, -l, 0.0) for l in ls]
    xs = [eye + p for p in ps]
    for _ in range(3):
        ps = [_dot(p, _to_block_diag(p, bd_mask)) for p in ps]
        xs = [x + _dot(x, _to_block_diag(p, bd_mask)) for x, p in zip(xs, ps)]
    for off_diag in (jnp.logical_and(blk32, jnp.logical_not(blk16)), jnp.logical_not(blk32)):
        xe = [_dot(x, _to_block_diag(jnp.where(off_diag, l, 0.0), bd_mask)) for x, l in zip(xs, ls)]
        xs = [x - _dot(y, _to_block_diag(x, bd_mask)) for x, y in zip(xs, xe)]
    return xs


def _head_lane_expander(col0):
    r = lax.broadcasted_iota(jnp.int32, (LANE, GROUP_W), 0)
    c = lax.broadcasted_iota(jnp.int32, (LANE, GROUP_W), 1) // HEAD_DIM
    return jnp.where(r == col0 + c, 1.0, 0.0).astype(F32)


def _fwd_chunk(j, n_lat_chunks, n_chunks):
    n_ctx_chunks = n_chunks - n_lat_chunks
    return jnp.where(j < n_ctx_chunks, n_lat_chunks + j, j - n_ctx_chunks)


def _bwd_chunk(j, n_lat_chunks, n_chunks):
    return n_chunks - 1 - j


def _seq_edges(c, n_lat_chunks, n_chunks):
    first = jnp.logical_or(c == 0, c == n_lat_chunks)
    last = jnp.logical_or(c == n_lat_chunks - 1, c == n_chunks - 1)
    return first, last


def _shift_rows(x, prev, nxt, back):
    row = lax.broadcasted_iota(jnp.int32, (CHUNK, 1), 0)
    if back == -1:
        return jnp.where(row == CHUNK - 1, nxt[0:1, :], pltpu.roll(x, CHUNK - 1, 0))
    out = pltpu.roll(x, back, 0)
    n_prev = prev.shape[0]
    for i in range(back):
        out = jnp.where(row == i, prev[n_prev - back + i:n_prev - back + i + 1, :], out)
    return out


def _natural_scan_specs(nb, width, out_width, n_lat_chunks, n_chunks, last8):
    r8 = CHUNK // SUBLANE
    fwd = functools.partial(_fwd_chunk, n_lat_chunks=n_lat_chunks, n_chunks=n_chunks)
    bwd = functools.partial(_bwd_chunk, n_lat_chunks=n_lat_chunks, n_chunks=n_chunks)
    chunk = lambda f, w: pl.BlockSpec((nb, CHUNK, w), lambda b, j: (b, f(j), 0))
    prev = lambda f: pl.BlockSpec((nb, SUBLANE, width), lambda b, j: (b, jnp.maximum(f(j) * r8 - 1, 0), 0))
    nxt = lambda f: pl.BlockSpec((nb, SUBLANE, width), lambda b, j: (b, jnp.minimum((f(j) + 1) * r8, last8), 0))
    ins = [chunk(fwd, width), prev(fwd), nxt(fwd), chunk(bwd, width), prev(bwd), nxt(bwd)]
    return ins, chunk(fwd, out_width), chunk(bwd, out_width)


def _chunk_specs(nb, width, n_steps, first_row_block, column_view):
    if column_view:
        return (pl.BlockSpec((nb, CHUNK, width), lambda b, j: (b, 0, j)),
                pl.BlockSpec((nb, CHUNK, width), lambda b, j: (b, 0, n_steps - 1 - j)))
    return (pl.BlockSpec((nb, CHUNK, width), lambda b, j: (b, first_row_block + j, 0)),
            pl.BlockSpec((nb, CHUNK, width), lambda b, j: (b, first_row_block + n_steps - 1 - j, 0)))


def _halo_specs(nb, width, n_steps, first_row_block, column_view, last_halo):
    rh = CHUNK // HALO_ROWS
    fwd = lambda j: j
    bwd = lambda j: n_steps - 1 - j
    if column_view:
        prev = lambda f: pl.BlockSpec((nb, HALO_ROWS, width), lambda b, j: (b, rh - 1, jnp.maximum(f(j) - 1, 0)))
        nxt = lambda f: pl.BlockSpec((nb, HALO_ROWS, width), lambda b, j: (b, 0, jnp.minimum(f(j) + 1, n_steps - 1)))
    else:
        prev = lambda f: pl.BlockSpec((nb, HALO_ROWS, width),
                                      lambda b, j: (b, jnp.maximum((first_row_block + f(j)) * rh - 1, 0), 0))
        nxt = lambda f: pl.BlockSpec((nb, HALO_ROWS, width),
                                     lambda b, j: (b, jnp.minimum((first_row_block + f(j) + 1) * rh, last_halo), 0))
    return prev(fwd), nxt(fwd), prev(bwd), nxt(bwd)


def _rwkv_prepare(p, prev_row, next_row, prm, d, reverse):
    (mu_ref, wup_ref, w0_ref, aup_ref, a0_ref, gup_ref, kk_ref, ka_ref, rk_ref) = prm
    G = GROUP_W
    row = lax.broadcasted_iota(jnp.int32, (CHUNK, 1), 0)
    prev = jnp.where(row == 0, prev_row, pltpu.roll(p, 1, 0))
    nxt = jnp.where(row == CHUNK - 1, next_row, pltpu.roll(p, CHUNK - 1, 0))
    ps = p + mu_ref[0:1, :] * (prev - p) + mu_ref[1:2, :] * (nxt - p)
    r, k, v = ps[:, :G], ps[:, G:2 * G], ps[:, 2 * G:3 * G]
    lora_in = ps[:, 3 * G:3 * G + LANE]
    gate_in = ps[:, 3 * G + LANE:3 * G + 2 * LANE]
    bd_mask = _same_head((G, G))
    head_ones = jnp.where(bd_mask, 1.0, 0.0).astype(F32)
    log_w = -math.exp(-0.5) * jax.nn.sigmoid(_dot(jnp.tanh(lora_in), wup_ref[d]) + w0_ref[d:d + 1, :])
    a = jax.nn.sigmoid(_dot(lora_in, aup_ref[d]) + a0_ref[d:d + 1, :])
    kr = k * kk_ref[...]
    kk = kr * lax.rsqrt(_dot_sel(kr * kr, head_ones) + EPS)
    k_mod = k * (1.0 + (a - 1.0) * ka_ref[...])
    bonus = _dot_sel(r * k_mod * rk_ref[...], head_ones) * v
    lg = _sel_dot(_cum_matrix(reverse), log_w)
    e_neg = jnp.exp(-lg)
    at = kk * jnp.exp(lg - log_w)
    bt = -(kk * a) * e_neg
    kt = k_mod * e_neg
    rt = r * jnp.exp(lg)
    g_last = jnp.exp(lg[0:1, :] if reverse else lg[CHUNK - 1:CHUNK, :])
    strict, incl, _ = _token_masks(reverse)
    gate = None if reverse else _dot(jax.nn.sigmoid(gate_in), gup_ref[...])
    return dict(at=at, rt=rt, bt=bt, kt=kt, v=v, g_last=g_last, bonus=bonus, gate=gate, strict=strict, incl=incl)


def _rwkv_kernel(pf_ref, pfp_ref, pfn_ref, pb_ref, pbp_ref, pbn_ref,
                 mu_ref, wup_ref, w0_ref, aup_ref, a0_ref, gup_ref, kk_ref, ka_ref, rk_ref,
                 yf_ref, bf_ref, g_ref, yb_ref, bb_ref, s_ref, *, n_lat_chunks, n_chunks):
    j = pl.program_id(1)

    @pl.when(j == 0)
    def _():
        s_ref[...] = jnp.zeros_like(s_ref)

    prm = (mu_ref, wup_ref, w0_ref, aup_ref, a0_ref, gup_ref, kk_ref, ka_ref, rk_ref)
    nb = pf_ref.shape[0]
    bd_mask = _same_head((GROUP_W, GROUP_W))
    idx = [(i, d) for i in range(nb) for d in range(2)]
    cs = []
    for i, d in idx:
        chunk = (_bwd_chunk if d else _fwd_chunk)(j, n_lat_chunks, n_chunks)
        first, last = _seq_edges(chunk, n_lat_chunks, n_chunks)
        p_ref, pp_ref, pn_ref = (pb_ref, pbp_ref, pbn_ref) if d else (pf_ref, pfp_ref, pfn_ref)
        prev_row = jnp.where(first, 0.0, pp_ref[i, SUBLANE - 1:SUBLANE, :])
        next_row = jnp.where(last, 0.0, pn_ref[i, 0:1, :])
        cs.append(_rwkv_prepare(p_ref[i], prev_row, next_row, prm, d, bool(d)))
    ss = [s_ref[i, d] for i, d in idx]
    lhs = [jnp.concatenate([c['at'], c['rt']], axis=0) for c in cs]
    m_b = [_dot_nt(l, _to_block_diag(c['bt'], bd_mask)) for l, c in zip(lhs, cs)]
    m_k = [_dot_nt(l, _to_block_diag(c['kt'], bd_mask)) for l, c in zip(lhs, cs)]
    a_ak = [jnp.where(c['strict'], m[:CHUNK], 0.0) for c, m in zip(cs, m_k)]
    a_rb = [jnp.where(c['incl'], m[CHUNK:], 0.0) for c, m in zip(cs, m_b)]
    a_rk = [jnp.where(c['incl'], m[CHUNK:], 0.0) for c, m in zip(cs, m_k)]
    tinv = _unit_tri_inverse([jnp.where(c['strict'], -m[:CHUNK], 0.0) for c, m in zip(cs, m_b)], bd_mask)
    v_bd = [_to_block_diag(c['v'], bd_mask) for c in cs]
    rhs = [_dot_nt(c['at'], s) + _dot(a, vb) for c, s, a, vb in zip(cs, ss, a_ak, v_bd)]
    us = [_dot(t, _to_block_diag(r, bd_mask)) for t, r in zip(tinv, rhs)]
    ys = [_dot_nt(c['rt'], s) + _dot(a, vb) for c, s, a, vb in zip(cs, ss, a_rk, v_bd)]
    ys = [y + _dot(a, _to_block_diag(u, bd_mask)) for y, a, u in zip(ys, a_rb, us)]
    upds = [_dot_tn(jnp.concatenate([u, c['v']], axis=0), jnp.concatenate([c['bt'], c['kt']], axis=0))
            for u, c in zip(us, cs)]
    for n_, (i, d) in enumerate(idx):
        (yb_ref if d else yf_ref)[i] = ys[n_]
        (bb_ref if d else bf_ref)[i] = cs[n_]['bonus']
        if not d:
            g_ref[i] = cs[n_]['gate']
        s_ref[i, d] = cs[n_]['g_last'] * (ss[n_] + jnp.where(bd_mask, upds[n_], 0.0))


def rwkv7_mixer(p3d, n_lat, mu, w_up, w0, a_up, a0, g_up, k_k, k_a, r_k):
    bsz, t, width = p3d.shape
    G = GROUP_W
    n_chunks = t // CHUNK
    n_lat_chunks = n_lat // CHUNK
    mu_pad = _pad_cols(mu, width)
    wup = jnp.zeros((2, LANE, G), F32)
    aup = jnp.zeros((2, LANE, G), F32)
    for d in range(2):
        wup = wup.at[d, d * RWKV_LORA_W:(d + 1) * RWKV_LORA_W].set(w_up[d])
        lo = 2 * RWKV_LORA_W + d * RWKV_LORA_A
        aup = aup.at[d, lo:lo + RWKV_LORA_A].set(a_up[d])
    gup = jnp.zeros((LANE, G), F32).at[:RWKV_LORA_G].set(g_up)
    consts = [mu_pad, wup, w0, aup, a0, gup, k_k.reshape(1, G), k_a.reshape(1, G), r_k.reshape(1, G)]
    nb = math.gcd(SCAN_SAMPLES, bsz)
    ins, f_out, b_out = _natural_scan_specs(nb, width, G, n_lat_chunks, n_chunks, t // SUBLANE - 1)
    out_sds = jax.ShapeDtypeStruct((bsz, t, G), F32)
    yf, bf, gate, yb, bb = pl.pallas_call(
        functools.partial(_rwkv_kernel, n_lat_chunks=n_lat_chunks, n_chunks=n_chunks),
        grid=(bsz // nb, n_chunks),
        in_specs=ins + [_const_spec(c.shape) for c in consts],
        out_specs=[f_out] * 3 + [b_out] * 2,
        out_shape=[out_sds] * 5,
        scratch_shapes=[pltpu.VMEM((nb, 2, G, G), F32)],
        compiler_params=_params("arbitrary", "arbitrary"),
        name="rwkv_scan",
    )(p3d, p3d, p3d, p3d, p3d, p3d, *consts)
    return [a.reshape(bsz * t, G) for a in (yf, yb, bf, bb, gate)]


def _neg_expm1(x):
    series = -x * (1.0 + x * (1 / 2 + x * (1 / 6 + x * (1 / 24 + x * (1 / 120 + x * (1 / 720 + x / 5040))))))
    return jnp.where(x > -0.05, series, 1.0 - jnp.exp(x))


def _lru_chunk(p, prev8, next8, cw_ref, cb_ref, w_ref, bias_ref, lam_ref, h_ref, d, reverse):
    G = GROUP_W
    x = p[:, :G]
    u = (cw_ref[0:1, :] * _shift_rows(x, prev8, next8, 2) + cw_ref[1:2, :] * _shift_rows(x, prev8, next8, 1)
         + cw_ref[2:3, :] * x + cw_ref[3:4, :] * _shift_rows(x, prev8, next8, -1)) + cb_ref[...]
    pre = _dot(u, w_ref[d]) + bias_ref[d:d + 1, :]
    r = jax.nn.sigmoid(pre[:, :G])
    i = jax.nn.sigmoid(pre[:, G:])
    log_a = -LRU_C * r * jax.nn.softplus(-lam_ref[d:d + 1, :])
    a = jnp.exp(log_a)
    b = jnp.sqrt(_neg_expm1(2.0 * log_a)) * (i * u)
    row = lax.broadcasted_iota(jnp.int32, (CHUNK, 1), 0)
    k = 1
    while k < CHUNK:
        shift = CHUNK - k if reverse else k
        valid = (row < CHUNK - k) if reverse else (row >= k)
        b = jnp.where(valid, a * pltpu.roll(b, shift, 0) + b, b)
        a = jnp.where(valid, a * pltpu.roll(a, shift, 0), a)
        k *= 2
    h = b + a * h_ref[0:1, :]
    h_ref[0:1, :] = h[0:1, :] if reverse else h[CHUNK - 1:CHUNK, :]
    return h


def _lru_kernel(pf_ref, pfp_ref, pfn_ref, pb_ref, pbp_ref, pbn_ref, cw_ref, cb_ref, w_ref, bias_ref, lam_ref,
                hf_ref, hb_ref, st_ref, *, n_lat_chunks, n_chunks):
    j = pl.program_id(1)

    @pl.when(j == 0)
    def _():
        st_ref[...] = jnp.zeros_like(st_ref)

    G = GROUP_W
    for i in range(pf_ref.shape[0]):
        first, last = _seq_edges(_fwd_chunk(j, n_lat_chunks, n_chunks), n_lat_chunks, n_chunks)
        prev8 = jnp.where(first, 0.0, pfp_ref[i, :, :G])
        next8 = jnp.where(last, 0.0, pfn_ref[i, :, :G])
        hf_ref[i] = _lru_chunk(pf_ref[i], prev8, next8, cw_ref, cb_ref, w_ref, bias_ref, lam_ref,
                               st_ref.at[i, 0], 0, False)
        first, last = _seq_edges(_bwd_chunk(j, n_lat_chunks, n_chunks), n_lat_chunks, n_chunks)
        prev8 = jnp.where(first, 0.0, pbp_ref[i, :, :G])
        next8 = jnp.where(last, 0.0, pbn_ref[i, :, :G])
        hb_ref[i] = _lru_chunk(pb_ref[i], prev8, next8, cw_ref, cb_ref, w_ref, bias_ref, lam_ref,
                               st_ref.at[i, 1], 1, True)


def rglru_mixer(p3d, n_lat, conv_w, conv_b, w_a, b_a, w_x, b_x, lam):
    bsz, t, width = p3d.shape
    G = GROUP_W
    n_chunks = t // CHUNK
    n_lat_chunks = n_lat // CHUNK
    w = jnp.zeros((2, G, 2 * G), F32)
    for h in range(N_HEADS):
        sl = slice(h * HEAD_DIM, (h + 1) * HEAD_DIM)
        w = w.at[:, sl, sl].set(w_a[:, h])
        w = w.at[:, sl, G + h * HEAD_DIM:G + (h + 1) * HEAD_DIM].set(w_x[:, h])
    consts = [jnp.pad(conv_w, ((0, SUBLANE - CONV_W), (0, 0))), conv_b.reshape(1, G), w,
              jnp.concatenate([b_a, b_x], axis=1), lam]
    nb = math.gcd(SCAN_SAMPLES, bsz)
    ins, f_out, b_out = _natural_scan_specs(nb, width, G, n_lat_chunks, n_chunks, t // SUBLANE - 1)
    h_sds = jax.ShapeDtypeStruct((bsz, t, G), F32)
    hf, hb = pl.pallas_call(
        functools.partial(_lru_kernel, n_lat_chunks=n_lat_chunks, n_chunks=n_chunks),
        grid=(bsz // nb, n_chunks),
        in_specs=ins + [_const_spec(c.shape) for c in consts],
        out_specs=[f_out, b_out],
        out_shape=[h_sds, h_sds],
        scratch_shapes=[pltpu.VMEM((nb, 2, SUBLANE, G), F32)],
        compiler_params=_params("arbitrary", "arbitrary"),
        name="lru_scan",
    )(p3d, p3d, p3d, p3d, p3d, p3d, *consts)
    return hf.reshape(bsz * t, G), hb.reshape(bsz * t, G)


def _gdn_prepare(p, gates, prev, nxt, cw_ref, al_ref, dt_ref, d, reverse):
    G = GROUP_W
    x = p[:, :3 * G].astype(F32)
    prev = prev[:, :3 * G].astype(F32)
    nxt = nxt[:, :3 * G].astype(F32)
    conv = (cw_ref[0:1, :] * _shift_rows(x, prev, nxt, 2) + cw_ref[1:2, :] * _shift_rows(x, prev, nxt, 1)
            + cw_ref[2:3, :] * x + cw_ref[3:4, :] * _shift_rows(x, prev, nxt, -1))
    qkv = conv * jax.nn.sigmoid(conv)
    bd_mask = _same_head((G, G))
    head_ones = jnp.where(bd_mask, 1.0, 0.0).astype(F32)
    q, k, v = qkv[:, :G], qkv[:, G:2 * G], qkv[:, 2 * G:]
    q = q * lax.rsqrt(_dot_sel(q * q, head_ones) + EPS) * (HEAD_DIM ** -0.5)
    k = k * lax.rsqrt(_dot_sel(k * k, head_ones) + EPS)
    la = -al_ref[d:d + 1, :] * jax.nn.softplus(_dot_sel(gates, _head_lane_expander(d * N_HEADS)) + dt_ref[d:d + 1, :])
    beta = jax.nn.sigmoid(_dot_sel(gates, _head_lane_expander((2 + d) * N_HEADS)))
    gam = _sel_dot(_cum_matrix(reverse), la)
    strict, incl, diag = _token_masks(reverse)
    gam_row = jnp.sum(jnp.where(diag, gam, 0.0), axis=0, keepdims=True)
    decay = jnp.exp(jnp.where(incl, gam - gam_row, -jnp.inf))
    k_beta = k * beta
    k_bd = _to_block_diag(k, bd_mask)
    lower = jnp.where(strict, _dot_nt(k_beta, k_bd) * decay, 0.0)
    e_gam = jnp.exp(gam)
    gam_last = gam[0:1, :] if reverse else gam[CHUNK - 1:CHUNK, :]
    return dict(lower=lower, vb=v * beta, kbe=k_beta * e_gam, qk=_dot_nt(q, k_bd) * decay, q_dec=q * e_gam,
                k_dec=k * jnp.exp(gam_last - gam), g_last=jnp.exp(gam_last))


def _gdn_kernel(pf_ref, gf_ref, pfp_ref, pfn_ref, pb_ref, gb_ref, pbp_ref, pbn_ref, cw_ref, al_ref, dt_ref,
                s0_ref, of_ref, ob_ref, s_ref, *, n_steps):
    j = pl.program_id(1)

    @pl.when(j == 0)
    def _():
        s_ref[...] = s0_ref[...]

    bd_mask = _same_head((GROUP_W, GROUP_W))
    cs, ss = [], []
    for i in range(pf_ref.shape[0]):
        prev = jnp.where(j == 0, jnp.zeros((), BF16), pfp_ref[i])
        nxt = jnp.where(j == n_steps - 1, jnp.zeros((), BF16), pfn_ref[i])
        cs.append(_gdn_prepare(pf_ref[i], gf_ref[i], prev, nxt, cw_ref, al_ref, dt_ref, 0, False))
        prev = jnp.where(j == n_steps - 1, jnp.zeros((), BF16), pbp_ref[i])
        nxt = jnp.where(j == 0, jnp.zeros((), BF16), pbn_ref[i])
        cs.append(_gdn_prepare(pb_ref[i], gb_ref[i], prev, nxt, cw_ref, al_ref, dt_ref, 1, True))
        ss += [s_ref[i, 0], s_ref[i, 1]]
    tinv = _unit_tri_inverse([c['lower'] for c in cs], bd_mask)
    us = [_dot(t, _to_block_diag(c['vb'], bd_mask)) for t, c in zip(tinv, cs)]
    ws = [_dot(t, _to_block_diag(c['kbe'], bd_mask)) for t, c in zip(tinv, cs)]
    vn = [u - _dot(w, s) for u, w, s in zip(us, ws, ss)]
    outs = [_dot(c['q_dec'], s) + _dot(c['qk'], _to_block_diag(v, bd_mask)) for c, s, v in zip(cs, ss, vn)]
    sn = [c['g_last'] * s + jnp.where(bd_mask, _dot_tn(c['k_dec'], v), 0.0) for c, s, v in zip(cs, ss, vn)]
    for i in range(pf_ref.shape[0]):
        of_ref[i] = outs[2 * i].astype(of_ref.dtype)
        ob_ref[i] = outs[2 * i + 1].astype(ob_ref.dtype)
        s_ref[i, 0] = sn[2 * i]
        s_ref[i, 1] = sn[2 * i + 1]


def gdn_scan(p, gates, n_steps, first_row_block, column_view, last_halo, cw, al, dt, s0):
    bsz = p.shape[0]
    G = GROUP_W
    nb = math.gcd(SCAN_SAMPLES, bsz)
    f_in, b_in = _chunk_specs(nb, 4 * G, n_steps, first_row_block, column_view)
    f_g, b_g = _chunk_specs(nb, LANE, n_steps, first_row_block, column_view)
    fp, fn, bp, bn = _halo_specs(nb, 4 * G, n_steps, first_row_block, column_view, last_halo)
    f_out, b_out = _chunk_specs(nb, G, n_steps, 0, column_view)
    h_sds = jax.ShapeDtypeStruct((bsz, CHUNK, GRID_W * G) if column_view else (bsz, n_steps * CHUNK, G), BF16)
    state_spec = pl.BlockSpec((nb, 2, G, G), lambda b, j: (b, 0, 0, 0))
    return pl.pallas_call(
        functools.partial(_gdn_kernel, n_steps=n_steps),
        grid=(bsz // nb, n_steps),
        in_specs=[f_in, f_g, fp, fn, b_in, b_g, bp, bn, _const_spec(cw.shape), _const_spec(al.shape),
                  _const_spec(dt.shape), state_spec],
        out_specs=[f_out, b_out, state_spec],
        out_shape=[h_sds, h_sds, jax.ShapeDtypeStruct(s0.shape, F32)],
        compiler_params=_params("arbitrary", "arbitrary"),
        name="gdn_scan",
    )(p, gates, p, p, p, gates, p, p, cw, al, dt, s0)


def gdn_mixer(p3d, g3d, n_lat, conv_w, a_log, dt_bias):
    bsz, t, width = p3d.shape
    G = GROUP_W
    n_ctx = t - n_lat
    assert n_lat == CHUNK * GRID_W
    al = jnp.repeat(jnp.exp(a_log), HEAD_DIM, axis=1)
    dt = jnp.repeat(dt_bias, HEAD_DIM, axis=1)
    cw = jnp.pad(conv_w, ((0, SUBLANE - CONV_W), (0, 0)))
    s0 = jnp.zeros((bsz, 2, G, G), F32)
    last_halo = t // HALO_ROWS - 1
    of_c, ob_c, s1 = gdn_scan(p3d, g3d, n_ctx // CHUNK, n_lat // CHUNK, False, last_halo, cw, al, dt, s0)
    of_l, ob_l, _ = gdn_scan(p3d.reshape(bsz, t // GRID_W, GRID_W * width), g3d.reshape(bsz, t // GRID_W, GRID_W * LANE),
                             GRID_W, 0, True, last_halo, cw, al, dt, s1)
    return (of_l.reshape(bsz * n_lat, G), ob_l.reshape(bsz * n_lat, G),
            of_c.reshape(bsz * n_ctx, G), ob_c.reshape(bsz * n_ctx, G))


def _head_max(x):
    lane_head = lax.broadcasted_iota(jnp.int32, x.shape, 1) // HEAD_DIM
    out = jnp.zeros_like(x)
    for h in range(N_HEADS):
        mh = jnp.max(jnp.where(lane_head == h, x, -jnp.inf), axis=-1, keepdims=True)
        out = jnp.where(lane_head == h, mh, out)
    return out


def _mlstm_prepare(p, gates, ib_ref, fb_ref, d, reverse):
    G = GROUP_W
    p = p[:, :3 * G].astype(F32)
    q, k, v = p[:, :G], p[:, G:2 * G] * (HEAD_DIM ** -0.5), p[:, 2 * G:3 * G]
    bd_mask = _same_head((G, G))
    li = _dot_sel(gates, _head_lane_expander(d * N_HEADS)) + ib_ref[d:d + 1, :]
    lf = jax.nn.log_sigmoid(_dot_sel(gates, _head_lane_expander((2 + d) * N_HEADS)) + fb_ref[d:d + 1, :])
    b = _sel_dot(_cum_matrix(reverse), lf)
    _, incl, diag = _token_masks(reverse)
    g = li - b
    g_row = jnp.sum(jnp.where(diag, g, 0.0), axis=0, keepdims=True)
    log_d = jnp.where(incl, b + g_row, -jnp.inf)
    b_last = b[0:1, :] if reverse else b[CHUNK - 1:CHUNK, :]
    log_end = b_last + g
    return dict(q=q, k=k, v=v, b=b, log_d=log_d, m_intra=_head_max(log_d), b_last=b_last, log_end=log_end,
                end_max=jnp.max(log_end, axis=0, keepdims=True), qk=_dot_nt(q, _to_block_diag(k, bd_mask)))


def _mlstm_kernel(pf_ref, gf_ref, pb_ref, gb_ref, ib_ref, fb_ref, c0_ref, nm0_ref, hf_ref, hb_ref, c_ref, nm_ref):
    @pl.when(pl.program_id(1) == 0)
    def _():
        c_ref[...] = c0_ref[...]
        nm_ref[...] = nm0_ref[...]

    nb = pf_ref.shape[0]
    bd_mask = _same_head((GROUP_W, GROUP_W))
    head_ones = jnp.where(bd_mask, 1.0, 0.0).astype(F32)
    idx = [(i, d) for i in range(nb) for d in range(2)]
    cs = [_mlstm_prepare((pb_ref if d else pf_ref)[i], (gb_ref if d else gf_ref)[i], ib_ref, fb_ref, d, bool(d))
          for i, d in idx]
    mems = [c_ref[i, d] for i, d in idx]
    nrms = [nm_ref[i, d, 0:1, :] for i, d in idx]
    m_prevs = [nm_ref[i, d, 1:2, :] for i, d in idx]
    m_inters = [c['b'] + m for c, m in zip(cs, m_prevs)]
    m_ts = [jnp.maximum(mi, c['m_intra']) for mi, c in zip(m_inters, cs)]
    w_inters = [jnp.exp(mi - mt) for mi, mt in zip(m_inters, m_ts)]
    pms = [jnp.exp(c['log_d'] - mt) * c['qk'] for c, mt in zip(cs, m_ts)]
    inter = [_dot_nt(c['q'], mem) for c, mem in zip(cs, mems)]
    nums = [_dot(pm, _to_block_diag(c['v'], bd_mask)) + w * it for pm, c, w, it in zip(pms, cs, w_inters, inter)]
    dens = [_dot_sel(pm + w * (c['q'] * nrm), head_ones) for pm, w, c, nrm in zip(pms, w_inters, cs, nrms)]
    hs = [num / jnp.maximum(jnp.abs(den), jnp.exp(-mt)) for num, den, mt in zip(nums, dens, m_ts)]
    m_news = [jnp.maximum(c['b_last'] + m, c['end_max']) for c, m in zip(cs, m_prevs)]
    w_prevs = [jnp.exp(c['b_last'] + m - mn) for c, m, mn in zip(cs, m_prevs, m_news)]
    w_news = [jnp.exp(c['log_end'] - mn) for c, mn in zip(cs, m_news)]
    upds = [_dot_tn(wn * c['v'], c['k']) for wn, c in zip(w_news, cs)]
    for n_, (i, d) in enumerate(idx):
        (hb_ref if d else hf_ref)[i] = hs[n_].astype(hf_ref.dtype)
        c_ref[i, d] = w_prevs[n_] * mems[n_] + jnp.where(bd_mask, upds[n_], 0.0)
        nm_ref[i, d, 0:1, :] = w_prevs[n_] * nrms[n_] + jnp.sum(w_news[n_] * cs[n_]['k'], axis=0, keepdims=True)
        nm_ref[i, d, 1:2, :] = m_news[n_]


def mlstm_scan(p, gates, n_steps, first_row_block, column_view, ib, fb, c0, nm0):
    bsz = p.shape[0]
    G = GROUP_W
    nb = math.gcd(MLSTM_SCAN_SAMPLES, bsz)
    f_in, b_in = _chunk_specs(nb, 4 * G, n_steps, first_row_block, column_view)
    f_g, b_g = _chunk_specs(nb, LANE, n_steps, first_row_block, column_view)
    f_out, b_out = _chunk_specs(nb, G, n_steps, 0, column_view)
    h_sds = jax.ShapeDtypeStruct((bsz, CHUNK, GRID_W * G) if column_view else (bsz, n_steps * CHUNK, G), BF16)
    state_specs = [pl.BlockSpec((nb, 2, G, G), lambda b, j: (b, 0, 0, 0)),
                   pl.BlockSpec((nb, 2, SUBLANE, G), lambda b, j: (b, 0, 0, 0))]
    return pl.pallas_call(
        _mlstm_kernel,
        grid=(bsz // nb, n_steps),
        in_specs=[f_in, f_g, b_in, b_g, _const_spec(ib.shape), _const_spec(fb.shape)] + state_specs,
        out_specs=[f_out, b_out] + state_specs,
        out_shape=[h_sds, h_sds, jax.ShapeDtypeStruct(c0.shape, F32), jax.ShapeDtypeStruct(nm0.shape, F32)],
        compiler_params=_params("arbitrary", "arbitrary"),
        name="mlstm_scan",
    )(p, gates, p, gates, ib, fb, c0, nm0)


def mlstm_mixer(p3d, g3d, n_lat, i_b, f_b):
    bsz, t, width = p3d.shape
    G = GROUP_W
    n_ctx = t - n_lat
    assert n_lat == CHUNK * GRID_W
    ib = jnp.repeat(i_b, HEAD_DIM, axis=1)
    fb = jnp.repeat(f_b, HEAD_DIM, axis=1)
    c0 = jnp.zeros((bsz, 2, G, G), F32)
    nm0 = jnp.zeros((bsz, 2, SUBLANE, G), F32)
    hf_c, hb_c, c1, nm1 = mlstm_scan(p3d, g3d, n_ctx // CHUNK, n_lat // CHUNK, False, ib, fb, c0, nm0)
    hf_l, hb_l, _, _ = mlstm_scan(p3d.reshape(bsz, t // GRID_W, GRID_W * width),
                                  g3d.reshape(bsz, t // GRID_W, GRID_W * LANE), GRID_W, 0, True, ib, fb, c1, nm1)
    return (hf_l.reshape(bsz * n_lat, G), hb_l.reshape(bsz * n_lat, G),
            hf_c.reshape(bsz * n_ctx, G), hb_c.reshape(bsz * n_ctx, G))


def kernel(x, c, ctx, c_ctx, mod_w, mod_b, norm_mix_g, norm_ffn_g, w_in, w_out, lru_conv_w, lru_conv_b, lru_w_a, lru_b_a, lru_w_x, lru_b_x, lru_lambda, rwkv_mu, rwkv_w_up, rwkv_w0, rwkv_a_up, rwkv_a0, rwkv_g_up, rwkv_k_k, rwkv_k_a, rwkv_r_k, rwkv_ln_g, rwkv_ln_b, mlstm_i_b, mlstm_f_b, mlstm_norm_g, gdn_conv_w, gdn_a_log, gdn_dt_bias, gdn_norm_g, ffn_w_gate, ffn_w_up, ffn_w_down, moe_router, moe_w_gate, moe_w_up, moe_w_down, final_norm_g):
    bsz, seq, d = x.shape
    n_ctx = ctx.shape[1]
    t = n_ctx + seq
    n = bsz * t
    tiles_per_seq = t // ROW_TILE
    assert n_ctx == ROW_TILE and seq % ROW_TILE == 0
    G = GROUP_W
    x2 = jnp.concatenate([x, ctx], axis=1).reshape(n, d)
    mod = jnp.einsum('bd,ldk->lbk', jax.nn.silu(c), mod_w) + mod_b[:, None]
    mod_c = jnp.einsum('d,ldk->lk', jax.nn.silu(c_ctx), mod_w) + mod_b
    mods = jnp.stack([mod, jnp.broadcast_to(mod_c[:, None], mod.shape)], axis=2).reshape(DEPTH, 2 * bsz, 6, 1, d)
    col_groups = ((0, OFF_B), (OFF_B, OFF_C), (OFF_C, OFF_C + 4 * G), (OFF_C + 4 * G, OFF_D),
                  (OFF_D, OFF_D + 4 * G), (OFF_D + 4 * G, IN_COLS))
    w_groups = [jnp.pad(w_in[:, :, lo:hi], ((0, 0), (0, 0), (0, _round_up(hi - lo, LANE) - (hi - lo)))).astype(BF16)
                for lo, hi in col_groups]
    proj_dtypes = (F32, F32, BF16, F32, BF16, F32)
    w_out_bf = w_out.astype(BF16)
    ffn_bf = [w.astype(BF16) for w in (ffn_w_gate, ffn_w_up, ffn_w_down)]
    router_pad = jnp.pad(moe_router, ((0, 0), (0, 0), (0, LANE - N_EXPERTS)))
    is_latent = (jnp.arange(t) < seq)[None, :, None]
    for l in range(DEPTH):
        sh1, sc1, gt1, sh2, sc2, gt2 = [mods[l, :, j] for j in range(6)]
        pa, pb, pc, pcg, pd, pdg = norm_proj(x2, norm_mix_g[l], sc1, sh1, w_groups, proj_dtypes, l, tiles_per_seq)
        lru = rglru_mixer(pa.reshape(bsz, t, -1), seq, lru_conv_w[l], lru_conv_b[l], lru_w_a[l], lru_b_a[l],
                          lru_w_x[l], lru_b_x[l], lru_lambda[l])
        rwkv = rwkv7_mixer(pb.reshape(bsz, t, -1), seq, rwkv_mu[l], rwkv_w_up[l], rwkv_w0[l], rwkv_a_up[l],
                           rwkv_a0[l], rwkv_g_up[l], rwkv_k_k[l], rwkv_k_a[l], rwkv_r_k[l])
        mlstm = mlstm_mixer(pc.reshape(bsz, t, -1), pcg.reshape(bsz, t, -1), seq, mlstm_i_b[l], mlstm_f_b[l])
        gdn = gdn_mixer(pd.reshape(bsz, t, -1), pdg.reshape(bsz, t, -1), seq, gdn_conv_w[l], gdn_a_log[l],
                        gdn_dt_bias[l])
        x2 = out_proj(lru, rwkv, mlstm, gdn, pa, pc, pd, rwkv_ln_g[l], rwkv_ln_b[l], mlstm_norm_g[l],
                      gdn_norm_g[l], w_out_bf, l, x2, gt1, tiles_per_seq)
        if l % 2 == 0:
            x2 = dense_ffn(x2, norm_ffn_g[l], sc2, sh2, gt2, *ffn_bf, l // 2, tiles_per_seq)
        else:
            h2, lg = norm_router(x2, norm_ffn_g[l], sc2, sh2, router_pad, l // 2, tiles_per_seq)
            y = moe_ffn(h2, lg[:, :N_EXPERTS], moe_w_gate, moe_w_up, moe_w_down, l // 2)
            gt_pair = gt2.reshape(bsz, 2, 1, d)
            x2 = (x2.reshape(bsz, t, d)
                  + jnp.where(is_latent, gt_pair[:, 0], gt_pair[:, 1]) * y.reshape(bsz, t, d)).reshape(n, d)
    return final_norm(x2.reshape(bsz, t, d), final_norm_g, seq)
```

```python
import functools
import math

import jax
import jax.numpy as jnp
from jax import lax
from jax.experimental import pallas as pl
from jax.experimental.pallas import tpu as pltpu

D_MODEL = 1024
DEPTH = 4
GRID_W = 64
N_MIXERS = 4
GROUP_W = D_MODEL // N_MIXERS
HEAD_DIM = 64
N_HEADS = GROUP_W // HEAD_DIM
CONV_W = 4
LRU_C = 8.0
RWKV_LORA_W = 32
RWKV_LORA_A = 32
RWKV_LORA_G = 64
RWKV_GN_EPS = 64e-5
CHUNK = 64
D_FF = 2816
N_EXPERTS = 8
TOP_K = 2
D_FF_EXPERT = 1408
MOE_BLOCK = 256
EPS = 1e-6

A_COLS = 2 * GROUP_W
B_COLS = 3 * GROUP_W + 2 * RWKV_LORA_W + 2 * RWKV_LORA_A + RWKV_LORA_G
C_COLS = 4 * GROUP_W + 4 * N_HEADS
D_COLS = 4 * GROUP_W + 4 * N_HEADS
OFF_B = A_COLS
OFF_C = OFF_B + B_COLS
OFF_D = OFF_C + C_COLS
IN_COLS = OFF_D + D_COLS

LANE = 128
SUBLANE = 8
HALO_ROWS = 16
VMEM_LIMIT = 56 * 1024 * 1024
ROW_TILE = 256
SCAN_SAMPLES = 8
MLSTM_SCAN_SAMPLES = 4
F32 = jnp.float32
BF16 = jnp.bfloat16
HI = lax.Precision.HIGHEST


def _round_up(n, m):
    return (n + m - 1) // m * m


def _pad_cols(w, n):
    return jnp.pad(w, ((0, 0), (0, n - w.shape[1])))


def _const_spec(shape):
    nd = len(shape)
    return pl.BlockSpec(shape, lambda *_: (0,) * nd, pipeline_mode=pl.Buffered(1))


def _params(*sem):
    return pltpu.CompilerParams(dimension_semantics=sem, vmem_limit_bytes=VMEM_LIMIT)


def _mod_spec(d, tiles_per_seq):
    def index(i):
        return (2 * (i // tiles_per_seq) + (i % tiles_per_seq) // (tiles_per_seq - 1), 0, 0)
    return pl.BlockSpec((1, 1, d), index)


def _norm_mod(x, g, sc, sh):
    ms = jnp.mean(x * x, axis=-1, keepdims=True)
    return (x * lax.rsqrt(ms + EPS) * g) * (1.0 + sc) + sh


def _dot_dims(a, b, dims, prec):
    if prec is None:
        a, b, prec = a.astype(BF16), b.astype(BF16), None
    return lax.dot_general(a, b, (dims, ((), ())), preferred_element_type=F32, precision=prec)


def _dot(a, b, prec=None):
    return _dot_dims(a, b, ((1,), (0,)), prec)


def _dot_nt(a, b, prec=None):
    return _dot_dims(a, b, ((1,), (1,)), prec)


def _dot_tn(a, b, prec=None):
    return _dot_dims(a, b, ((0,), (0,)), prec)


def _split3(x):
    x1 = x.astype(BF16)
    r = x - x1.astype(F32)
    x2 = r.astype(BF16)
    return x1, x2, (r - x2.astype(F32)).astype(BF16)


def _dot_sel(a, sel):
    s = sel.astype(BF16)
    p1, p2, p3 = [lax.dot_general(p, s, (((1,), (0,)), ((), ())), preferred_element_type=F32) for p in _split3(a)]
    return p1 + p2 + p3


def _sel_dot(sel, b):
    s = sel.astype(BF16)
    p1, p2, p3 = [lax.dot_general(s, p, (((1,), (0,)), ((), ())), preferred_element_type=F32) for p in _split3(b)]
    return p1 + p2 + p3


def _norm_proj_kernel(x_ref, g_ref, sc_ref, sh_ref, *refs, n_out):
    h = _norm_mod(x_ref[...], g_ref[...], sc_ref[0], sh_ref[0]).astype(BF16)
    for w_ref, o_ref in zip(refs[:n_out], refs[n_out:]):
        o_ref[...] = _dot(h, w_ref[...]).astype(o_ref.dtype)


def _layer_spec(w, layer):
    return pl.BlockSpec((None,) + w.shape[1:], lambda *_: (layer, 0, 0), pipeline_mode=pl.Buffered(1))


def norm_proj(x2d, g, sc, sh, ws, out_dtypes, layer, tiles_per_seq):
    n, d = x2d.shape
    tm = ROW_TILE
    mod_spec = _mod_spec(d, tiles_per_seq)
    return pl.pallas_call(
        functools.partial(_norm_proj_kernel, n_out=len(ws)),
        grid=(n // tm,),
        in_specs=[pl.BlockSpec((tm, d), lambda i: (i, 0)), _const_spec((1, d)), mod_spec, mod_spec]
        + [_layer_spec(w, layer) for w in ws],
        out_specs=[pl.BlockSpec((tm, w.shape[2]), lambda i: (i, 0)) for w in ws],
        out_shape=[jax.ShapeDtypeStruct((n, w.shape[2]), dt) for w, dt in zip(ws, out_dtypes)],
        compiler_params=_params("arbitrary"),
        name="norm_proj",
    )(x2d, g.reshape(1, d), sc, sh, *ws)


def _out_proj_kernel(a_hf, a_hb, a_gate, b_yf, b_yb, b_bf, b_bb, b_g, c_fl, c_bl, c_fc, c_bc, c_gate,
                     d_fl, d_bl, d_fc, d_bc, d_gate, lng_ref, lnb_ref, cg_ref, dg_ref, w_ref, x_ref, gt_ref,
                     o_ref, *, tiles_per_seq):
    is_ctx = pl.program_id(0) % tiles_per_seq == tiles_per_seq - 1
    seg_mean = jnp.where(_same_head((GROUP_W, GROUP_W)), 1.0 / HEAD_DIM, 0.0).astype(F32)
    f32 = lambda ref: ref[...].astype(F32)
    ya = (a_hf[...] + a_hb[...]) * jax.nn.gelu(a_gate[...])
    yb = b_yf[...] + b_yb[...]
    mean = _dot_sel(yb, seg_mean)
    var = _dot_sel(jnp.square(yb - mean), seg_mean)
    yb = (yb - mean) * lax.rsqrt(var + RWKV_GN_EPS) * lng_ref[...] + lnb_ref[...]
    yb = (yb + b_bf[...] + b_bb[...]) * b_g[...]
    hc = jnp.where(is_ctx, f32(c_fc) + f32(c_bc), f32(c_fl) + f32(c_bl))
    yc = hc * lax.rsqrt(_dot_sel(hc * hc, seg_mean) + EPS) * cg_ref[...] * jax.nn.sigmoid(f32(c_gate))
    hd = jnp.where(is_ctx, f32(d_fc) + f32(d_bc), f32(d_fl) + f32(d_bl))
    gate = f32(d_gate)
    yd = hd * lax.rsqrt(_dot_sel(hd * hd, seg_mean) + EPS) * dg_ref[...] * (gate * jax.nn.sigmoid(gate))
    y = jnp.concatenate([ya, yb, yc, yd], axis=-1).astype(BF16)
    o_ref[...] = x_ref[...] + gt_ref[0] * _dot(y, w_ref[...])


def out_proj(lru, rwkv, mlstm, gdn, pa, pc, pd, ln_g, ln_b, c_g, d_g, w, layer, x2d, gt, tiles_per_seq):
    n, d = x2d.shape
    G = GROUP_W
    tm = ROW_TILE
    lat_tiles = tiles_per_seq - 1
    row = pl.BlockSpec((tm, G), lambda i: (i, 0))
    lat = pl.BlockSpec((tm, G), lambda i: ((i // tiles_per_seq) * lat_tiles
                                             + jnp.minimum(i % tiles_per_seq, lat_tiles - 1), 0))
    ctx = pl.BlockSpec((tm, G), lambda i: (i // tiles_per_seq, 0))
    col = lambda c: pl.BlockSpec((tm, G), lambda i: (i, c))
    vec = _const_spec((1, G))
    return pl.pallas_call(
        functools.partial(_out_proj_kernel, tiles_per_seq=tiles_per_seq),
        grid=(n // tm,),
        in_specs=[row, row, col(1)] + [row] * 5 + [lat, lat, ctx, ctx, col(3)] * 2 + [vec] * 4
        + [_layer_spec(w, layer), pl.BlockSpec((tm, d), lambda i: (i, 0)), _mod_spec(d, tiles_per_seq)],
        out_specs=pl.BlockSpec((tm, d), lambda i: (i, 0)),
        out_shape=jax.ShapeDtypeStruct((n, d), F32),
        compiler_params=_params("arbitrary"),
        name="out_proj",
    )(*lru, pa, *rwkv, *mlstm, pc, *gdn, pd, ln_g.reshape(1, G), ln_b.reshape(1, G), c_g.reshape(1, G),
      d_g.reshape(1, G), w, x2d, gt)


def _swiglu_chunks(hb, wg_ref, wu_ref, wd_ref, n_chunks, fc):
    acc = None
    for c in range(n_chunks):
        a = _dot(hb, wg_ref[:, c * fc:(c + 1) * fc])
        u = _dot(hb, wu_ref[:, c * fc:(c + 1) * fc])
        z = (a * jax.nn.sigmoid(a) * u).astype(BF16)
        part = _dot(z, wd_ref[c * fc:(c + 1) * fc, :])
        acc = part if acc is None else acc + part
    return acc


def _dense_ffn_kernel(x_ref, g_ref, sc_ref, sh_ref, gt_ref, wg_ref, wu_ref, wd_ref, o_ref, *, n_chunks, fc):
    x = x_ref[...]
    hb = _norm_mod(x, g_ref[...], sc_ref[0], sh_ref[0]).astype(BF16)
    o_ref[...] = x + gt_ref[0] * _swiglu_chunks(hb, wg_ref, wu_ref, wd_ref, n_chunks, fc)


def dense_ffn(x2d, g, sc, sh, gt, wg, wu, wd, layer, tiles_per_seq):
    n, d = x2d.shape
    tm = ROW_TILE
    fc = D_FF // 2
    mod_spec = _mod_spec(d, tiles_per_seq)
    return pl.pallas_call(
        functools.partial(_dense_ffn_kernel, n_chunks=D_FF // fc, fc=fc),
        grid=(n // tm,),
        in_specs=[pl.BlockSpec((tm, d), lambda i: (i, 0)), _const_spec((1, d)), mod_spec, mod_spec, mod_spec,
                  _layer_spec(wg, layer), _layer_spec(wu, layer), _layer_spec(wd, layer)],
        out_specs=pl.BlockSpec((tm, d), lambda i: (i, 0)),
        out_shape=jax.ShapeDtypeStruct((n, d), F32),
        compiler_params=_params("arbitrary"),
        name="dense_ffn",
    )(x2d, g.reshape(1, d), sc, sh, gt, wg, wu, wd)


def _router_kernel(x_ref, g_ref, sc_ref, sh_ref, r_ref, h_ref, l_ref):
    h = _norm_mod(x_ref[...], g_ref[...], sc_ref[0], sh_ref[0])
    h_ref[...] = h.astype(BF16)
    l_ref[...] = _dot(h, r_ref[...], HI)


def norm_router(x2d, g, sc, sh, router_pad, layer, tiles_per_seq):
    n, d = x2d.shape
    tm = ROW_TILE
    mod_spec = _mod_spec(d, tiles_per_seq)
    return pl.pallas_call(
        _router_kernel,
        grid=(n // tm,),
        in_specs=[pl.BlockSpec((tm, d), lambda i: (i, 0)), _const_spec((1, d)), mod_spec, mod_spec,
                  _layer_spec(router_pad, layer)],
        out_specs=[pl.BlockSpec((tm, d), lambda i: (i, 0)), pl.BlockSpec((tm, LANE), lambda i: (i, 0))],
        out_shape=[jax.ShapeDtypeStruct((n, d), BF16), jax.ShapeDtypeStruct((n, LANE), F32)],
        compiler_params=_params("arbitrary"),
        name="norm_router",
    )(x2d, g.reshape(1, d), sc, sh, router_pad)


def _expert_kernel(be_ref, nb_ref, xb_ref, wg_ref, wu_ref, wd_ref, o_ref, wg_bf, wu_bf, wd_bf):
    i = pl.program_id(0)
    new_expert = jnp.logical_or(i == 0, be_ref[i] != be_ref[jnp.maximum(i - 1, 0)])

    @pl.when(jnp.logical_and(new_expert, i < nb_ref[0]))
    def _():
        wg_bf[...] = wg_ref[0].astype(BF16)
        wu_bf[...] = wu_ref[0].astype(BF16)
        wd_bf[...] = wd_ref[0].astype(BF16)

    @pl.when(i < nb_ref[0])
    def _():
        o_ref[...] = _swiglu_chunks(xb_ref[...], wg_bf, wu_bf, wd_bf, 1, D_FF_EXPERT).astype(o_ref.dtype)

    @pl.when(i >= nb_ref[0])
    def _():
        o_ref[...] = jnp.zeros_like(o_ref)


def expert_ffn(xb, block_e, n_used, wg, wu, wd, layer, tm):
    npad, d = xb.shape
    f = wg.shape[3]
    grid_spec = pltpu.PrefetchScalarGridSpec(
        num_scalar_prefetch=2,
        grid=(npad // tm,),
        in_specs=[pl.BlockSpec((tm, d), lambda i, be, nb: (i, 0)),
                  pl.BlockSpec((None, 1, d, f), lambda i, be, nb: (layer, be[i], 0, 0)),
                  pl.BlockSpec((None, 1, d, f), lambda i, be, nb: (layer, be[i], 0, 0)),
                  pl.BlockSpec((None, 1, f, d), lambda i, be, nb: (layer, be[i], 0, 0))],
        out_specs=pl.BlockSpec((tm, d), lambda i, be, nb: (i, 0)),
        scratch_shapes=[pltpu.VMEM((d, f), BF16), pltpu.VMEM((d, f), BF16), pltpu.VMEM((f, d), BF16)],
    )
    return pl.pallas_call(
        _expert_kernel,
        grid_spec=grid_spec,
        out_shape=jax.ShapeDtypeStruct((npad, d), BF16),
        compiler_params=_params("arbitrary"),
        name="expert_ffn",
    )(block_e, n_used, xb, wg, wu, wd)


def moe_ffn(h_bf, logits, wg, wu, wd, layer):
    t, d = h_bf.shape
    top_logits, top_idx = lax.top_k(logits, TOP_K)
    gate = jax.nn.softmax(top_logits, axis=-1)
    tk = t * TOP_K
    onehot = (top_idx.reshape(1, tk) == jnp.arange(N_EXPERTS, dtype=jnp.int32)[:, None]).astype(jnp.int32)
    csum = jnp.cumsum(onehot, axis=1)
    counts = csum[:, -1]
    padded = (counts + MOE_BLOCK - 1) // MOE_BLOCK * MOE_BLOCK
    pad_end = jnp.cumsum(padded)
    pad_start = pad_end - padded
    pos = jnp.sum(onehot * (csum - 1 + pad_start[:, None]), axis=0)
    n_blocks = -(-tk // MOE_BLOCK) + N_EXPERTS
    buf_tok = jnp.zeros((n_blocks * MOE_BLOCK,), jnp.int32).at[pos].set(jnp.arange(tk, dtype=jnp.int32) // TOP_K)
    slot_of = pos.reshape(t, TOP_K)
    block_e = jnp.clip(jnp.searchsorted(pad_end, jnp.arange(n_blocks, dtype=jnp.int32) * MOE_BLOCK,
                                        side='right'), 0, N_EXPERTS - 1).astype(jnp.int32)
    n_used = (pad_end[-1] // MOE_BLOCK).astype(jnp.int32).reshape(1)
    yb = expert_ffn(h_bf[buf_tok], block_e, n_used, wg, wu, wd, layer, MOE_BLOCK)
    return gate[:, 0:1] * yb[slot_of[:, 0]] + gate[:, 1:2] * yb[slot_of[:, 1]]


def _final_norm_kernel(x_ref, g_ref, o_ref):
    x = x_ref[...]
    ms = jnp.mean(x * x, axis=-1, keepdims=True)
    o_ref[...] = x * lax.rsqrt(ms + EPS) * g_ref[...]


def final_norm(x3d, g, n_lat):
    bsz, t, d = x3d.shape
    tm = ROW_TILE
    return pl.pallas_call(
        _final_norm_kernel,
        grid=(bsz, n_lat // tm),
        in_specs=[pl.BlockSpec((None, tm, d), lambda b, i: (b, i, 0)), _const_spec((1, d))],
        out_specs=pl.BlockSpec((None, tm, d), lambda b, i: (b, i, 0)),
        out_shape=jax.ShapeDtypeStruct((bsz, n_lat, d), F32),
        compiler_params=_params("arbitrary", "arbitrary"),
        name="final_norm",
    )(x3d, g.reshape(1, d))


def _same_head(shape):
    r = lax.broadcasted_iota(jnp.int32, shape, 0) // HEAD_DIM
    c = lax.broadcasted_iota(jnp.int32, shape, 1) // HEAD_DIM
    return r == c


def _to_block_diag(x, mask):
    xb = x.astype(BF16)
    return jnp.where(mask, jnp.concatenate([xb] * N_HEADS, axis=0), jnp.zeros((), BF16))


def _token_masks(reverse):
    t = lax.broadcasted_iota(jnp.int32, (CHUNK, GROUP_W), 0)
    s = lax.broadcasted_iota(jnp.int32, (CHUNK, GROUP_W), 1) % HEAD_DIM
    if reverse:
        return s > t, s >= t, s == t
    return s < t, s <= t, s == t


def _cum_matrix(reverse):
    t = lax.broadcasted_iota(jnp.int32, (CHUNK, CHUNK), 0)
    s = lax.broadcasted_iota(jnp.int32, (CHUNK, CHUNK), 1)
    return jnp.where((s >= t) if reverse else (s <= t), 1.0, 0.0).astype(F32)


def _unit_tri_inverse(ls, bd_mask):
    t = lax.broadcasted_iota(jnp.int32, (CHUNK, GROUP_W), 0)
    s = lax.broadcasted_iota(jnp.int32, (CHUNK, GROUP_W), 1) % HEAD_DIM
    blk16 = (t // 16) == (s // 16)
    blk32 = (t // 32) == (s // 32)
    eye = jnp.where(t == s, 1.0, 0.0)
    ps = [jnp.where(blk16, -l, 0.0) for l in ls]
    xs = [eye + p for p in ps]
    for _ in range(3):
        ps = [_dot(p, _to_block_diag(p, bd_mask)) for p in ps]
        xs = [x + _dot(x, _to_block_diag(p, bd_mask)) for x, p in zip(xs, ps)]
    for off_diag in (jnp.logical_and(blk32, jnp.logical_not(blk16)), jnp.logical_not(blk32)):
        xe = [_dot(x, _to_block_diag(jnp.where(off_diag, l, 0.0), bd_mask)) for x, l in zip(xs, ls)]
        xs = [x - _dot(y, _to_block_diag(x, bd_mask)) for x, y in zip(xs, xe)]
    return xs


def _head_lane_expander(col0):
    r = lax.broadcasted_iota(jnp.int32, (LANE, GROUP_W), 0)
    c = lax.broadcasted_iota(jnp.int32, (LANE, GROUP_W), 1) // HEAD_DIM
    return jnp.where(r == col0 + c, 1.0, 0.0).astype(F32)


def _fwd_chunk(j, n_lat_chunks, n_chunks):
    n_ctx_chunks = n_chunks - n_lat_chunks
    return jnp.where(j < n_ctx_chunks, n_lat_chunks + j, j - n_ctx_chunks)


def _bwd_chunk(j, n_lat_chunks, n_chunks):
    return n_chunks - 1 - j


def _seq_edges(c, n_lat_chunks, n_chunks):
    first = jnp.logical_or(c == 0, c == n_lat_chunks)
    last = jnp.logical_or(c == n_lat_chunks - 1, c == n_chunks - 1)
    return first, last


def _shift_rows(x, prev, nxt, back):
    row = lax.broadcasted_iota(jnp.int32, (CHUNK, 1), 0)
    if back == -1:
        return jnp.where(row == CHUNK - 1, nxt[0:1, :], pltpu.roll(x, CHUNK - 1, 0))
    out = pltpu.roll(x, back, 0)
    n_prev = prev.shape[0]
    for i in range(back):
        out = jnp.where(row == i, prev[n_prev - back + i:n_prev - back + i + 1, :], out)
    return out


def _natural_scan_specs(nb, width, out_width, n_lat_chunks, n_chunks, last8):
    r8 = CHUNK // SUBLANE
    fwd = functools.partial(_fwd_chunk, n_lat_chunks=n_lat_chunks, n_chunks=n_chunks)
    bwd = functools.partial(_bwd_chunk, n_lat_chunks=n_lat_chunks, n_chunks=n_chunks)
    chunk = lambda f, w: pl.BlockSpec((nb, CHUNK, w), lambda b, j: (b, f(j), 0))
    prev = lambda f: pl.BlockSpec((nb, SUBLANE, width), lambda b, j: (b, jnp.maximum(f(j) * r8 - 1, 0), 0))
    nxt = lambda f: pl.BlockSpec((nb, SUBLANE, width), lambda b, j: (b, jnp.minimum((f(j) + 1) * r8, last8), 0))
    ins = [chunk(fwd, width), prev(fwd), nxt(fwd), chunk(bwd, width), prev(bwd), nxt(bwd)]
    return ins, chunk(fwd, out_width), chunk(bwd, out_width)


def _chunk_specs(nb, width, n_steps, first_row_block, column_view):
    if column_view:
        return (pl.BlockSpec((nb, CHUNK, width), lambda b, j: (b, 0, j)),
                pl.BlockSpec((nb, CHUNK, width), lambda b, j: (b, 0, n_steps - 1 - j)))
    return (pl.BlockSpec((nb, CHUNK, width), lambda b, j: (b, first_row_block + j, 0)),
            pl.BlockSpec((nb, CHUNK, width), lambda b, j: (b, first_row_block + n_steps - 1 - j, 0)))


def _halo_specs(nb, width, n_steps, first_row_block, column_view, last_halo):
    rh = CHUNK // HALO_ROWS
    fwd = lambda j: j
    bwd = lambda j: n_steps - 1 - j
    if column_view:
        prev = lambda f: pl.BlockSpec((nb, HALO_ROWS, width), lambda b, j: (b, rh - 1, jnp.maximum(f(j) - 1, 0)))
        nxt = lambda f: pl.BlockSpec((nb, HALO_ROWS, width), lambda b, j: (b, 0, jnp.minimum(f(j) + 1, n_steps - 1)))
    else:
        prev = lambda f: pl.BlockSpec((nb, HALO_ROWS, width),
                                      lambda b, j: (b, jnp.maximum((first_row_block + f(j)) * rh - 1, 0), 0))
        nxt = lambda f: pl.BlockSpec((nb, HALO_ROWS, width),
                                     lambda b, j: (b, jnp.minimum((first_row_block + f(j) + 1) * rh, last_halo), 0))
    return prev(fwd), nxt(fwd), prev(bwd), nxt(bwd)


def _rwkv_prepare(p, prev_row, next_row, prm, d, reverse):
    (mu_ref, wup_ref, w0_ref, aup_ref, a0_ref, gup_ref, kk_ref, ka_ref, rk_ref) = prm
    G = GROUP_W
    row = lax.broadcasted_iota(jnp.int32, (CHUNK, 1), 0)
    prev = jnp.where(row == 0, prev_row, pltpu.roll(p, 1, 0))
    nxt = jnp.where(row == CHUNK - 1, next_row, pltpu.roll(p, CHUNK - 1, 0))
    ps = p + mu_ref[0:1, :] * (prev - p) + mu_ref[1:2, :] * (nxt - p)
    r, k, v = ps[:, :G], ps[:, G:2 * G], ps[:, 2 * G:3 * G]
    lora_in = ps[:, 3 * G:3 * G + LANE]
    gate_in = ps[:, 3 * G + LANE:3 * G + 2 * LANE]
    bd_mask = _same_head((G, G))
    head_ones = jnp.where(bd_mask, 1.0, 0.0).astype(F32)
    log_w = -math.exp(-0.5) * jax.nn.sigmoid(_dot(jnp.tanh(lora_in), wup_ref[d]) + w0_ref[d:d + 1, :])
    a = jax.nn.sigmoid(_dot(lora_in, aup_ref[d]) + a0_ref[d:d + 1, :])
    kr = k * kk_ref[...]
    kk = kr * lax.rsqrt(_dot_sel(kr * kr, head_ones) + EPS)
    k_mod = k * (1.0 + (a - 1.0) * ka_ref[...])
    bonus = _dot_sel(r * k_mod * rk_ref[...], head_ones) * v
    lg = _sel_dot(_cum_matrix(reverse), log_w)
    e_neg = jnp.exp(-lg)
    at = kk * jnp.exp(lg - log_w)
    bt = -(kk * a) * e_neg
    kt = k_mod * e_neg
    rt = r * jnp.exp(lg)
    g_last = jnp.exp(lg[0:1, :] if reverse else lg[CHUNK - 1:CHUNK, :])
    strict, incl, _ = _token_masks(reverse)
    gate = None if reverse else _dot(jax.nn.sigmoid(gate_in), gup_ref[...])
    return dict(at=at, rt=rt, bt=bt, kt=kt, v=v, g_last=g_last, bonus=bonus, gate=gate, strict=strict, incl=incl)


def _rwkv_kernel(pf_ref, pfp_ref, pfn_ref, pb_ref, pbp_ref, pbn_ref,
                 mu_ref, wup_ref, w0_ref, aup_ref, a0_ref, gup_ref, kk_ref, ka_ref, rk_ref,
                 yf_ref, bf_ref, g_ref, yb_ref, bb_ref, s_ref, *, n_lat_chunks, n_chunks):
    j = pl.program_id(1)

    @pl.when(j == 0)
    def _():
        s_ref[...] = jnp.zeros_like(s_ref)

    prm = (mu_ref, wup_ref, w0_ref, aup_ref, a0_ref, gup_ref, kk_ref, ka_ref, rk_ref)
    nb = pf_ref.shape[0]
    bd_mask = _same_head((GROUP_W, GROUP_W))
    idx = [(i, d) for i in range(nb) for d in range(2)]
    cs = []
    for i, d in idx:
        chunk = (_bwd_chunk if d else _fwd_chunk)(j, n_lat_chunks, n_chunks)
        first, last = _seq_edges(chunk, n_lat_chunks, n_chunks)
        p_ref, pp_ref, pn_ref = (pb_ref, pbp_ref, pbn_ref) if d else (pf_ref, pfp_ref, pfn_ref)
        prev_row = jnp.where(first, 0.0, pp_ref[i, SUBLANE - 1:SUBLANE, :])
        next_row = jnp.where(last, 0.0, pn_ref[i, 0:1, :])
        cs.append(_rwkv_prepare(p_ref[i], prev_row, next_row, prm, d, bool(d)))
    ss = [s_ref[i, d] for i, d in idx]
    lhs = [jnp.concatenate([c['at'], c['rt']], axis=0) for c in cs]
    m_b = [_dot_nt(l, _to_block_diag(c['bt'], bd_mask)) for l, c in zip(lhs, cs)]
    m_k = [_dot_nt(l, _to_block_diag(c['kt'], bd_mask)) for l, c in zip(lhs, cs)]
    a_ak = [jnp.where(c['strict'], m[:CHUNK], 0.0) for c, m in zip(cs, m_k)]
    a_rb = [jnp.where(c['incl'], m[CHUNK:], 0.0) for c, m in zip(cs, m_b)]
    a_rk = [jnp.where(c['incl'], m[CHUNK:], 0.0) for c, m in zip(cs, m_k)]
    tinv = _unit_tri_inverse([jnp.where(c['strict'], -m[:CHUNK], 0.0) for c, m in zip(cs, m_b)], bd_mask)
    v_bd = [_to_block_diag(c['v'], bd_mask) for c in cs]
    by_state = [_dot_nt(l, s) for l, s in zip(lhs, ss)]
    by_value = [_dot(jnp.concatenate([a, b], axis=0), vb) for a, b, vb in zip(a_ak, a_rk, v_bd)]
    us = [_dot(t, _to_block_diag(m[:CHUNK] + n[:CHUNK], bd_mask)) for t, m, n in zip(tinv, by_state, by_value)]
    ys = [m[CHUNK:] + n[CHUNK:] + _dot(a, _to_block_diag(u, bd_mask))
          for m, n, a, u in zip(by_state, by_value, a_rb, us)]
    upds = [_dot_tn(jnp.concatenate([u, c['v']], axis=0), jnp.concatenate([c['bt'], c['kt']], axis=0))
            for u, c in zip(us, cs)]
    for n_, (i, d) in enumerate(idx):
        (yb_ref if d else yf_ref)[i] = ys[n_]
        (bb_ref if d else bf_ref)[i] = cs[n_]['bonus']
        if not d:
            g_ref[i] = cs[n_]['gate']
        s_ref[i, d] = cs[n_]['g_last'] * (ss[n_] + jnp.where(bd_mask, upds[n_], 0.0))


def rwkv7_mixer(p3d, n_lat, mu, w_up, w0, a_up, a0, g_up, k_k, k_a, r_k):
    bsz, t, width = p3d.shape
    G = GROUP_W
    n_chunks = t // CHUNK
    n_lat_chunks = n_lat // CHUNK
    mu_pad = _pad_cols(mu, width)
    wup = jnp.zeros((2, LANE, G), F32)
    aup = jnp.zeros((2, LANE, G), F32)
    for d in range(2):
        wup = wup.at[d, d * RWKV_LORA_W:(d + 1) * RWKV_LORA_W].set(w_up[d])
        lo = 2 * RWKV_LORA_W + d * RWKV_LORA_A
        aup = aup.at[d, lo:lo + RWKV_LORA_A].set(a_up[d])
    gup = jnp.zeros((LANE, G), F32).at[:RWKV_LORA_G].set(g_up)
    consts = [mu_pad, wup, w0, aup, a0, gup, k_k.reshape(1, G), k_a.reshape(1, G), r_k.reshape(1, G)]
    nb = math.gcd(SCAN_SAMPLES, bsz)
    ins, f_out, b_out = _natural_scan_specs(nb, width, G, n_lat_chunks, n_chunks, t // SUBLANE - 1)
    out_sds = jax.ShapeDtypeStruct((bsz, t, G), F32)
    yf, bf, gate, yb, bb = pl.pallas_call(
        functools.partial(_rwkv_kernel, n_lat_chunks=n_lat_chunks, n_chunks=n_chunks),
        grid=(bsz // nb, n_chunks),
        in_specs=ins + [_const_spec(c.shape) for c in consts],
        out_specs=[f_out] * 3 + [b_out] * 2,
        out_shape=[out_sds] * 5,
        scratch_shapes=[pltpu.VMEM((nb, 2, G, G), F32)],
        compiler_params=_params("arbitrary", "arbitrary"),
        name="rwkv_scan",
    )(p3d, p3d, p3d, p3d, p3d, p3d, *consts)
    return [a.reshape(bsz * t, G) for a in (yf, yb, bf, bb, gate)]


def _neg_expm1(x):
    series = -x * (1.0 + x * (1 / 2 + x * (1 / 6 + x * (1 / 24 + x * (1 / 120 + x * (1 / 720 + x / 5040))))))
    return jnp.where(x > -0.05, series, 1.0 - jnp.exp(x))


def _lru_chunk(p, prev8, next8, cw_ref, cb_ref, w_ref, bias_ref, lam_ref, h_ref, d, reverse):
    G = GROUP_W
    x = p[:, :G]
    u = (cw_ref[0:1, :] * _shift_rows(x, prev8, next8, 2) + cw_ref[1:2, :] * _shift_rows(x, prev8, next8, 1)
         + cw_ref[2:3, :] * x + cw_ref[3:4, :] * _shift_rows(x, prev8, next8, -1)) + cb_ref[...]
    pre = _dot(u, w_ref[d]) + bias_ref[d:d + 1, :]
    r = jax.nn.sigmoid(pre[:, :G])
    i = jax.nn.sigmoid(pre[:, G:])
    log_a = -LRU_C * r * jax.nn.softplus(-lam_ref[d:d + 1, :])
    a = jnp.exp(log_a)
    b = jnp.sqrt(_neg_expm1(2.0 * log_a)) * (i * u)
    row = lax.broadcasted_iota(jnp.int32, (CHUNK, 1), 0)
    k = 1
    while k < CHUNK:
        shift = CHUNK - k if reverse else k
        valid = (row < CHUNK - k) if reverse else (row >= k)
        b = jnp.where(valid, a * pltpu.roll(b, shift, 0) + b, b)
        a = jnp.where(valid, a * pltpu.roll(a, shift, 0), a)
        k *= 2
    h = b + a * h_ref[0:1, :]
    h_ref[0:1, :] = h[0:1, :] if reverse else h[CHUNK - 1:CHUNK, :]
    return h


def _lru_kernel(pf_ref, pfp_ref, pfn_ref, pb_ref, pbp_ref, pbn_ref, cw_ref, cb_ref, w_ref, bias_ref, lam_ref,
                hf_ref, hb_ref, st_ref, *, n_lat_chunks, n_chunks):
    j = pl.program_id(1)

    @pl.when(j == 0)
    def _():
        st_ref[...] = jnp.zeros_like(st_ref)

    G = GROUP_W
    for i in range(pf_ref.shape[0]):
        first, last = _seq_edges(_fwd_chunk(j, n_lat_chunks, n_chunks), n_lat_chunks, n_chunks)
        prev8 = jnp.where(first, 0.0, pfp_ref[i, :, :G])
        next8 = jnp.where(last, 0.0, pfn_ref[i, :, :G])
        hf_ref[i] = _lru_chunk(pf_ref[i], prev8, next8, cw_ref, cb_ref, w_ref, bias_ref, lam_ref,
                               st_ref.at[i, 0], 0, False)
        first, last = _seq_edges(_bwd_chunk(j, n_lat_chunks, n_chunks), n_lat_chunks, n_chunks)
        prev8 = jnp.where(first, 0.0, pbp_ref[i, :, :G])
        next8 = jnp.where(last, 0.0, pbn_ref[i, :, :G])
        hb_ref[i] = _lru_chunk(pb_ref[i], prev8, next8, cw_ref, cb_ref, w_ref, bias_ref, lam_ref,
                               st_ref.at[i, 1], 1, True)


def rglru_mixer(p3d, n_lat, conv_w, conv_b, w_a, b_a, w_x, b_x, lam):
    bsz, t, width = p3d.shape
    G = GROUP_W
    n_chunks = t // CHUNK
    n_lat_chunks = n_lat // CHUNK
    w = jnp.zeros((2, G, 2 * G), F32)
    for h in range(N_HEADS):
        sl = slice(h * HEAD_DIM, (h + 1) * HEAD_DIM)
        w = w.at[:, sl, sl].set(w_a[:, h])
        w = w.at[:, sl, G + h * HEAD_DIM:G + (h + 1) * HEAD_DIM].set(w_x[:, h])
    consts = [jnp.pad(conv_w, ((0, SUBLANE - CONV_W), (0, 0))), conv_b.reshape(1, G), w,
              jnp.concatenate([b_a, b_x], axis=1), lam]
    nb = math.gcd(SCAN_SAMPLES, bsz)
    ins, f_out, b_out = _natural_scan_specs(nb, width, G, n_lat_chunks, n_chunks, t // SUBLANE - 1)
    h_sds = jax.ShapeDtypeStruct((bsz, t, G), F32)
    hf, hb = pl.pallas_call(
        functools.partial(_lru_kernel, n_lat_chunks=n_lat_chunks, n_chunks=n_chunks),
        grid=(bsz // nb, n_chunks),
        in_specs=ins + [_const_spec(c.shape) for c in consts],
        out_specs=[f_out, b_out],
        out_shape=[h_sds, h_sds],
        scratch_shapes=[pltpu.VMEM((nb, 2, SUBLANE, G), F32)],
        compiler_params=_params("arbitrary", "arbitrary"),
        name="lru_scan",
    )(p3d, p3d, p3d, p3d, p3d, p3d, *consts)
    return hf.reshape(bsz * t, G), hb.reshape(bsz * t, G)


def _gdn_prepare(p, gates, prev, nxt, cw_ref, al_ref, dt_ref, d, reverse):
    G = GROUP_W
    x = p[:, :3 * G].astype(F32)
    prev = prev[:, :3 * G].astype(F32)
    nxt = nxt[:, :3 * G].astype(F32)
    conv = (cw_ref[0:1, :] * _shift_rows(x, prev, nxt, 2) + cw_ref[1:2, :] * _shift_rows(x, prev, nxt, 1)
            + cw_ref[2:3, :] * x + cw_ref[3:4, :] * _shift_rows(x, prev, nxt, -1))
    qkv = conv * jax.nn.sigmoid(conv)
    bd_mask = _same_head((G, G))
    head_ones = jnp.where(bd_mask, 1.0, 0.0).astype(F32)
    q, k, v = qkv[:, :G], qkv[:, G:2 * G], qkv[:, 2 * G:]
    q = q * lax.rsqrt(_dot_sel(q * q, head_ones) + EPS) * (HEAD_DIM ** -0.5)
    k = k * lax.rsqrt(_dot_sel(k * k, head_ones) + EPS)
    la = -al_ref[d:d + 1, :] * jax.nn.softplus(_dot_sel(gates, _head_lane_expander(d * N_HEADS)) + dt_ref[d:d + 1, :])
    beta = jax.nn.sigmoid(_dot_sel(gates, _head_lane_expander((2 + d) * N_HEADS)))
    gam = _sel_dot(_cum_matrix(reverse), la)
    strict, incl, diag = _token_masks(reverse)
    gam_row = jnp.sum(jnp.where(diag, gam, 0.0), axis=0, keepdims=True)
    decay = jnp.exp(jnp.where(incl, gam - gam_row, -jnp.inf))
    k_beta = k * beta
    k_bd = _to_block_diag(k, bd_mask)
    lower = jnp.where(strict, _dot_nt(k_beta, k_bd) * decay, 0.0)
    e_gam = jnp.exp(gam)
    gam_last = gam[0:1, :] if reverse else gam[CHUNK - 1:CHUNK, :]
    return dict(lower=lower, vb=v * beta, kbe=k_beta * e_gam, qk=_dot_nt(q, k_bd) * decay, q_dec=q * e_gam,
                k_dec=k * jnp.exp(gam_last - gam), g_last=jnp.exp(gam_last))


def _gdn_kernel(pf_ref, gf_ref, pfp_ref, pfn_ref, pb_ref, gb_ref, pbp_ref, pbn_ref, cw_ref, al_ref, dt_ref,
                s0_ref, of_ref, ob_ref, s_ref, *, n_steps):
    j = pl.program_id(1)

    @pl.when(j == 0)
    def _():
        s_ref[...] = s0_ref[...]

    bd_mask = _same_head((GROUP_W, GROUP_W))
    cs, ss = [], []
    for i in range(pf_ref.shape[0]):
        prev = jnp.where(j == 0, jnp.zeros((), BF16), pfp_ref[i])
        nxt = jnp.where(j == n_steps - 1, jnp.zeros((), BF16), pfn_ref[i])
        cs.append(_gdn_prepare(pf_ref[i], gf_ref[i], prev, nxt, cw_ref, al_ref, dt_ref, 0, False))
        prev = jnp.where(j == n_steps - 1, jnp.zeros((), BF16), pbp_ref[i])
        nxt = jnp.where(j == 0, jnp.zeros((), BF16), pbn_ref[i])
        cs.append(_gdn_prepare(pb_ref[i], gb_ref[i], prev, nxt, cw_ref, al_ref, dt_ref, 1, True))
        ss += [s_ref[i, 0], s_ref[i, 1]]
    tinv = _unit_tri_inverse([c['lower'] for c in cs], bd_mask)
    G = GROUP_W
    uw = [_dot(t, jnp.concatenate([_to_block_diag(c['vb'], bd_mask), _to_block_diag(c['kbe'], bd_mask)], axis=1))
          for t, c in zip(tinv, cs)]
    by_state = [_dot(jnp.concatenate([m[:, G:], c['q_dec']], axis=0), s) for m, c, s in zip(uw, cs, ss)]
    vn = [m[:, :G] - n[:CHUNK] for m, n in zip(uw, by_state)]
    outs = [n[CHUNK:] + _dot(c['qk'], _to_block_diag(v, bd_mask)) for n, c, v in zip(by_state, cs, vn)]
    sn = [c['g_last'] * s + jnp.where(bd_mask, _dot_tn(c['k_dec'], v), 0.0) for c, s, v in zip(cs, ss, vn)]
    for i in range(pf_ref.shape[0]):
        of_ref[i] = outs[2 * i].astype(of_ref.dtype)
        ob_ref[i] = outs[2 * i + 1].astype(ob_ref.dtype)
        s_ref[i, 0] = sn[2 * i]
        s_ref[i, 1] = sn[2 * i + 1]


def gdn_scan(p, gates, n_steps, first_row_block, column_view, last_halo, cw, al, dt, s0):
    bsz = p.shape[0]
    G = GROUP_W
    nb = math.gcd(SCAN_SAMPLES, bsz)
    f_in, b_in = _chunk_specs(nb, 4 * G, n_steps, first_row_block, column_view)
    f_g, b_g = _chunk_specs(nb, LANE, n_steps, first_row_block, column_view)
    fp, fn, bp, bn = _halo_specs(nb, 4 * G, n_steps, first_row_block, column_view, last_halo)
    f_out, b_out = _chunk_specs(nb, G, n_steps, 0, column_view)
    h_sds = jax.ShapeDtypeStruct((bsz, CHUNK, GRID_W * G) if column_view else (bsz, n_steps * CHUNK, G), BF16)
    state_spec = pl.BlockSpec((nb, 2, G, G), lambda b, j: (b, 0, 0, 0))
    return pl.pallas_call(
        functools.partial(_gdn_kernel, n_steps=n_steps),
        grid=(bsz // nb, n_steps),
        in_specs=[f_in, f_g, fp, fn, b_in, b_g, bp, bn, _const_spec(cw.shape), _const_spec(al.shape),
                  _const_spec(dt.shape), state_spec],
        out_specs=[f_out, b_out, state_spec],
        out_shape=[h_sds, h_sds, jax.ShapeDtypeStruct(s0.shape, F32)],
        compiler_params=_params("arbitrary", "arbitrary"),
        name="gdn_scan",
    )(p, gates, p, p, p, gates, p, p, cw, al, dt, s0)


def gdn_mixer(p3d, g3d, n_lat, conv_w, a_log, dt_bias):
    bsz, t, width = p3d.shape
    G = GROUP_W
    n_ctx = t - n_lat
    assert n_lat == CHUNK * GRID_W
    al = jnp.repeat(jnp.exp(a_log), HEAD_DIM, axis=1)
    dt = jnp.repeat(dt_bias, HEAD_DIM, axis=1)
    cw = jnp.pad(conv_w, ((0, SUBLANE - CONV_W), (0, 0)))
    s0 = jnp.zeros((bsz, 2, G, G), F32)
    last_halo = t // HALO_ROWS - 1
    of_c, ob_c, s1 = gdn_scan(p3d, g3d, n_ctx // CHUNK, n_lat // CHUNK, False, last_halo, cw, al, dt, s0)
    of_l, ob_l, _ = gdn_scan(p3d.reshape(bsz, t // GRID_W, GRID_W * width), g3d.reshape(bsz, t // GRID_W, GRID_W * LANE),
                             GRID_W, 0, True, last_halo, cw, al, dt, s1)
    return (of_l.reshape(bsz * n_lat, G), ob_l.reshape(bsz * n_lat, G),
            of_c.reshape(bsz * n_ctx, G), ob_c.reshape(bsz * n_ctx, G))


def _head_max(x):
    lane_head = lax.broadcasted_iota(jnp.int32, x.shape, 1) // HEAD_DIM
    out = jnp.zeros_like(x)
    for h in range(N_HEADS):
        mh = jnp.max(jnp.where(lane_head == h, x, -jnp.inf), axis=-1, keepdims=True)
        out = jnp.where(lane_head == h, mh, out)
    return out


def _mlstm_prepare(p, gates, ib_ref, fb_ref, d, reverse):
    G = GROUP_W
    p = p[:, :3 * G].astype(F32)
    q, k, v = p[:, :G], p[:, G:2 * G] * (HEAD_DIM ** -0.5), p[:, 2 * G:3 * G]
    bd_mask = _same_head((G, G))
    li = _dot_sel(gates, _head_lane_expander(d * N_HEADS)) + ib_ref[d:d + 1, :]
    lf = jax.nn.log_sigmoid(_dot_sel(gates, _head_lane_expander((2 + d) * N_HEADS)) + fb_ref[d:d + 1, :])
    b = _sel_dot(_cum_matrix(reverse), lf)
    _, incl, diag = _token_masks(reverse)
    g = li - b
    g_row = jnp.sum(jnp.where(diag, g, 0.0), axis=0, keepdims=True)
    log_d = jnp.where(incl, b + g_row, -jnp.inf)
    b_last = b[0:1, :] if reverse else b[CHUNK - 1:CHUNK, :]
    log_end = b_last + g
    return dict(q=q, k=k, v=v, b=b, log_d=log_d, m_intra=_head_max(log_d), b_last=b_last, log_end=log_end,
                end_max=jnp.max(log_end, axis=0, keepdims=True), qk=_dot_nt(q, _to_block_diag(k, bd_mask)))


def _mlstm_kernel(pf_ref, gf_ref, pb_ref, gb_ref, ib_ref, fb_ref, c0_ref, nm0_ref, hf_ref, hb_ref, c_ref, nm_ref):
    @pl.when(pl.program_id(1) == 0)
    def _():
        c_ref[...] = c0_ref[...]
        nm_ref[...] = nm0_ref[...]

    nb = pf_ref.shape[0]
    bd_mask = _same_head((GROUP_W, GROUP_W))
    head_ones = jnp.where(bd_mask, 1.0, 0.0).astype(F32)
    idx = [(i, d) for i in range(nb) for d in range(2)]
    cs = [_mlstm_prepare((pb_ref if d else pf_ref)[i], (gb_ref if d else gf_ref)[i], ib_ref, fb_ref, d, bool(d))
          for i, d in idx]
    mems = [c_ref[i, d] for i, d in idx]
    nrms = [nm_ref[i, d, 0:1, :] for i, d in idx]
    m_prevs = [nm_ref[i, d, 1:2, :] for i, d in idx]
    m_inters = [c['b'] + m for c, m in zip(cs, m_prevs)]
    m_ts = [jnp.maximum(mi, c['m_intra']) for mi, c in zip(m_inters, cs)]
    w_inters = [jnp.exp(mi - mt) for mi, mt in zip(m_inters, m_ts)]
    pms = [jnp.exp(c['log_d'] - mt) * c['qk'] for c, mt in zip(cs, m_ts)]
    inter = [_dot_nt(c['q'], mem) for c, mem in zip(cs, mems)]
    nums = [_dot(pm, _to_block_diag(c['v'], bd_mask)) + w * it for pm, c, w, it in zip(pms, cs, w_inters, inter)]
    dens = [_dot_sel(pm + w * (c['q'] * nrm), head_ones) for pm, w, c, nrm in zip(pms, w_inters, cs, nrms)]
    hs = [num / jnp.maximum(jnp.abs(den), jnp.exp(-mt)) for num, den, mt in zip(nums, dens, m_ts)]
    m_news = [jnp.maximum(c['b_last'] + m, c['end_max']) for c, m in zip(cs, m_prevs)]
    w_prevs = [jnp.exp(c['b_last'] + m - mn) for c, m, mn in zip(cs, m_prevs, m_news)]
    w_news = [jnp.exp(c['log_end'] - mn) for c, mn in zip(cs, m_news)]
    upds = [_dot_tn(wn * c['v'], c['k']) for wn, c in zip(w_news, cs)]
    for n_, (i, d) in enumerate(idx):
        (hb_ref if d else hf_ref)[i] = hs[n_].astype(hf_ref.dtype)
        c_ref[i, d] = w_prevs[n_] * mems[n_] + jnp.where(bd_mask, upds[n_], 0.0)
        nm_ref[i, d, 0:1, :] = w_prevs[n_] * nrms[n_] + jnp.sum(w_news[n_] * cs[n_]['k'], axis=0, keepdims=True)
        nm_ref[i, d, 1:2, :] = m_news[n_]


def mlstm_scan(p, gates, n_steps, first_row_block, column_view, ib, fb, c0, nm0):
    bsz = p.shape[0]
    G = GROUP_W
    nb = math.gcd(MLSTM_SCAN_SAMPLES, bsz)
    f_in, b_in = _chunk_specs(nb, 4 * G, n_steps, first_row_block, column_view)
    f_g, b_g = _chunk_specs(nb, LANE, n_steps, first_row_block, column_view)
    f_out, b_out = _chunk_specs(nb, G, n_steps, 0, column_view)
    h_sds = jax.ShapeDtypeStruct((bsz, CHUNK, GRID_W * G) if column_view else (bsz, n_steps * CHUNK, G), BF16)
    state_specs = [pl.BlockSpec((nb, 2, G, G), lambda b, j: (b, 0, 0, 0)),
                   pl.BlockSpec((nb, 2, SUBLANE, G), lambda b, j: (b, 0, 0, 0))]
    return pl.pallas_call(
        _mlstm_kernel,
        grid=(bsz // nb, n_steps),
        in_specs=[f_in, f_g, b_in, b_g, _const_spec(ib.shape), _const_spec(fb.shape)] + state_specs,
        out_specs=[f_out, b_out] + state_specs,
        out_shape=[h_sds, h_sds, jax.ShapeDtypeStruct(c0.shape, F32), jax.ShapeDtypeStruct(nm0.shape, F32)],
        compiler_params=_params("arbitrary", "arbitrary"),
        name="mlstm_scan",
    )(p, gates, p, gates, ib, fb, c0, nm0)


def mlstm_mixer(p3d, g3d, n_lat, i_b, f_b):
    bsz, t, width = p3d.shape
    G = GROUP_W
    n_ctx = t - n_lat
    assert n_lat == CHUNK * GRID_W
    ib = jnp.repeat(i_b, HEAD_DIM, axis=1)
    fb = jnp.repeat(f_b, HEAD_DIM, axis=1)
    c0 = jnp.zeros((bsz, 2, G, G), F32)
    nm0 = jnp.zeros((bsz, 2, SUBLANE, G), F32)
    hf_c, hb_c, c1, nm1 = mlstm_scan(p3d, g3d, n_ctx // CHUNK, n_lat // CHUNK, False, ib, fb, c0, nm0)
    hf_l, hb_l, _, _ = mlstm_scan(p3d.reshape(bsz, t // GRID_W, GRID_W * width),
                                  g3d.reshape(bsz, t // GRID_W, GRID_W * LANE), GRID_W, 0, True, ib, fb, c1, nm1)
    return (hf_l.reshape(bsz * n_lat, G), hb_l.reshape(bsz * n_lat, G),
            hf_c.reshape(bsz * n_ctx, G), hb_c.reshape(bsz * n_ctx, G))


def kernel(x, c, ctx, c_ctx, mod_w, mod_b, norm_mix_g, norm_ffn_g, w_in, w_out, lru_conv_w, lru_conv_b, lru_w_a, lru_b_a, lru_w_x, lru_b_x, lru_lambda, rwkv_mu, rwkv_w_up, rwkv_w0, rwkv_a_up, rwkv_a0, rwkv_g_up, rwkv_k_k, rwkv_k_a, rwkv_r_k, rwkv_ln_g, rwkv_ln_b, mlstm_i_b, mlstm_f_b, mlstm_norm_g, gdn_conv_w, gdn_a_log, gdn_dt_bias, gdn_norm_g, ffn_w_gate, ffn_w_up, ffn_w_down, moe_router, moe_w_gate, moe_w_up, moe_w_down, final_norm_g):
    bsz, seq, d = x.shape
    n_ctx = ctx.shape[1]
    t = n_ctx + seq
    n = bsz * t
    tiles_per_seq = t // ROW_TILE
    assert n_ctx == ROW_TILE and seq % ROW_TILE == 0
    G = GROUP_W
    x2 = jnp.concatenate([x, ctx], axis=1).reshape(n, d)
    mod = jnp.einsum('bd,ldk->lbk', jax.nn.silu(c), mod_w) + mod_b[:, None]
    mod_c = jnp.einsum('d,ldk->lk', jax.nn.silu(c_ctx), mod_w) + mod_b
    mods = jnp.stack([mod, jnp.broadcast_to(mod_c[:, None], mod.shape)], axis=2).reshape(DEPTH, 2 * bsz, 6, 1, d)
    col_groups = ((0, OFF_B), (OFF_B, OFF_C), (OFF_C, OFF_C + 4 * G), (OFF_C + 4 * G, OFF_D),
                  (OFF_D, OFF_D + 4 * G), (OFF_D + 4 * G, IN_COLS))
    w_groups = [jnp.pad(w_in[:, :, lo:hi], ((0, 0), (0, 0), (0, _round_up(hi - lo, LANE) - (hi - lo)))).astype(BF16)
                for lo, hi in col_groups]
    proj_dtypes = (F32, F32, BF16, F32, BF16, F32)
    w_out_bf = w_out.astype(BF16)
    ffn_bf = [w.astype(BF16) for w in (ffn_w_gate, ffn_w_up, ffn_w_down)]
    router_pad = jnp.pad(moe_router, ((0, 0), (0, 0), (0, LANE - N_EXPERTS)))
    is_latent = (jnp.arange(t) < seq)[None, :, None]
    for l in range(DEPTH):
        sh1, sc1, gt1, sh2, sc2, gt2 = [mods[l, :, j] for j in range(6)]
        pa, pb, pc, pcg, pd, pdg = norm_proj(x2, norm_mix_g[l], sc1, sh1, w_groups, proj_dtypes, l, tiles_per_seq)
        lru = rglru_mixer(pa.reshape(bsz, t, -1), seq, lru_conv_w[l], lru_conv_b[l], lru_w_a[l], lru_b_a[l],
                          lru_w_x[l], lru_b_x[l], lru_lambda[l])
        rwkv = rwkv7_mixer(pb.reshape(bsz, t, -1), seq, rwkv_mu[l], rwkv_w_up[l], rwkv_w0[l], rwkv_a_up[l],
                           rwkv_a0[l], rwkv_g_up[l], rwkv_k_k[l], rwkv_k_a[l], rwkv_r_k[l])
        mlstm = mlstm_mixer(pc.reshape(bsz, t, -1), pcg.reshape(bsz, t, -1), seq, mlstm_i_b[l], mlstm_f_b[l])
        gdn = gdn_mixer(pd.reshape(bsz, t, -1), pdg.reshape(bsz, t, -1), seq, gdn_conv_w[l], gdn_a_log[l],
                        gdn_dt_bias[l])
        x2 = out_proj(lru, rwkv, mlstm, gdn, pa, pc, pd, rwkv_ln_g[l], rwkv_ln_b[l], mlstm_norm_g[l],
                      gdn_norm_g[l], w_out_bf, l, x2, gt1, tiles_per_seq)
        if l % 2 == 0:
            x2 = dense_ffn(x2, norm_ffn_g[l], sc2, sh2, gt2, *ffn_bf, l // 2, tiles_per_seq)
        else:
            h2, lg = norm_router(x2, norm_ffn_g[l], sc2, sh2, router_pad, l // 2, tiles_per_seq)
            y = moe_ffn(h2, lg[:, :N_EXPERTS], moe_w_gate, moe_w_up, moe_w_down, l // 2)
            gt_pair = gt2.reshape(bsz, 2, 1, d)
            x2 = (x2.reshape(bsz, t, d)
                  + jnp.where(is_latent, gt_pair[:, 0], gt_pair[:, 1]) * y.reshape(bsz, t, d)).reshape(n, d)
    return final_norm(x2.reshape(bsz, t, d), final_norm_g, seq)
```
